```python
import jax
import jax.numpy as jnp
from jax import lax
import numpy as np

D_MODEL = 1024
BATCH = 8
SEQ = 2048
DEPTH = 2

GRID_W = 64
CTX_LEN = 256
EPS = 1e-6
N_MOD = 6

SSD_HEADS = 16
SSD_HEAD_DIM = 64
SSD_INNER = SSD_HEADS * SSD_HEAD_DIM
SSD_GROUPS = 4
SSD_STATE = 128
SSD_CONV = 5
SSD_CHUNK = 128
SSD_BC = SSD_GROUPS * SSD_STATE
SSD_CONV_DIM = SSD_INNER + 2 * SSD_BC
SSD_STATE_COLS = SSD_CONV_DIM + 2 * SSD_HEADS

CF_CH = 1024
CF_KERNEL = 31

AB_IN = SSD_INNER + SSD_STATE_COLS + 2 * CF_CH
AB_OUT = SSD_INNER + CF_CH

HG_HEADS = 8
HG_DK = 128
HG_DV = D_MODEL // HG_HEADS
HG_KEY = HG_HEADS * HG_DK
HG_VAL = HG_HEADS * HG_DV
HG_CHUNK = 64
HG_IN = HG_KEY + 2 * HG_VAL + 2 * HG_KEY

N_EXPERTS = 16
N_EXPERT_GROUPS = 4
EXPERTS_PER_GROUP = N_EXPERTS // N_EXPERT_GROUPS
TOP_K = 2
D_EXPERT = 512

N_EVEN = (DEPTH + 1) // 2
N_ODD = DEPTH // 2

kernel_name = "hybrid_ssd_conformer_hgrn2_moe_dit"


def rmsnorm(x, g):
    xf = x.astype(jnp.float32)
    y = xf * lax.rsqrt(jnp.mean(xf * xf, axis=-1, keepdims=True) + EPS)
    return (y * g.astype(jnp.float32)).astype(x.dtype)


def layernorm(x, g, b):
    xf = x.astype(jnp.float32)
    mu = jnp.mean(xf, axis=-1, keepdims=True)
    xc = xf - mu
    var = jnp.mean(xc * xc, axis=-1, keepdims=True)
    return (xc * lax.rsqrt(var + EPS) * g.astype(jnp.float32) + b.astype(jnp.float32)).astype(x.dtype)


def modulation(cond, w, b):
    m = jax.nn.silu(cond) @ w + b
    return m.reshape(cond.shape[:-1] + (N_MOD, D_MODEL))


def dwconv(x, w, b):
    k = w.shape[0]
    y = lax.conv_general_dilated(x, w[:, None, :].astype(x.dtype), window_strides=(1,),
                                 padding=[(k // 2, k // 2)], dimension_numbers=("NWC", "WIO", "NWC"),
                                 feature_group_count=x.shape[-1])
    return y + b.astype(y.dtype)


def _flip(t):
    return jnp.flip(t, axis=1)


def _same(t):
    return t


def to_col_major(t, rows):
    b, _, ch = t.shape
    return t.reshape(b, rows, GRID_W, ch).transpose(0, 2, 1, 3).reshape(b, rows * GRID_W, ch)


def from_col_major(t, rows):
    b, _, ch = t.shape
    return t.reshape(b, GRID_W, rows, ch).transpose(0, 2, 1, 3).reshape(b, rows * GRID_W, ch)


def _carry_step(s, inp):
    dec, st = inp
    return dec * s + st, s


def ssd_scan(xs, dt, a_neg, bm, cm, s0, with_output):
    bsz, seqlen, nh, hp = xs.shape
    ng, ns = bm.shape[2], bm.shape[3]
    rep = nh // ng
    nc = seqlen // SSD_CHUNK
    la = (dt * a_neg).reshape(bsz, nc, SSD_CHUNK, ng, rep)
    acum = jnp.cumsum(la, axis=2)
    xdt = (xs.astype(jnp.float32) * dt[..., None]).reshape(bsz, nc, SSD_CHUNK, ng, rep, hp)
    bc = bm.astype(jnp.float32).reshape(bsz, nc, SSD_CHUNK, ng, ns)
    to_end = jnp.exp(acum[:, :, -1:] - acum)
    chunk_states = jnp.einsum("bcsgn,bcsgr,bcsgrp->bcgrpn", bc, to_end, xdt).reshape(bsz, nc, nh, hp, ns)
    chunk_decay = jnp.exp(acum[:, :, -1]).reshape(bsz, nc, nh)[..., None, None]
    s_final, s_in = lax.scan(_carry_step, s0, (jnp.moveaxis(chunk_decay, 1, 0), jnp.moveaxis(chunk_states, 1, 0)))
    if not with_output:
        return None, s_final
    s_in = jnp.moveaxis(s_in, 0, 1).reshape(bsz, nc, ng, rep, hp, ns)
    cc = cm.astype(jnp.float32).reshape(bsz, nc, SSD_CHUNK, ng, ns)
    causal = jnp.tril(jnp.ones((SSD_CHUNK, SSD_CHUNK), dtype=bool))[:, :, None, None]
    seg = acum[:, :, :, None] - acum[:, :, None, :]
    decay = jnp.exp(jnp.where(causal, seg, -jnp.inf))
    cb = jnp.einsum("bclgn,bcsgn->bclsg", cc, bc)
    y_diag = jnp.einsum("bclsg,bclsgr,bcsgrp->bclgrp", cb, decay, xdt)
    y_off = jnp.einsum("bclgn,bcgrpn,bclgr->bclgrp", cc, s_in, jnp.exp(acum))
    return (y_diag + y_off).reshape(bsz, seqlen, nh, hp), s_final


def ssd_inputs(p_state, conv_w, conv_b, dt_bias):
    b, n, _ = p_state.shape
    xbc = jax.nn.silu(dwconv(p_state[..., :SSD_CONV_DIM], conv_w, conv_b))
    xs = xbc[..., :SSD_INNER].reshape(b, n, SSD_HEADS, SSD_HEAD_DIM)
    bm = xbc[..., SSD_INNER:SSD_INNER + SSD_BC].reshape(b, n, SSD_GROUPS, SSD_STATE)
    cm = xbc[..., SSD_INNER + SSD_BC:].reshape(b, n, SSD_GROUPS, SSD_STATE)
    dt_raw = p_state[..., SSD_CONV_DIM:].astype(jnp.float32).reshape(b, n, 2, SSD_HEADS)
    dt = jax.nn.softplus(dt_raw + dt_bias.astype(jnp.float32))
    return xs, bm, cm, dt


def ssd_readout(y, xs, z, d_skip, norm_g):
    b, n = y.shape[0], y.shape[1]
    y = y + d_skip.astype(jnp.float32)[:, None] * xs.astype(jnp.float32)
    y = y.reshape(b, n, SSD_INNER) * jax.nn.silu(z.astype(jnp.float32))
    return rmsnorm(y, norm_g)


def conformer_conv(v, gate, dw_w, dw_b, ln_g, ln_b, rows):
    u = v * jax.nn.sigmoid(gate)
    b, n, ch = u.shape
    if rows is not None:
        u = u.reshape(b * rows, GRID_W, ch)
    u = dwconv(u, dw_w, dw_b).reshape(b, n, ch)
    return jax.nn.silu(layernorm(u, ln_g, ln_b))


def even_mixer(hc, hl, w_in, conv_w, conv_b, dt_bias, a_log, d_skip, ssd_g,
               cf_w, cf_b, cf_lng, cf_lnb, w_out, rows, need_ctx):
    a_neg = -jnp.exp(a_log.astype(jnp.float32))
    s_lo, s_hi = SSD_INNER, SSD_INNER + SSD_STATE_COLS
    pl = hl @ w_in
    if need_ctx:
        pc = hc @ w_in
        pc_state = pc[..., s_lo:s_hi]
    else:
        pc_state = hc @ w_in[:, s_lo:s_hi]
    xs_l, b_l, c_l, dt_l = ssd_inputs(pl[..., s_lo:s_hi], conv_w, conv_b, dt_bias)
    xs_c, b_c, c_c, dt_c = ssd_inputs(pc_state, conv_w, conv_b, dt_bias)
    bsz = hl.shape[0]
    ys_c, ys_l = [], []
    for d in range(2):
        flip = _flip if d else _same
        s0 = jnp.zeros((bsz, SSD_HEADS, SSD_HEAD_DIM, SSD_STATE), jnp.float32)
        y_c, s_ctx = ssd_scan(flip(xs_c), flip(dt_c[:, :, d]), a_neg[d], flip(b_c), flip(c_c), s0, need_ctx)
        y_l, _ = ssd_scan(flip(xs_l), flip(dt_l[:, :, d]), a_neg[d], flip(b_l), flip(c_l), s_ctx, True)
        ys_l.append(flip(y_l))
        if need_ctx:
            ys_c.append(flip(y_c))
    cf0 = SSD_INNER + SSD_STATE_COLS
    o_ssd = ssd_readout(ys_l[0] + ys_l[1], xs_l, pl[..., :SSD_INNER], d_skip, ssd_g)
    o_cf = conformer_conv(pl[..., cf0:cf0 + CF_CH], pl[..., cf0 + CF_CH:], cf_w, cf_b, cf_lng, cf_lnb, rows)
    y_lat = (jnp.concatenate([o_ssd, o_cf.astype(o_ssd.dtype)], axis=-1) @ w_out).astype(hl.dtype)
    y_ctx = None
    if need_ctx:
        oc_ssd = ssd_readout(ys_c[0] + ys_c[1], xs_c, pc[..., :SSD_INNER], d_skip, ssd_g)
        oc_cf = conformer_conv(pc[..., cf0:cf0 + CF_CH], pc[..., cf0 + CF_CH:], cf_w, cf_b, cf_lng, cf_lnb, None)
        y_ctx = (jnp.concatenate([oc_ssd, oc_cf.astype(oc_ssd.dtype)], axis=-1) @ w_out).astype(hc.dtype)
    return y_ctx, y_lat


def hgrn2_scan(q, k, v, logf, s0, with_output):
    bsz, seqlen, nh, dk = k.shape
    dv = v.shape[-1]
    nc = seqlen // HG_CHUNK

    def chunks(t):
        return t.astype(jnp.float32).reshape(bsz, nc, HG_CHUNK, nh, t.shape[-1])

    kc, vc = chunks(k), chunks(v)
    gcum = jnp.cumsum(chunks(logf), axis=2)
    g_end = gcum[:, :, -1]
    k_end = kc * jnp.exp(g_end[:, :, None] - gcum)
    chunk_states = jnp.einsum("bcshk,bcshv->bchkv", k_end, vc)
    s_final, s_in = lax.scan(_carry_step, s0, (jnp.moveaxis(jnp.exp(g_end)[..., None], 1, 0),
                                               jnp.moveaxis(chunk_states, 1, 0)))
    if not with_output:
        return None, s_final
    s_in = jnp.moveaxis(s_in, 0, 1)
    qc = chunks(q)
    g_mid = gcum[:, :, HG_CHUNK // 2 - 1:HG_CHUNK // 2]
    q_rel = qc * jnp.exp(gcum - g_mid)
    k_rel = kc * jnp.exp(g_mid - gcum)
    causal = jnp.tril(jnp.ones((HG_CHUNK, HG_CHUNK), dtype=bool))
    att = jnp.where(causal, jnp.einsum("bclhk,bcshk->bchls", q_rel, k_rel), 0.0)
    o_intra = jnp.einsum("bchls,bcshv->bclhv", att, vc)
    o_inter = jnp.einsum("bclhk,bchkv->bclhv", qc * jnp.exp(gcum), s_in)
    return (o_intra + o_inter).reshape(bsz, seqlen, nh, dv), s_final


def hgrn2_gates(p_state, lb):
    b, n, _ = p_state.shape
    v = p_state[..., :HG_VAL].reshape(b, n, HG_HEADS, HG_DV)
    f_raw = p_state[..., HG_VAL:].astype(jnp.float32).reshape(b, n, 2, HG_HEADS, HG_DK)
    lbh = lb.reshape(HG_HEADS, HG_DK)
    f = lbh + (1.0 - lbh) * jax.nn.sigmoid(f_raw)
    return v, 1.0 - f, jnp.log(f)


def hgrn2_readout(o, g, norm_g):
    b, n = o.shape[0], o.shape[1]
    return rmsnorm(o, norm_g).reshape(b, n, HG_VAL) * jax.nn.silu(g.astype(jnp.float32))


def odd_mixer(hc, hl, w_in, lb, norm_g, w_out, rows, need_ctx):
    st0 = HG_KEY + HG_VAL
    bsz, n_lat = hl.shape[0], hl.shape[1]
    pl = to_col_major(hl, rows) @ w_in
    q_l = jax.nn.silu(pl[..., :HG_KEY]).reshape(bsz, n_lat, HG_HEADS, HG_DK)
    v_l, k_l, lf_l = hgrn2_gates(pl[..., st0:], lb)
    if need_ctx:
        pc = hc @ w_in
        q_c = jax.nn.silu(pc[..., :HG_KEY]).reshape(bsz, hc.shape[1], HG_HEADS, HG_DK)
        pc_state = pc[..., st0:]
    else:
        pc_state = hc @ w_in[:, st0:]
    v_c, k_c, lf_c = hgrn2_gates(pc_state, lb)
    os_c, os_l = [], []
    for d in range(2):
        flip = _flip if d else _same
        s0 = jnp.zeros((bsz, HG_HEADS, HG_DK, HG_DV), jnp.float32)
        qc_d = flip(q_c) if need_ctx else None
        o_c, s_ctx = hgrn2_scan(qc_d, flip(k_c[:, :, d]), flip(v_c), flip(lf_c[:, :, d]), s0, need_ctx)
        o_l, _ = hgrn2_scan(flip(q_l), flip(k_l[:, :, d]), flip(v_l), flip(lf_l[:, :, d]), s_ctx, True)
        os_l.append(flip(o_l))
        if need_ctx:
            os_c.append(flip(o_c))
    y_lat = from_col_major(hgrn2_readout(os_l[0] + os_l[1], pl[..., HG_KEY:st0], norm_g) @ w_out, rows).astype(hl.dtype)
    y_ctx = None
    if need_ctx:
        y_ctx = (hgrn2_readout(os_c[0] + os_c[1], pc[..., HG_KEY:st0], norm_g) @ w_out).astype(hc.dtype)
    return y_ctx, y_lat


def moe(h, router_w, router_b, w_gate, w_up, w_down):
    shp = h.shape
    t = h.reshape(-1, D_MODEL)
    scores = jax.nn.sigmoid((t @ router_w).astype(jnp.float32))
    sel = scores + router_b.astype(jnp.float32)
    grouped = sel.reshape(-1, N_EXPERT_GROUPS, EXPERTS_PER_GROUP)
    group_score = lax.top_k(grouped, TOP_K)[0].sum(-1)
    g_idx = jnp.argmax(group_score, axis=-1)
    idx = jnp.broadcast_to(g_idx[:, None, None], (t.shape[0], 1, EXPERTS_PER_GROUP))
    in_group = jnp.take_along_axis(grouped, idx, axis=1)[:, 0]
    _, local = lax.top_k(in_group, TOP_K)
    expert_idx = g_idx[:, None] * EXPERTS_PER_GROUP + local
    wts = jnp.take_along_axis(scores, expert_idx, axis=1)
    wts = wts / jnp.sum(wts, axis=-1, keepdims=True)
    combine = jnp.sum(jax.nn.one_hot(expert_idx, N_EXPERTS, dtype=jnp.float32) * wts[..., None], axis=1)

    def expert(acc, p):
        wg, wu, wd, gate_col = p
        y = (jax.nn.silu(t @ wg) * (t @ wu)) @ wd
        return acc + gate_col[:, None] * y, None

    out, _ = lax.scan(expert, jnp.zeros(t.shape, jnp.float32), (w_gate, w_up, w_down, combine.T))
    return out.reshape(shp).astype(h.dtype)


def setup_inputs(seed: int = 0) -> dict:
    key = jax.random.key(seed)
    ks = jax.random.split(key, 32)
    f32 = jnp.float32

    def dense(k, shape, fan_in, scale=1.0):
        return scale * fan_in ** -0.5 * jax.random.normal(k, shape, f32)

    def gain(k, shape):
        return 1.0 + 0.1 * jax.random.normal(k, shape, f32)

    def small(k, shape, s=0.02):
        return s * jax.random.normal(k, shape, f32)

    dt0 = jnp.exp(jax.random.uniform(ks[14], (N_EVEN, 2, SSD_HEADS), f32, np.log(1e-3), np.log(1e-1)))
    return {
        "x": jax.random.normal(ks[0], (BATCH, SEQ, D_MODEL), f32),
        "c": jax.random.normal(ks[1], (BATCH, D_MODEL), f32),
        "ctx": jax.random.normal(ks[2], (BATCH, CTX_LEN, D_MODEL), f32),
        "c_ctx": jax.random.normal(ks[3], (D_MODEL,), f32),
        "mod_w": dense(ks[4], (DEPTH, D_MODEL, N_MOD * D_MODEL), D_MODEL, 0.5),
        "mod_b": small(ks[5], (DEPTH, N_MOD * D_MODEL)),
        "norm_mix_g": gain(ks[6], (DEPTH, D_MODEL)),
        "norm_ffn_g": gain(ks[7], (DEPTH, D_MODEL)),
        "router_w": dense(ks[8], (D_MODEL, N_EXPERTS), D_MODEL),
        "router_b": small(ks[9], (N_EXPERTS,), 0.01),
        "moe_w_gate": dense(ks[10], (DEPTH, N_EXPERTS, D_MODEL, D_EXPERT), D_MODEL),
        "moe_w_up": dense(ks[11], (DEPTH, N_EXPERTS, D_MODEL, D_EXPERT), D_MODEL),
        "moe_w_down": dense(ks[12], (DEPTH, N_EXPERTS, D_EXPERT, D_MODEL), D_EXPERT),
        "ab_w_in": dense(ks[13], (N_EVEN, D_MODEL, AB_IN), D_MODEL),
        "ssd_conv_w": dense(ks[15], (N_EVEN, SSD_CONV, SSD_CONV_DIM), SSD_CONV),
        "ssd_conv_b": small(ks[16], (N_EVEN, SSD_CONV_DIM)),
        "ssd_dt_bias": dt0 + jnp.log(-jnp.expm1(-dt0)),
        "ssd_a_log": jnp.log(jax.random.uniform(ks[17], (N_EVEN, 2, SSD_HEADS), f32, 1.0, 16.0)),
        "ssd_d": gain(ks[18], (N_EVEN, SSD_HEADS)),
        "ssd_norm_g": gain(ks[19], (N_EVEN, SSD_INNER)),
        "cf_dw_w": dense(ks[20], (N_EVEN, CF_KERNEL, CF_CH), CF_KERNEL),
        "cf_dw_b": small(ks[21], (N_EVEN, CF_CH)),
        "cf_ln_g": gain(ks[22], (N_EVEN, CF_CH)),
        "cf_ln_b": small(ks[23], (N_EVEN, CF_CH)),
        "ab_w_out": dense(ks[24], (N_EVEN, AB_OUT, D_MODEL), AB_OUT),
        "hg_w_in": dense(ks[25], (N_ODD, D_MODEL, HG_IN), D_MODEL),
        "hg_lb": small(ks[26], (DEPTH, HG_KEY), 0.1),
        "hg_norm_g": gain(ks[27], (N_ODD, HG_DV)),
        "hg_w_out": dense(ks[28], (N_ODD, HG_VAL, D_MODEL), HG_VAL),
        "final_norm_g": gain(ks[29], (D_MODEL,)),
    }


def reference(x, c, ctx, c_ctx, mod_w, mod_b, norm_mix_g, norm_ffn_g, router_w, router_b,
              moe_w_gate, moe_w_up, moe_w_down, ab_w_in, ssd_conv_w, ssd_conv_b, ssd_dt_bias,
              ssd_a_log, ssd_d, ssd_norm_g, cf_dw_w, cf_dw_b, cf_ln_g, cf_ln_b, ab_w_out,
              hg_w_in, hg_lb, hg_norm_g, hg_w_out, final_norm_g):
    rows = x.shape[1] // GRID_W
    lb_all = jnp.cumsum(jax.nn.softmax(hg_lb.astype(jnp.float32), axis=0), axis=0)
    lb_all = lb_all - lb_all[0]
    xl, xc = x, ctx
    for l in range(DEPTH):
        last = l == DEPTH - 1
        j = l // 2
        m_l = modulation(c, mod_w[l], mod_b[l])[:, :, None, :]
        m_c = modulation(c_ctx, mod_w[l], mod_b[l])
        hl = rmsnorm(xl, norm_mix_g[l]) * (1.0 + m_l[:, 1]) + m_l[:, 0]
        hc = rmsnorm(xc, norm_mix_g[l]) * (1.0 + m_c[1]) + m_c[0]
        if l % 2 == 0:
            y_c, y_l = even_mixer(hc, hl, ab_w_in[j], ssd_conv_w[j], ssd_conv_b[j], ssd_dt_bias[j],
                                  ssd_a_log[j], ssd_d[j], ssd_norm_g[j], cf_dw_w[j], cf_dw_b[j],
                                  cf_ln_g[j], cf_ln_b[j], ab_w_out[j], rows, not last)
        else:
            y_c, y_l = odd_mixer(hc, hl, hg_w_in[j], lb_all[l], hg_norm_g[j], hg_w_out[j], rows, not last)
        xl = xl + m_l[:, 2] * y_l
        hl = rmsnorm(xl, norm_ffn_g[l]) * (1.0 + m_l[:, 4]) + m_l[:, 3]
        xl = xl + m_l[:, 5] * moe(hl, router_w, router_b, moe_w_gate[l], moe_w_up[l], moe_w_down[l])
        if not last:
            xc = xc + m_c[2] * y_c
            hc = rmsnorm(xc, norm_ffn_g[l]) * (1.0 + m_c[4]) + m_c[3]
            xc = xc + m_c[5] * moe(hc, router_w, router_b, moe_w_gate[l], moe_w_up[l], moe_w_down[l])
    return rmsnorm(xl, final_norm_g)
```

```python
import functools

import jax
import jax.numpy as jnp
from jax import lax
from jax.experimental import pallas as pl
from jax.experimental.pallas import tpu as pltpu

F32 = jnp.float32
BF16 = jnp.bfloat16

D_MODEL = 1024
GRID_W = 64
EPS = 1e-6
N_MOD = 6

SSD_HEADS = 16
SSD_HEAD_DIM = 64
SSD_INNER = SSD_HEADS * SSD_HEAD_DIM
SSD_GROUPS = 4
SSD_STATE = 128
SSD_CONV = 5
SSD_CHUNK = 128
SSD_BC = SSD_GROUPS * SSD_STATE
SSD_CONV_DIM = SSD_INNER + 2 * SSD_BC
SSD_GROUP_W = SSD_INNER // SSD_GROUPS

CF_CH = 1024
CF_KERNEL = 31
CF_PAD = 16

HG_HEADS = 8
HG_DK = 128
HG_DV = 128
HG_KEY = HG_HEADS * HG_DK
HG_VAL = HG_HEADS * HG_DV
HG_CHUNK = 64

N_EXPERTS = 16
N_EXPERT_GROUPS = 4
EXPERTS_PER_GROUP = 4
D_EXPERT = 512

LANES = 128
SUBLANES = 8
HALO = 16
VMEM_LIMIT = 48 * 1024 * 1024


def _cparams(sem):
    return pltpu.CompilerParams(dimension_semantics=sem, vmem_limit_bytes=VMEM_LIMIT)


def _silu(x):
    return x * jax.nn.sigmoid(x)


def _split3(v):
    hi = v.astype(BF16)
    r1 = v - hi.astype(F32)
    mid = r1.astype(BF16)
    lo = (r1 - mid.astype(F32)).astype(BF16)
    return hi, mid, lo


def _dot01(m01, v):
    hi, mid, lo = _split3(v)
    out = jnp.dot(m01, lo, preferred_element_type=F32)
    out = out + jnp.dot(m01, mid, preferred_element_type=F32)
    return out + jnp.dot(m01, hi, preferred_element_type=F32)


def _dot_v01(v, m01):
    hi, mid, lo = _split3(v)
    out = jnp.dot(lo, m01, preferred_element_type=F32)
    out = out + jnp.dot(mid, m01, preferred_element_type=F32)
    return out + jnp.dot(hi, m01, preferred_element_type=F32)


_NN = (((1,), (0,)), ((), ()))
_NT = (((1,), (1,)), ((), ()))
_TN = (((0,), (0,)), ((), ()))


def _dot_f32(a, b, dn):
    a1, a2, a3 = _split3(a)
    b1, b2, b3 = _split3(b)
    out = lax.dot_general(a3, b1, dn, preferred_element_type=F32)
    out = out + lax.dot_general(a1, b3, dn, preferred_element_type=F32)
    out = out + lax.dot_general(a2, b2, dn, preferred_element_type=F32)
    out = out + lax.dot_general(a2, b1, dn, preferred_element_type=F32)
    out = out + lax.dot_general(a1, b2, dn, preferred_element_type=F32)
    return out + lax.dot_general(a1, b1, dn, preferred_element_type=F32)


def _scan_tri(n, rev):
    r = lax.broadcasted_iota(jnp.int32, (n, n), 0)
    c = lax.broadcasted_iota(jnp.int32, (n, n), 1)
    return ((c >= r) if rev else (c <= r))


def _mod_kernel(c_ref, w_ref, b_ref, o_ref):
    cond = _silu(c_ref[...])
    o_ref[0] = _dot_f32(cond, w_ref[0], _NN) + b_ref[0]


def _modulation(cond, mod_w, mod_b):
    depth, d, n = mod_w.shape
    r = cond.shape[0]
    tn = 512
    return pl.pallas_call(
        _mod_kernel,
        grid=(depth, n // tn),
        in_specs=[pl.BlockSpec((r, d), lambda l, j: (0, 0)),
                  pl.BlockSpec((1, d, tn), lambda l, j: (l, 0, j)),
                  pl.BlockSpec((1, 1, tn), lambda l, j: (l, 0, j))],
        out_specs=pl.BlockSpec((1, r, tn), lambda l, j: (l, 0, j)),
        out_shape=jax.ShapeDtypeStruct((depth, r, n), F32),
        compiler_params=_cparams(("parallel", "parallel")),
        name="modulation",
    )(cond, mod_w, mod_b.reshape(depth, 1, n))


def _normmod(x, g, shift, scale):
    ms = jnp.mean(x * x, axis=-1, keepdims=True)
    return (x * lax.rsqrt(ms + EPS) * g) * (1.0 + scale) + shift


def _normmod_kernel(x_ref, g_ref, sh_ref, sc_ref, o_ref):
    o_ref[0] = _normmod(x_ref[0], g_ref[...], sh_ref[0], sc_ref[0]).astype(o_ref.dtype)


def _mod_spec(nb):
    return pl.BlockSpec((1, 1, D_MODEL), (lambda b, *_: (b, 0, 0)) if nb > 1 else (lambda b, *_: (0, 0, 0)))


def _norm_mod(x, g, shift, scale):
    bsz, seq, d = x.shape
    tl = min(seq, 512)
    return pl.pallas_call(
        _normmod_kernel,
        grid=(bsz, seq // tl),
        in_specs=[pl.BlockSpec((1, tl, d), lambda b, i: (b, i, 0)),
                  pl.BlockSpec((1, d), lambda b, i: (0, 0)),
                  _mod_spec(shift.shape[0]), _mod_spec(scale.shape[0])],
        out_specs=pl.BlockSpec((1, tl, d), lambda b, i: (b, i, 0)),
        out_shape=jax.ShapeDtypeStruct((bsz, seq, d), BF16),
        compiler_params=_cparams(("parallel", "parallel")),
        name="norm_mod",
    )(x, g.reshape(1, d), shift, scale)


def _normmod_cm_kernel(x_ref, g_ref, sh_ref, sc_ref, o_ref, *, rows):
    for c in range(SUBLANES):
        piece = x_ref[0, :, c, :]
        o_ref[0, c * rows:(c + 1) * rows, :] = _normmod(piece, g_ref[...], sh_ref[0], sc_ref[0]).astype(o_ref.dtype)


def _norm_mod_colmajor(x, g, shift, scale):
    bsz, seq, d = x.shape
    rows = seq // GRID_W
    x4 = x.reshape(bsz, rows, GRID_W, d)
    return pl.pallas_call(
        functools.partial(_normmod_cm_kernel, rows=rows),
        grid=(bsz, GRID_W // SUBLANES),
        in_specs=[pl.BlockSpec((1, rows, SUBLANES, d), lambda b, i: (b, 0, i, 0)),
                  pl.BlockSpec((1, d), lambda b, i: (0, 0)),
                  _mod_spec(shift.shape[0]), _mod_spec(scale.shape[0])],
        out_specs=pl.BlockSpec((1, SUBLANES * rows, d), lambda b, i: (b, i, 0)),
        out_shape=jax.ShapeDtypeStruct((bsz, seq, d), BF16),
        compiler_params=_cparams(("parallel", "parallel")),
        name="norm_mod_colmajor",
    )(x4, g.reshape(1, d), shift, scale)


def _mm_kernel(a_ref, b_ref, o_ref):
    o_ref[0] = jnp.dot(a_ref[0], b_ref[...], preferred_element_type=F32).astype(o_ref.dtype)


def _mm_res_kernel(a_ref, b_ref, res_ref, gate_ref, o_ref):
    y = jnp.dot(a_ref[0], b_ref[...], preferred_element_type=F32)
    o_ref[0] = res_ref[0] + gate_ref[0] * y


def _mm_res_cm_kernel(a_ref, b_ref, res_ref, gate_ref, o_ref, *, rows):
    y = jnp.dot(a_ref[0], b_ref[...], preferred_element_type=F32)
    for c in range(SUBLANES):
        o_ref[0, :, c, :] = res_ref[0, :, c, :] + gate_ref[0] * y[c * rows:(c + 1) * rows, :]


def _matmul(a, w, out_dtype):
    bsz, seq, k = a.shape
    n = w.shape[1]
    tm, tn = min(seq, 1024), min(n, 512)
    return pl.pallas_call(
        _mm_kernel,
        grid=(bsz, seq // tm, n // tn),
        in_specs=[pl.BlockSpec((1, tm, k), lambda b, i, j: (b, i, 0)),
                  pl.BlockSpec((k, tn), lambda b, i, j: (0, j))],
        out_specs=pl.BlockSpec((1, tm, tn), lambda b, i, j: (b, i, j)),
        out_shape=jax.ShapeDtypeStruct((bsz, seq, n), out_dtype),
        compiler_params=_cparams(("parallel", "parallel", "parallel")),
        name="matmul",
    )(a, w)


def _matmul_res(a, w, res, gate):
    bsz, seq, k = a.shape
    n = w.shape[1]
    tm, tn = min(seq, 1024), min(n, 512)
    nb = gate.shape[0]
    return pl.pallas_call(
        _mm_res_kernel,
        grid=(bsz, seq // tm, n // tn),
        in_specs=[pl.BlockSpec((1, tm, k), lambda b, i, j: (b, i, 0)),
                  pl.BlockSpec((k, tn), lambda b, i, j: (0, j)),
                  pl.BlockSpec((1, tm, tn), lambda b, i, j: (b, i, j)),
                  pl.BlockSpec((1, 1, tn), (lambda b, i, j: (b, 0, j)) if nb > 1 else (lambda b, i, j: (0, 0, j)))],
        out_specs=pl.BlockSpec((1, tm, tn), lambda b, i, j: (b, i, j)),
        out_shape=jax.ShapeDtypeStruct((bsz, seq, n), F32),
        compiler_params=_cparams(("parallel", "parallel", "parallel")),
        name="matmul_res",
    )(a, w, res, gate)


def _matmul_res_colmajor(a, w, res, gate):
    bsz, seq, k = a.shape
    n = w.shape[1]
    rows = seq // GRID_W
    tm, tn = SUBLANES * rows, min(n, 512)
    res4 = res.reshape(bsz, rows, GRID_W, n)
    out = pl.pallas_call(
        functools.partial(_mm_res_cm_kernel, rows=rows),
        grid=(bsz, seq // tm, n // tn),
        in_specs=[pl.BlockSpec((1, tm, k), lambda b, i, j: (b, i, 0)),
                  pl.BlockSpec((k, tn), lambda b, i, j: (0, j)),
                  pl.BlockSpec((1, rows, SUBLANES, tn), lambda b, i, j: (b, 0, i, j)),
                  pl.BlockSpec((1, 1, tn), lambda b, i, j: (b, 0, j))],
        out_specs=pl.BlockSpec((1, rows, SUBLANES, tn), lambda b, i, j: (b, 0, i, j)),
        out_shape=jax.ShapeDtypeStruct((bsz, rows, GRID_W, n), F32),
        compiler_params=_cparams(("parallel", "parallel", "parallel")),
        name="matmul_res_colmajor",
    )(a, w, res4, gate)
    return out.reshape(bsz, seq, n)


def _ssd_kernel(*refs, rev, nchunks, readout):
    if readout:
        (xbc_ref, prev_ref, next_ref, dt_ref, cw_ref, cb_ref, dtb_ref, alog_ref, dsk_ref, s0_ref,
         yin_ref, z_ref, ng_ref, y_ref, sfin_ref, pad_ref, st_ref) = refs
    else:
        (xbc_ref, prev_ref, next_ref, dt_ref, cw_ref, cb_ref, dtb_ref, alog_ref, dsk_ref, s0_ref,
         y_ref, sfin_ref, pad_ref, st_ref) = refs
    step = pl.program_id(1)
    u = (nchunks - 1 - step) if rev else step
    ck = SSD_CHUNK

    @pl.when(step == 0)
    def _():
        st_ref[...] = s0_ref[0]

    pad_ref[HALO:HALO + ck, :] = xbc_ref[0].astype(F32)
    pad_ref[0:HALO, :] = jnp.where(u > 0, prev_ref[0].astype(F32), 0.0)
    pad_ref[HALO + ck:2 * HALO + ck, :] = jnp.where(u < nchunks - 1, next_ref[0].astype(F32), 0.0)
    half = SSD_CONV // 2
    acc = cb_ref[...] + cw_ref[0:1, :] * pad_ref[pl.ds(HALO - half, ck), :]
    for k in range(1, SSD_CONV):
        acc = acc + cw_ref[k:k + 1, :] * pad_ref[pl.ds(HALO - half + k, ck), :]
    xbc = _silu(acc)
    xs = xbc[:, :SSD_INNER]

    dt = jax.nn.softplus(dt_ref[0] + dtb_ref[...])
    la = dt * (-jnp.exp(alog_ref[...]))
    tri = _scan_tri(ck, rev).astype(BF16)
    acum = _dot01(tri, la)
    last = 0 if rev else ck - 1
    total = acum[last:last + 1, :]
    acum_t = acum.T

    hrow = lax.broadcasted_iota(jnp.int32, (LANES, SSD_INNER), 0)
    hcol = lax.broadcasted_iota(jnp.int32, (LANES, SSD_INNER), 1)
    expand = (hcol // SSD_HEAD_DIM == hrow).astype(BF16)
    dt_x = _dot_v01(dt, expand)
    ea_x = _dot_v01(jnp.exp(acum), expand)
    te_x = _dot_v01(jnp.exp(total - acum), expand)
    cd_x = _dot_v01(jnp.broadcast_to(jnp.exp(total), (SUBLANES, LANES)), expand)[0:1, :]

    xdt = xs * dt_x
    xdt_b = xdt.astype(BF16)
    xw_b = (xdt * te_x).astype(BF16)

    r = lax.broadcasted_iota(jnp.int32, (ck, ck), 0)
    c = lax.broadcasted_iota(jnp.int32, (ck, ck), 1)
    causal = (c >= r) if rev else (c <= r)
    lane = lax.broadcasted_iota(jnp.int32, (ck, LANES), 1)
    nt = (((1,), (1,)), ((), ()))
    tn = (((0,), (0,)), ((), ()))

    y_parts = []
    for g in range(SSD_GROUPS):
        b_g = xbc[:, SSD_INNER + g * SSD_STATE:SSD_INNER + (g + 1) * SSD_STATE].astype(BF16)
        c_g = xbc[:, SSD_INNER + SSD_BC + g * SSD_STATE:SSD_INNER + SSD_BC + (g + 1) * SSD_STATE].astype(BF16)
        cb = lax.dot_general(c_g, b_g, nt, preferred_element_type=F32)
        lo = g * SSD_GROUP_W
        s_in = st_ref[g]
        y_off = jnp.dot(c_g, s_in.astype(BF16), preferred_element_type=F32) * ea_x[:, lo:lo + SSD_GROUP_W]
        chunk_state = lax.dot_general(b_g, xw_b[:, lo:lo + SSD_GROUP_W], tn, preferred_element_type=F32)
        st_ref[g] = cd_x[:, lo:lo + SSD_GROUP_W] * s_in + chunk_state
        heads_per_group = SSD_HEADS // SSD_GROUPS
        for pair in range(heads_per_group // 2):
            plo = lo + pair * LANES
            xpair = xdt_b[:, plo:plo + LANES]
            y_pair = y_off[:, pair * LANES:(pair + 1) * LANES]
            for half_i in range(2):
                h = g * heads_per_group + pair * 2 + half_i
                seg = acum[:, h:h + 1] - acum_t[h:h + 1, :]
                m = (cb * jnp.where(causal, jnp.exp(seg), 0.0)).astype(BF16)
                in_half = (lane >= half_i * SSD_HEAD_DIM) & (lane < (half_i + 1) * SSD_HEAD_DIM)
                y_pair = y_pair + jnp.dot(m, jnp.where(in_half, xpair, jnp.zeros_like(xpair)),
                                          preferred_element_type=F32)
            y_parts.append(y_pair)
    y = jnp.concatenate(y_parts, axis=1)

    if readout:
        y = y + yin_ref[0].astype(F32)
        y = y * _silu(z_ref[0].astype(F32))
        ms = jnp.mean(y * y, axis=-1, keepdims=True)
        y_ref[0] = (y * lax.rsqrt(ms + EPS) * ng_ref[...]).astype(y_ref.dtype)
    else:
        y_ref[0] = (y + dsk_ref[...] * xs).astype(y_ref.dtype)

    @pl.when(step == nchunks - 1)
    def _():
        sfin_ref[0] = st_ref[...]


def _ssd_scan(xbc, dt_raw, conv_w, conv_b, dt_bias, a_log, d_skip, s0, rev, y_in=None, z=None, norm_g=None):
    bsz, seq, _ = xbc.shape
    nchunks = seq // SSD_CHUNK
    hb = SSD_CHUNK // HALO
    nhalo = seq // HALO
    readout = y_in is not None

    def cidx(i):
        return (nchunks - 1 - i) if rev else i

    def pad_lanes(v):
        return jnp.pad(v.reshape(1, -1), ((0, 0), (0, LANES - v.shape[-1])))

    in_specs = [
        pl.BlockSpec((1, SSD_CHUNK, SSD_CONV_DIM), lambda b, i: (b, cidx(i), 0)),
        pl.BlockSpec((1, HALO, SSD_CONV_DIM), lambda b, i: (b, jnp.maximum(cidx(i) * hb - 1, 0), 0)),
        pl.BlockSpec((1, HALO, SSD_CONV_DIM), lambda b, i: (b, jnp.minimum((cidx(i) + 1) * hb, nhalo - 1), 0)),
        pl.BlockSpec((1, SSD_CHUNK, LANES), lambda b, i: (b, cidx(i), 0)),
        pl.BlockSpec((SUBLANES, SSD_CONV_DIM), lambda b, i: (0, 0)),
        pl.BlockSpec((1, SSD_CONV_DIM), lambda b, i: (0, 0)),
        pl.BlockSpec((1, LANES), lambda b, i: (0, 0)),
        pl.BlockSpec((1, LANES), lambda b, i: (0, 0)),
        pl.BlockSpec((1, SSD_INNER), lambda b, i: (0, 0)),
        pl.BlockSpec((1, SSD_GROUPS, SSD_STATE, SSD_GROUP_W), lambda b, i: (b, 0, 0, 0)),
    ]
    args = [xbc, xbc, xbc, dt_raw,
            jnp.pad(conv_w, ((0, SUBLANES - SSD_CONV), (0, 0))), conv_b.reshape(1, -1),
            pad_lanes(dt_bias), pad_lanes(a_log), jnp.repeat(d_skip, SSD_HEAD_DIM).reshape(1, -1), s0]
    if readout:
        in_specs += [pl.BlockSpec((1, SSD_CHUNK, SSD_INNER), lambda b, i: (b, cidx(i), 0)),
                     pl.BlockSpec((1, SSD_CHUNK, SSD_INNER), lambda b, i: (b, cidx(i), 0)),
                     pl.BlockSpec((1, SSD_INNER), lambda b, i: (0, 0))]
        args += [y_in, z, norm_g.reshape(1, -1)]
    return pl.pallas_call(
        functools.partial(_ssd_kernel, rev=rev, nchunks=nchunks, readout=readout),
        grid=(bsz, nchunks),
        in_specs=in_specs,
        out_specs=[pl.BlockSpec((1, SSD_CHUNK, SSD_INNER), lambda b, i: (b, cidx(i), 0)),
                   pl.BlockSpec((1, SSD_GROUPS, SSD_STATE, SSD_GROUP_W), lambda b, i: (b, 0, 0, 0))],
        out_shape=[jax.ShapeDtypeStruct((bsz, seq, SSD_INNER), BF16),
                   jax.ShapeDtypeStruct((bsz, SSD_GROUPS, SSD_STATE, SSD_GROUP_W), F32)],
        scratch_shapes=[pltpu.VMEM((SSD_CHUNK + 2 * HALO, SSD_CONV_DIM), F32),
                        pltpu.VMEM((SSD_GROUPS, SSD_STATE, SSD_GROUP_W), F32)],
        compiler_params=_cparams(("parallel", "arbitrary")),
        name="ssd_scan_rev" if rev else "ssd_scan_fwd",
    )(*args)


def _conformer_kernel(p_ref, w_ref, b_ref, g_ref, beta_ref, o_ref, pad_ref, conv_ref, *, nseq, slen):
    v = p_ref[0, :, :CF_CH].astype(F32)
    gate = p_ref[0, :, CF_CH:].astype(F32)
    u = v * jax.nn.sigmoid(gate)
    half = CF_KERNEL // 2
    for s in range(nseq):
        pad_ref[s, 0:CF_PAD, :] = jnp.zeros((CF_PAD, CF_CH), F32)
        pad_ref[s, CF_PAD + slen:2 * CF_PAD + slen, :] = jnp.zeros((CF_PAD, CF_CH), F32)
        pad_ref[s, CF_PAD:CF_PAD + slen, :] = u[s * slen:(s + 1) * slen, :]
    for s in range(nseq):
        for cb in range(CF_CH // LANES):
            ch = slice(cb * LANES, (cb + 1) * LANES)
            acc = b_ref[:, ch] + w_ref[0:1, ch] * pad_ref[s, pl.ds(CF_PAD - half, slen), ch]
            for k in range(1, CF_KERNEL):
                acc = acc + w_ref[k:k + 1, ch] * pad_ref[s, pl.ds(CF_PAD - half + k, slen), ch]
            conv_ref[s * slen:(s + 1) * slen, ch] = acc
    y = conv_ref[...]
    mu = jnp.mean(y, axis=-1, keepdims=True)
    xc = y - mu
    var = jnp.mean(xc * xc, axis=-1, keepdims=True)
    y = xc * lax.rsqrt(var + EPS) * g_ref[...] + beta_ref[...]
    o_ref[0] = _silu(y).astype(o_ref.dtype)


def _conformer(p_cf, dw_w, dw_b, ln_g, ln_b, slen):
    bsz, seq, _ = p_cf.shape
    nseq = max(1, min(seq, 256) // slen)
    tb = nseq * slen
    kpad = -(-CF_KERNEL // SUBLANES) * SUBLANES
    return pl.pallas_call(
        functools.partial(_conformer_kernel, nseq=nseq, slen=slen),
        grid=(bsz, seq // tb),
        in_specs=[pl.BlockSpec((1, tb, 2 * CF_CH), lambda b, i: (b, i, 0)),
                  pl.BlockSpec((kpad, CF_CH), lambda b, i: (0, 0)),
                  pl.BlockSpec((1, CF_CH), lambda b, i: (0, 0)),
                  pl.BlockSpec((1, CF_CH), lambda b, i: (0, 0)),
                  pl.BlockSpec((1, CF_CH), lambda b, i: (0, 0))],
        out_specs=pl.BlockSpec((1, tb, CF_CH), lambda b, i: (b, i, 0)),
        out_shape=jax.ShapeDtypeStruct((bsz, seq, CF_CH), BF16),
        scratch_shapes=[pltpu.VMEM((nseq, slen + 2 * CF_PAD, CF_CH), F32), pltpu.VMEM((tb, CF_CH), F32)],
        compiler_params=_cparams(("parallel", "parallel")),
        name="conformer_conv",
    )(p_cf, jnp.pad(dw_w, ((0, kpad - CF_KERNEL), (0, 0))), dw_b.reshape(1, -1), ln_g.reshape(1, -1),
      ln_b.reshape(1, -1))


def _hgrn2_kernel(*refs, rev, nchunks, mode):
    if mode == "state":
        v_ref, f_ref, lb_ref, s0_ref, sfin_ref, st_ref = refs
    elif mode == "out":
        q_ref, v_ref, f_ref, lb_ref, s0_ref, o_ref, st_ref = refs
    else:
        q_ref, v_ref, f_ref, lb_ref, s0_ref, oin_ref, g_ref, ng_ref, o_ref, st_ref = refs
    step = pl.program_id(1)
    ck = HG_CHUNK

    @pl.when(step == 0)
    def _():
        st_ref[...] = s0_ref[0]

    lb = lb_ref[...]
    f = lb + (1.0 - lb) * jax.nn.sigmoid(f_ref[0])
    kk = 1.0 - f
    gcum = _dot01(_scan_tri(ck, rev).astype(BF16), jnp.log(f))
    last = 0 if rev else ck - 1
    g_end = gcum[last:last + 1, :]
    k_end = (kk * jnp.exp(g_end - gcum)).astype(BF16)
    dec_end = jnp.exp(g_end)
    v = v_ref[0]
    nt = (((1,), (1,)), ((), ()))
    tn = (((0,), (0,)), ((), ()))

    if mode != "state":
        mid_pos = HG_CHUNK // 2 - 1
        mid = (ck - 1 - mid_pos) if rev else mid_pos
        g_mid = gcum[mid:mid + 1, :]
        q = _silu(q_ref[0].astype(F32))
        q_rel = (q * jnp.exp(gcum - g_mid)).astype(BF16)
        k_rel = (kk * jnp.exp(g_mid - gcum)).astype(BF16)
        q_dec = (q * jnp.exp(gcum)).astype(BF16)
        r = lax.broadcasted_iota(jnp.int32, (ck, ck), 0)
        c = lax.broadcasted_iota(jnp.int32, (ck, ck), 1)
        causal = (c >= r) if rev else (c <= r)

    outs = []
    for h in range(HG_HEADS):
        ks = slice(h * HG_DK, (h + 1) * HG_DK)
        vs = slice(h * HG_DV, (h + 1) * HG_DV)
        s_in = st_ref[h]
        if mode != "state":
            att = lax.dot_general(q_rel[:, ks], k_rel[:, ks], nt, preferred_element_type=F32)
            att = jnp.where(causal, att, 0.0).astype(BF16)
            o_h = jnp.dot(att, v[:, vs], preferred_element_type=F32)
            o_h = o_h + lax.dot_general(q_dec[:, ks], s_in.astype(BF16), nt, preferred_element_type=F32)
            outs.append(o_h)
        chunk_state = lax.dot_general(v[:, vs], k_end[:, ks], tn, preferred_element_type=F32)
        st_ref[h] = s_in * dec_end[:, ks] + chunk_state

    if mode == "out":
        o_ref[0] = jnp.concatenate(outs, axis=1).astype(o_ref.dtype)
    elif mode == "readout":
        gate = _silu(g_ref[0].astype(F32))
        for h in range(HG_HEADS):
            vs = slice(h * HG_DV, (h + 1) * HG_DV)
            o_h = outs[h] + oin_ref[0, :, vs].astype(F32)
            ms = jnp.mean(o_h * o_h, axis=-1, keepdims=True)
            o_ref[0, :, vs] = ((o_h * lax.rsqrt(ms + EPS) * ng_ref[...]) * gate[:, vs]).astype(o_ref.dtype)
    else:
        @pl.when(step == nchunks - 1)
        def _():
            sfin_ref[0] = st_ref[...]


def _hgrn2_scan(v, f_raw, lb, s0, rev, mode, q=None, o_in=None, g=None, norm_g=None):
    bsz, seq, _ = v.shape
    nchunks = seq // HG_CHUNK
    dcol = 1 if rev else 0

    def cidx(i):
        return (nchunks - 1 - i) if rev else i

    tok = lambda width: pl.BlockSpec((1, HG_CHUNK, width), lambda b, i: (b, cidx(i), 0))
    state_spec = pl.BlockSpec((1, HG_HEADS, HG_DV, HG_DK), lambda b, i: (b, 0, 0, 0))
    f_spec = pl.BlockSpec((1, HG_CHUNK, HG_KEY), lambda b, i: (b, cidx(i), dcol))
    lb_spec = pl.BlockSpec((1, HG_KEY), lambda b, i: (0, 0))
    state_shape = jax.ShapeDtypeStruct((bsz, HG_HEADS, HG_DV, HG_DK), F32)
    if mode == "state":
        in_specs, args = [tok(HG_VAL), f_spec, lb_spec, state_spec], [v, f_raw, lb.reshape(1, -1), s0]
        out_specs, out_shape = state_spec, state_shape
    else:
        in_specs = [tok(HG_KEY), tok(HG_VAL), f_spec, lb_spec, state_spec]
        args = [q, v, f_raw, lb.reshape(1, -1), s0]
        if mode == "readout":
            in_specs += [tok(HG_VAL), tok(HG_VAL), pl.BlockSpec((1, HG_DV), lambda b, i: (0, 0))]
            args += [o_in, g, norm_g.reshape(1, -1)]
        out_specs, out_shape = tok(HG_VAL), jax.ShapeDtypeStruct((bsz, seq, HG_VAL), BF16)
    return pl.pallas_call(
        functools.partial(_hgrn2_kernel, rev=rev, nchunks=nchunks, mode=mode),
        grid=(bsz, nchunks),
        in_specs=in_specs, out_specs=out_specs, out_shape=out_shape,
        scratch_shapes=[pltpu.VMEM((HG_HEADS, HG_DV, HG_DK), F32)],
        compiler_params=_cparams(("parallel", "arbitrary")),
        name=f"hgrn2_{mode}_{'rev' if rev else 'fwd'}",
    )(*args)


def _first_argmax(vals):
    best, idx = vals[0], jnp.zeros(vals[0].shape, jnp.int32)
    for j in range(1, len(vals)):
        better = vals[j] > best
        idx = jnp.where(better, j, idx)
        best = jnp.where(better, vals[j], best)
    return idx, best


def _pick(idx, vals):
    out = vals[-1]
    for j in range(len(vals) - 2, -1, -1):
        out = jnp.where(idx == j, vals[j], out)
    return out


def _router_kernel(x_ref, g_ref, sh_ref, sc_ref, rw_ref, rb_ref, h_ref, comb_ref):
    h = _normmod(x_ref[0], g_ref[...], sh_ref[0], sc_ref[0])
    h_ref[0] = h.astype(h_ref.dtype)
    scores = jax.nn.sigmoid(_dot_f32(rw_ref[...], h, _NT))
    sel = scores + rb_ref[...]
    srow = [sel[e:e + 1, :] for e in range(N_EXPERTS)]
    prow = [scores[e:e + 1, :] for e in range(N_EXPERTS)]
    gscore = []
    for gi in range(N_EXPERT_GROUPS):
        m = srow[gi * EXPERTS_PER_GROUP:(gi + 1) * EXPERTS_PER_GROUP]
        pair_sums = [m[i] + m[j] for i in range(EXPERTS_PER_GROUP) for j in range(i + 1, EXPERTS_PER_GROUP)]
        best = pair_sums[0]
        for p in pair_sums[1:]:
            best = jnp.maximum(best, p)
        gscore.append(best)
    gidx, _ = _first_argmax(gscore)
    in_sel = [_pick(gidx, [srow[gi * EXPERTS_PER_GROUP + j] for gi in range(N_EXPERT_GROUPS)])
              for j in range(EXPERTS_PER_GROUP)]
    in_p = [_pick(gidx, [prow[gi * EXPERTS_PER_GROUP + j] for gi in range(N_EXPERT_GROUPS)])
            for j in range(EXPERTS_PER_GROUP)]
    i1, _ = _first_argmax(in_sel)
    i2, _ = _first_argmax([jnp.where(i1 == j, -jnp.inf, in_sel[j]) for j in range(EXPERTS_PER_GROUP)])
    w1, w2 = _pick(i1, in_p), _pick(i2, in_p)
    den = w1 + w2
    e1, e2 = gidx * EXPERTS_PER_GROUP + i1, gidx * EXPERTS_PER_GROUP + i2
    tl = h.shape[0]
    erow = lax.broadcasted_iota(jnp.int32, (LANES, tl), 0)
    comb_t = jnp.where(erow == e1, w1 / den, 0.0) + jnp.where(erow == e2, w2 / den, 0.0)
    comb_ref[0] = comb_t.T


def _ffn_norm_router(x, g, shift, scale, router_w, router_b):
    bsz, seq, d = x.shape
    tl = min(seq, 512)
    return pl.pallas_call(
        _router_kernel,
        grid=(bsz, seq // tl),
        in_specs=[pl.BlockSpec((1, tl, d), lambda b, i: (b, i, 0)),
                  pl.BlockSpec((1, d), lambda b, i: (0, 0)),
                  _mod_spec(shift.shape[0]), _mod_spec(scale.shape[0]),
                  pl.BlockSpec((N_EXPERTS, d), lambda b, i: (0, 0)),
                  pl.BlockSpec((N_EXPERTS, 1), lambda b, i: (0, 0))],
        out_specs=[pl.BlockSpec((1, tl, d), lambda b, i: (b, i, 0)),
                   pl.BlockSpec((1, tl, LANES), lambda b, i: (b, i, 0))],
        out_shape=[jax.ShapeDtypeStruct((bsz, seq, d), BF16),
                   jax.ShapeDtypeStruct((bsz, seq, LANES), F32)],
        compiler_params=_cparams(("parallel", "parallel")),
        name="ffn_norm_router",
    )(x, g.reshape(1, d), shift, scale, router_w.T, router_b.reshape(N_EXPERTS, 1))


def _moe_dense_kernel(h_ref, wg_ref, wu_ref, wd_ref, comb_ref, res_ref, gate_ref, o_ref, acc_ref):
    e = pl.program_id(2)

    @pl.when(e == 0)
    def _():
        acc_ref[...] = jnp.zeros_like(acc_ref)

    h = h_ref[0]
    a = jnp.dot(h, wg_ref[0], preferred_element_type=F32)
    b = jnp.dot(h, wu_ref[0], preferred_element_type=F32)
    y = jnp.dot((_silu(a) * b).astype(BF16), wd_ref[0], preferred_element_type=F32)
    lane = lax.broadcasted_iota(jnp.int32, comb_ref.shape[1:], 1)
    col = jnp.sum(jnp.where(lane == e, comb_ref[0], 0.0), axis=1, keepdims=True)
    acc_ref[...] += col * y

    @pl.when(e == N_EXPERTS - 1)
    def _():
        o_ref[0] = res_ref[0] + gate_ref[0] * acc_ref[...]


def _moe_dense(h, comb, w_gate, w_up, w_down, res, gate):
    bsz, seq, d = h.shape
    tm = min(seq, 512)
    nb = gate.shape[0]
    return pl.pallas_call(
        _moe_dense_kernel,
        grid=(bsz, seq // tm, N_EXPERTS),
        in_specs=[pl.BlockSpec((1, tm, d), lambda b, i, e: (b, i, 0)),
                  pl.BlockSpec((1, d, D_EXPERT), lambda b, i, e: (e, 0, 0)),
                  pl.BlockSpec((1, d, D_EXPERT), lambda b, i, e: (e, 0, 0)),
                  pl.BlockSpec((1, D_EXPERT, d), lambda b, i, e: (e, 0, 0)),
                  pl.BlockSpec((1, tm, LANES), lambda b, i, e: (b, i, 0)),
                  pl.BlockSpec((1, tm, d), lambda b, i, e: (b, i, 0)),
                  pl.BlockSpec((1, 1, d), (lambda b, i, e: (b, 0, 0)) if nb > 1 else (lambda b, i, e: (0, 0, 0)))],
        out_specs=pl.BlockSpec((1, tm, d), lambda b, i, e: (b, i, 0)),
        out_shape=jax.ShapeDtypeStruct((bsz, seq, d), F32),
        scratch_shapes=[pltpu.VMEM((tm, d), F32)],
        compiler_params=_cparams(("parallel", "parallel", "arbitrary")),
        name="moe_dense",
    )(h, w_gate, w_up, w_down, comb, res, gate)


def _final_norm_kernel(x_ref, g_ref, o_ref):
    x = x_ref[0]
    ms = jnp.mean(x * x, axis=-1, keepdims=True)
    o_ref[0] = x * lax.rsqrt(ms + EPS) * g_ref[...]


def _final_norm(x, g):
    bsz, seq, d = x.shape
    tl = min(seq, 512)
    return pl.pallas_call(
        _final_norm_kernel,
        grid=(bsz, seq // tl),
        in_specs=[pl.BlockSpec((1, tl, d), lambda b, i: (b, i, 0)), pl.BlockSpec((1, d), lambda b, i: (0, 0))],
        out_specs=pl.BlockSpec((1, tl, d), lambda b, i: (b, i, 0)),
        out_shape=jax.ShapeDtypeStruct((bsz, seq, d), F32),
        compiler_params=_cparams(("parallel", "parallel")),
        name="final_norm",
    )(x, g.reshape(1, d))


def _even_layer_mixer(xs, mods, norm_g, w_in, conv_w, conv_b, dt_bias, a_log, d_skip, ssd_g,
                      cf_w, cf_b, cf_lng, cf_lnb, w_out, rows):
    s_lo = SSD_INNER
    dt_lo = SSD_INNER + SSD_CONV_DIM
    cf_lo = dt_lo + 2 * SSD_HEADS
    wb = w_in.astype(BF16)
    w_z, w_xbc, w_cf = wb[:, :s_lo], wb[:, s_lo:dt_lo], wb[:, cf_lo:]
    w_dt = [jnp.pad(wb[:, dt_lo + d * SSD_HEADS:dt_lo + (d + 1) * SSD_HEADS], ((0, 0), (0, LANES - SSD_HEADS)))
            for d in range(2)]
    w_out_b = w_out.astype(BF16)
    bsz = xs[0].shape[0]
    zero_state = jnp.zeros((bsz, SSD_GROUPS, SSD_STATE, SSD_GROUP_W), F32)
    proj = []
    for x, m in zip(xs, mods):
        h = _norm_mod(x, norm_g, m[0], m[1])
        proj.append(dict(z=_matmul(h, w_z, BF16), xbc=_matmul(h, w_xbc, BF16), cf=_matmul(h, w_cf, BF16),
                         dt=[_matmul(h, w_dt[d], F32) for d in range(2)]))
    o_ssd = [None, None]
    y_fwd = [None, None]
    state = zero_state
    for si in range(2):
        p = proj[si]
        y_fwd[si], state = _ssd_scan(p["xbc"], p["dt"][0], conv_w, conv_b, dt_bias[0], a_log[0], d_skip, state, False)
    state = zero_state
    for si in range(2):
        p = proj[si]
        o_ssd[si], state = _ssd_scan(p["xbc"], p["dt"][1], conv_w, conv_b, dt_bias[1], a_log[1], d_skip, state, True,
                                     y_in=y_fwd[si], z=p["z"], norm_g=ssd_g)
    outs = []
    for si, (x, m) in enumerate(zip(xs, mods)):
        slen = x.shape[1] if si == 0 else GRID_W
        o_cf = _conformer(proj[si]["cf"], cf_w, cf_b, cf_lng, cf_lnb, slen)
        cat = jnp.concatenate([o_ssd[si], o_cf], axis=-1)
        outs.append(_matmul_res(cat, w_out_b, x, m[2]))
    return outs


def _odd_layer_mixer(xc, xl, m_c, m_l, norm_g, w_in, lb, hg_g, w_out):
    st0 = HG_KEY + HG_VAL
    wb = w_in.astype(BF16)
    w_q, w_g, w_i, w_f = wb[:, :HG_KEY], wb[:, HG_KEY:st0], wb[:, st0:st0 + HG_VAL], wb[:, st0 + HG_VAL:]
    bsz = xl.shape[0]
    h_c = _norm_mod(xc, norm_g, m_c[0], m_c[1])
    h_l = _norm_mod_colmajor(xl, norm_g, m_l[0], m_l[1])
    v_c, f_c = _matmul(h_c, w_i, BF16), _matmul(h_c, w_f, F32)
    q_l, g_l = _matmul(h_l, w_q, BF16), _matmul(h_l, w_g, BF16)
    v_l, f_l = _matmul(h_l, w_i, BF16), _matmul(h_l, w_f, F32)
    zero_state = jnp.zeros((bsz, HG_HEADS, HG_DV, HG_DK), F32)
    s_f = _hgrn2_scan(v_c, f_c, lb, zero_state, False, "state")
    s_r = _hgrn2_scan(v_c, f_c, lb, zero_state, True, "state")
    o_f = _hgrn2_scan(v_l, f_l, lb, s_f, False, "out", q=q_l)
    o = _hgrn2_scan(v_l, f_l, lb, s_r, True, "readout", q=q_l, o_in=o_f, g=g_l, norm_g=hg_g)
    return _matmul_res_colmajor(o, w_out.astype(BF16), xl, m_l[2])


def _moe_block(x, m, norm_g, router_w, router_b, w_gate, w_up, w_down):
    h, comb = _ffn_norm_router(x, norm_g, m[3], m[4], router_w, router_b)
    return _moe_dense(h, comb, w_gate.astype(BF16), w_up.astype(BF16), w_down.astype(BF16), x, m[5])


def kernel(x, c, ctx, c_ctx, mod_w, mod_b, norm_mix_g, norm_ffn_g, router_w, router_b, moe_w_gate, moe_w_up,
           moe_w_down, ab_w_in, ssd_conv_w, ssd_conv_b, ssd_dt_bias, ssd_a_log, ssd_d, ssd_norm_g, cf_dw_w,
           cf_dw_b, cf_ln_g, cf_ln_b, ab_w_out, hg_w_in, hg_lb, hg_norm_g, hg_w_out, final_norm_g):
    depth = mod_w.shape[0]
    assert depth == 2, "layer schedule below is written for one even and one odd layer"
    bsz, seq, d = x.shape
    rows = seq // GRID_W
    lb_all = jnp.cumsum(jax.nn.softmax(hg_lb.astype(F32), axis=0), axis=0)
    lb_all = lb_all - lb_all[0]

    nrow = -(-(bsz + 1) // SUBLANES) * SUBLANES
    cond = jnp.zeros((nrow, d), F32).at[:bsz].set(c).at[bsz].set(c_ctx)
    mod = _modulation(cond, mod_w, mod_b).reshape(depth, nrow, N_MOD, 1, d)

    def mods(l):
        m_l = [mod[l, :bsz, k] for k in range(N_MOD)]
        m_c = [mod[l, bsz:bsz + 1, k] for k in range(N_MOD)]
        return m_c, m_l

    m_c, m_l = mods(0)
    xc, xl = _even_layer_mixer((ctx, x), (m_c, m_l), norm_mix_g[0], ab_w_in[0], ssd_conv_w[0], ssd_conv_b[0],
                               ssd_dt_bias[0], ssd_a_log[0], ssd_d[0], ssd_norm_g[0], cf_dw_w[0], cf_dw_b[0],
                               cf_ln_g[0], cf_ln_b[0], ab_w_out[0], rows)
    xl = _moe_block(xl, m_l, norm_ffn_g[0], router_w, router_b, moe_w_gate[0], moe_w_up[0], moe_w_down[0])
    xc = _moe_block(xc, m_c, norm_ffn_g[0], router_w, router_b, moe_w_gate[0], moe_w_up[0], moe_w_down[0])

    m_c, m_l = mods(1)
    xl = _odd_layer_mixer(xc, xl, m_c, m_l, norm_mix_g[1], hg_w_in[0], lb_all[1], hg_norm_g[0], hg_w_out[0])
    xl = _moe_block(xl, m_l, norm_ffn_g[1], router_w, router_b, moe_w_gate[1], moe_w_up[1], moe_w_down[1])
    return _final_norm(xl, final_norm_g)
```

```python
import functools

import jax
import jax.numpy as jnp
from jax import lax
from jax.experimental import pallas as pl
from jax.experimental.pallas import tpu as pltpu

F32 = jnp.float32
BF16 = jnp.bfloat16

D_MODEL = 1024
GRID_W = 64
EPS = 1e-6
N_MOD = 6

SSD_HEADS = 16
SSD_HEAD_DIM = 64
SSD_INNER = SSD_HEADS * SSD_HEAD_DIM
SSD_GROUPS = 4
SSD_STATE = 128
SSD_CONV = 5
SSD_CHUNK = 128
SSD_BC = SSD_GROUPS * SSD_STATE
SSD_CONV_DIM = SSD_INNER + 2 * SSD_BC
SSD_GROUP_W = SSD_INNER // SSD_GROUPS

CF_CH = 1024
CF_KERNEL = 31
CF_PAD = 16

HG_HEADS = 8
HG_DK = 128
HG_DV = 128
HG_KEY = HG_HEADS * HG_DK
HG_VAL = HG_HEADS * HG_DV
HG_CHUNK = 64

N_EXPERTS = 16
N_EXPERT_GROUPS = 4
EXPERTS_PER_GROUP = 4
D_EXPERT = 512
MOE_TL = 512
MOE_TM = 256

LANES = 128
SUBLANES = 8
HALO = 16
VMEM_LIMIT = 48 * 1024 * 1024


def _cparams(sem):
    return pltpu.CompilerParams(dimension_semantics=sem, vmem_limit_bytes=VMEM_LIMIT)


def _silu(x):
    return x * jax.nn.sigmoid(x)


def _split3(v):
    hi = v.astype(BF16)
    r1 = v - hi.astype(F32)
    mid = r1.astype(BF16)
    lo = (r1 - mid.astype(F32)).astype(BF16)
    return hi, mid, lo


def _dot01(m01, v):
    hi, mid, lo = _split3(v)
    out = jnp.dot(m01, lo, preferred_element_type=F32)
    out = out + jnp.dot(m01, mid, preferred_element_type=F32)
    return out + jnp.dot(m01, hi, preferred_element_type=F32)


def _dot_v01(v, m01):
    hi, mid, lo = _split3(v)
    out = jnp.dot(lo, m01, preferred_element_type=F32)
    out = out + jnp.dot(mid, m01, preferred_element_type=F32)
    return out + jnp.dot(hi, m01, preferred_element_type=F32)


_NN = (((1,), (0,)), ((), ()))
_NT = (((1,), (1,)), ((), ()))
_TN = (((0,), (0,)), ((), ()))


def _dot_f32(a, b, dn):
    a1, a2, a3 = _split3(a)
    b1, b2, b3 = _split3(b)
    out = lax.dot_general(a3, b1, dn, preferred_element_type=F32)
    out = out + lax.dot_general(a1, b3, dn, preferred_element_type=F32)
    out = out + lax.dot_general(a2, b2, dn, preferred_element_type=F32)
    out = out + lax.dot_general(a2, b1, dn, preferred_element_type=F32)
    out = out + lax.dot_general(a1, b2, dn, preferred_element_type=F32)
    return out + lax.dot_general(a1, b1, dn, preferred_element_type=F32)


def _scan_tri(n, rev):
    r = lax.broadcasted_iota(jnp.int32, (n, n), 0)
    c = lax.broadcasted_iota(jnp.int32, (n, n), 1)
    return ((c >= r) if rev else (c <= r))


def _mod_kernel(c_ref, w_ref, b_ref, o_ref):
    cond = _silu(c_ref[...])
    o_ref[0] = _dot_f32(cond, w_ref[0], _NN) + b_ref[0]


def _modulation(cond, mod_w, mod_b):
    depth, d, n = mod_w.shape
    r = cond.shape[0]
    tn = 512
    return pl.pallas_call(
        _mod_kernel,
        grid=(depth, n // tn),
        in_specs=[pl.BlockSpec((r, d), lambda l, j: (0, 0)),
                  pl.BlockSpec((1, d, tn), lambda l, j: (l, 0, j)),
                  pl.BlockSpec((1, 1, tn), lambda l, j: (l, 0, j))],
        out_specs=pl.BlockSpec((1, r, tn), lambda l, j: (l, 0, j)),
        out_shape=jax.ShapeDtypeStruct((depth, r, n), F32),
        compiler_params=_cparams(("parallel", "parallel")),
        name="modulation",
    )(cond, mod_w, mod_b.reshape(depth, 1, n))


def _normmod(x, g, shift, scale):
    ms = jnp.mean(x * x, axis=-1, keepdims=True)
    return (x * lax.rsqrt(ms + EPS) * g) * (1.0 + scale) + shift


def _normmod_kernel(x_ref, g_ref, sh_ref, sc_ref, o_ref):
    o_ref[0] = _normmod(x_ref[0], g_ref[...], sh_ref[0], sc_ref[0]).astype(o_ref.dtype)


def _mod_spec(nb):
    return pl.BlockSpec((1, 1, D_MODEL), (lambda b, *_: (b, 0, 0)) if nb > 1 else (lambda b, *_: (0, 0, 0)))


def _norm_mod(x, g, shift, scale):
    bsz, seq, d = x.shape
    tl = min(seq, 512)
    return pl.pallas_call(
        _normmod_kernel,
        grid=(bsz, seq // tl),
        in_specs=[pl.BlockSpec((1, tl, d), lambda b, i: (b, i, 0)),
                  pl.BlockSpec((1, d), lambda b, i: (0, 0)),
                  _mod_spec(shift.shape[0]), _mod_spec(scale.shape[0])],
        out_specs=pl.BlockSpec((1, tl, d), lambda b, i: (b, i, 0)),
        out_shape=jax.ShapeDtypeStruct((bsz, seq, d), BF16),
        compiler_params=_cparams(("parallel", "parallel")),
        name="norm_mod",
    )(x, g.reshape(1, d), shift, scale)


def _normmod_cm_kernel(x_ref, g_ref, sh_ref, sc_ref, o_ref, *, rows):
    for c in range(SUBLANES):
        piece = x_ref[0, :, c, :]
        o_ref[0, c * rows:(c + 1) * rows, :] = _normmod(piece, g_ref[...], sh_ref[0], sc_ref[0]).astype(o_ref.dtype)


def _norm_mod_colmajor(x, g, shift, scale):
    bsz, seq, d = x.shape
    rows = seq // GRID_W
    x4 = x.reshape(bsz, rows, GRID_W, d)
    return pl.pallas_call(
        functools.partial(_normmod_cm_kernel, rows=rows),
        grid=(bsz, GRID_W // SUBLANES),
        in_specs=[pl.BlockSpec((1, rows, SUBLANES, d), lambda b, i: (b, 0, i, 0)),
                  pl.BlockSpec((1, d), lambda b, i: (0, 0)),
                  _mod_spec(shift.shape[0]), _mod_spec(scale.shape[0])],
        out_specs=pl.BlockSpec((1, SUBLANES * rows, d), lambda b, i: (b, i, 0)),
        out_shape=jax.ShapeDtypeStruct((bsz, seq, d), BF16),
        compiler_params=_cparams(("parallel", "parallel")),
        name="norm_mod_colmajor",
    )(x4, g.reshape(1, d), shift, scale)


def _mm_kernel(a_ref, b_ref, o_ref):
    o_ref[0] = jnp.dot(a_ref[0], b_ref[...], preferred_element_type=F32).astype(o_ref.dtype)


def _mm_res_kernel(a_ref, b_ref, res_ref, gate_ref, o_ref):
    y = jnp.dot(a_ref[0], b_ref[...], preferred_element_type=F32)
    o_ref[0] = res_ref[0] + gate_ref[0] * y


def _mm_res_cm_kernel(a_ref, b_ref, res_ref, gate_ref, o_ref, *, rows):
    y = jnp.dot(a_ref[0], b_ref[...], preferred_element_type=F32)
    for c in range(SUBLANES):
        o_ref[0, :, c, :] = res_ref[0, :, c, :] + gate_ref[0] * y[c * rows:(c + 1) * rows, :]


def _matmul(a, w, out_dtype):
    bsz, seq, k = a.shape
    n = w.shape[1]
    tm, tn = min(seq, 1024), min(n, 512)
    return pl.pallas_call(
        _mm_kernel,
        grid=(bsz, seq // tm, n // tn),
        in_specs=[pl.BlockSpec((1, tm, k), lambda b, i, j: (b, i, 0)),
                  pl.BlockSpec((k, tn), lambda b, i, j: (0, j))],
        out_specs=pl.BlockSpec((1, tm, tn), lambda b, i, j: (b, i, j)),
        out_shape=jax.ShapeDtypeStruct((bsz, seq, n), out_dtype),
        compiler_params=_cparams(("parallel", "parallel", "parallel")),
        name="matmul",
    )(a, w)


def _matmul_res(a, w, res, gate):
    bsz, seq, k = a.shape
    n = w.shape[1]
    tm, tn = min(seq, 1024), min(n, 512)
    nb = gate.shape[0]
    return pl.pallas_call(
        _mm_res_kernel,
        grid=(bsz, seq // tm, n // tn),
        in_specs=[pl.BlockSpec((1, tm, k), lambda b, i, j: (b, i, 0)),
                  pl.BlockSpec((k, tn), lambda b, i, j: (0, j)),
                  pl.BlockSpec((1, tm, tn), lambda b, i, j: (b, i, j)),
                  pl.BlockSpec((1, 1, tn), (lambda b, i, j: (b, 0, j)) if nb > 1 else (lambda b, i, j: (0, 0, j)))],
        out_specs=pl.BlockSpec((1, tm, tn), lambda b, i, j: (b, i, j)),
        out_shape=jax.ShapeDtypeStruct((bsz, seq, n), F32),
        compiler_params=_cparams(("parallel", "parallel", "parallel")),
        name="matmul_res",
    )(a, w, res, gate)


def _matmul_res_colmajor(a, w, res, gate):
    bsz, seq, k = a.shape
    n = w.shape[1]
    rows = seq // GRID_W
    tm, tn = SUBLANES * rows, min(n, 512)
    res4 = res.reshape(bsz, rows, GRID_W, n)
    out = pl.pallas_call(
        functools.partial(_mm_res_cm_kernel, rows=rows),
        grid=(bsz, seq // tm, n // tn),
        in_specs=[pl.BlockSpec((1, tm, k), lambda b, i, j: (b, i, 0)),
                  pl.BlockSpec((k, tn), lambda b, i, j: (0, j)),
                  pl.BlockSpec((1, rows, SUBLANES, tn), lambda b, i, j: (b, 0, i, j)),
                  pl.BlockSpec((1, 1, tn), lambda b, i, j: (b, 0, j))],
        out_specs=pl.BlockSpec((1, rows, SUBLANES, tn), lambda b, i, j: (b, 0, i, j)),
        out_shape=jax.ShapeDtypeStruct((bsz, rows, GRID_W, n), F32),
        compiler_params=_cparams(("parallel", "parallel", "parallel")),
        name="matmul_res_colmajor",
    )(a, w, res4, gate)
    return out.reshape(bsz, seq, n)


def _ssd_kernel(*refs, rev, nchunks, readout):
    if readout:
        (xbc_ref, prev_ref, next_ref, dt_ref, cw_ref, cb_ref, dtb_ref, alog_ref, dsk_ref, s0_ref,
         yin_ref, z_ref, ng_ref, y_ref, sfin_ref, pad_ref, st_ref) = refs
    else:
        (xbc_ref, prev_ref, next_ref, dt_ref, cw_ref, cb_ref, dtb_ref, alog_ref, dsk_ref, s0_ref,
         y_ref, sfin_ref, pad_ref, st_ref) = refs
    step = pl.program_id(1)
    u = (nchunks - 1 - step) if rev else step
    ck = SSD_CHUNK

    @pl.when(step == 0)
    def _():
        st_ref[...] = s0_ref[0]

    pad_ref[HALO:HALO + ck, :] = xbc_ref[0].astype(F32)
    pad_ref[0:HALO, :] = jnp.where(u > 0, prev_ref[0].astype(F32), 0.0)
    pad_ref[HALO + ck:2 * HALO + ck, :] = jnp.where(u < nchunks - 1, next_ref[0].astype(F32), 0.0)
    half = SSD_CONV // 2
    acc = cb_ref[...] + cw_ref[0:1, :] * pad_ref[pl.ds(HALO - half, ck), :]
    for k in range(1, SSD_CONV):
        acc = acc + cw_ref[k:k + 1, :] * pad_ref[pl.ds(HALO - half + k, ck), :]
    xbc = _silu(acc)
    xs = xbc[:, :SSD_INNER]

    dt = jax.nn.softplus(dt_ref[0] + dtb_ref[...])
    la = dt * (-jnp.exp(alog_ref[...]))
    tri = _scan_tri(ck, rev).astype(BF16)
    acum = _dot01(tri, la)
    last = 0 if rev else ck - 1
    total = acum[last:last + 1, :]
    acum_t = acum.T

    hrow = lax.broadcasted_iota(jnp.int32, (LANES, SSD_INNER), 0)
    hcol = lax.broadcasted_iota(jnp.int32, (LANES, SSD_INNER), 1)
    expand = (hcol // SSD_HEAD_DIM == hrow).astype(BF16)
    dt_x = _dot_v01(dt, expand)
    ea_x = _dot_v01(jnp.exp(acum), expand)
    te_x = _dot_v01(jnp.exp(total - acum), expand)
    cd_x = _dot_v01(jnp.broadcast_to(jnp.exp(total), (SUBLANES, LANES)), expand)[0:1, :]

    xdt = xs * dt_x
    xdt_b = xdt.astype(BF16)
    xw_b = (xdt * te_x).astype(BF16)

    r = lax.broadcasted_iota(jnp.int32, (ck, ck), 0)
    c = lax.broadcasted_iota(jnp.int32, (ck, ck), 1)
    causal = (c >= r) if rev else (c <= r)
    lane = lax.broadcasted_iota(jnp.int32, (ck, LANES), 1)
    nt = (((1,), (1,)), ((), ()))
    tn = (((0,), (0,)), ((), ()))

    y_parts = []
    for g in range(SSD_GROUPS):
        b_g = xbc[:, SSD_INNER + g * SSD_STATE:SSD_INNER + (g + 1) * SSD_STATE].astype(BF16)
        c_g = xbc[:, SSD_INNER + SSD_BC + g * SSD_STATE:SSD_INNER + SSD_BC + (g + 1) * SSD_STATE].astype(BF16)
        cb = lax.dot_general(c_g, b_g, nt, preferred_element_type=F32)
        lo = g * SSD_GROUP_W
        s_in = st_ref[g]
        y_off = jnp.dot(c_g, s_in.astype(BF16), preferred_element_type=F32) * ea_x[:, lo:lo + SSD_GROUP_W]
        chunk_state = lax.dot_general(b_g, xw_b[:, lo:lo + SSD_GROUP_W], tn, preferred_element_type=F32)
        st_ref[g] = cd_x[:, lo:lo + SSD_GROUP_W] * s_in + chunk_state
        heads_per_group = SSD_HEADS // SSD_GROUPS
        for pair in range(heads_per_group // 2):
            plo = lo + pair * LANES
            xpair = xdt_b[:, plo:plo + LANES]
            y_pair = y_off[:, pair * LANES:(pair + 1) * LANES]
            for half_i in range(2):
                h = g * heads_per_group + pair * 2 + half_i
                seg = acum[:, h:h + 1] - acum_t[h:h + 1, :]
                m = (cb * jnp.where(causal, jnp.exp(seg), 0.0)).astype(BF16)
                in_half = (lane >= half_i * SSD_HEAD_DIM) & (lane < (half_i + 1) * SSD_HEAD_DIM)
                y_pair = y_pair + jnp.dot(m, jnp.where(in_half, xpair, jnp.zeros_like(xpair)),
                                          preferred_element_type=F32)
            y_parts.append(y_pair)
    y = jnp.concatenate(y_parts, axis=1)

    if readout:
        y = y + yin_ref[0].astype(F32)
        y = y * _silu(z_ref[0].astype(F32))
        ms = jnp.mean(y * y, axis=-1, keepdims=True)
        y_ref[0] = (y * lax.rsqrt(ms + EPS) * ng_ref[...]).astype(y_ref.dtype)
    else:
        y_ref[0] = (y + dsk_ref[...] * xs).astype(y_ref.dtype)

    @pl.when(step == nchunks - 1)
    def _():
        sfin_ref[0] = st_ref[...]


def _ssd_scan(xbc, dt_raw, conv_w, conv_b, dt_bias, a_log, d_skip, s0, rev, y_in=None, z=None, norm_g=None):
    bsz, seq, _ = xbc.shape
    nchunks = seq // SSD_CHUNK
    hb = SSD_CHUNK // HALO
    nhalo = seq // HALO
    readout = y_in is not None

    def cidx(i):
        return (nchunks - 1 - i) if rev else i

    def pad_lanes(v):
        return jnp.pad(v.reshape(1, -1), ((0, 0), (0, LANES - v.shape[-1])))

    in_specs = [
        pl.BlockSpec((1, SSD_CHUNK, SSD_CONV_DIM), lambda b, i: (b, cidx(i), 0)),
        pl.BlockSpec((1, HALO, SSD_CONV_DIM), lambda b, i: (b, jnp.maximum(cidx(i) * hb - 1, 0), 0)),
        pl.BlockSpec((1, HALO, SSD_CONV_DIM), lambda b, i: (b, jnp.minimum((cidx(i) + 1) * hb, nhalo - 1), 0)),
        pl.BlockSpec((1, SSD_CHUNK, LANES), lambda b, i: (b, cidx(i), 0)),
        pl.BlockSpec((SUBLANES, SSD_CONV_DIM), lambda b, i: (0, 0)),
        pl.BlockSpec((1, SSD_CONV_DIM), lambda b, i: (0, 0)),
        pl.BlockSpec((1, LANES), lambda b, i: (0, 0)),
        pl.BlockSpec((1, LANES), lambda b, i: (0, 0)),
        pl.BlockSpec((1, SSD_INNER), lambda b, i: (0, 0)),
        pl.BlockSpec((1, SSD_GROUPS, SSD_STATE, SSD_GROUP_W), lambda b, i: (b, 0, 0, 0)),
    ]
    args = [xbc, xbc, xbc, dt_raw,
            jnp.pad(conv_w, ((0, SUBLANES - SSD_CONV), (0, 0))), conv_b.reshape(1, -1),
            pad_lanes(dt_bias), pad_lanes(a_log), jnp.repeat(d_skip, SSD_HEAD_DIM).reshape(1, -1), s0]
    if readout:
        in_specs += [pl.BlockSpec((1, SSD_CHUNK, SSD_INNER), lambda b, i: (b, cidx(i), 0)),
                     pl.BlockSpec((1, SSD_CHUNK, SSD_INNER), lambda b, i: (b, cidx(i), 0)),
                     pl.BlockSpec((1, SSD_INNER), lambda b, i: (0, 0))]
        args += [y_in, z, norm_g.reshape(1, -1)]
    return pl.pallas_call(
        functools.partial(_ssd_kernel, rev=rev, nchunks=nchunks, readout=readout),
        grid=(bsz, nchunks),
        in_specs=in_specs,
        out_specs=[pl.BlockSpec((1, SSD_CHUNK, SSD_INNER), lambda b, i: (b, cidx(i), 0)),
                   pl.BlockSpec((1, SSD_GROUPS, SSD_STATE, SSD_GROUP_W), lambda b, i: (b, 0, 0, 0))],
        out_shape=[jax.ShapeDtypeStruct((bsz, seq, SSD_INNER), BF16),
                   jax.ShapeDtypeStruct((bsz, SSD_GROUPS, SSD_STATE, SSD_GROUP_W), F32)],
        scratch_shapes=[pltpu.VMEM((SSD_CHUNK + 2 * HALO, SSD_CONV_DIM), F32),
                        pltpu.VMEM((SSD_GROUPS, SSD_STATE, SSD_GROUP_W), F32)],
        compiler_params=_cparams(("parallel", "arbitrary")),
        name="ssd_scan_rev" if rev else "ssd_scan_fwd",
    )(*args)


def _conformer_kernel(p_ref, w_ref, b_ref, g_ref, beta_ref, o_ref, pad_ref, conv_ref, *, nseq, slen):
    v = p_ref[0, :, :CF_CH].astype(F32)
    gate = p_ref[0, :, CF_CH:].astype(F32)
    u = v * jax.nn.sigmoid(gate)
    half = CF_KERNEL // 2
    for s in range(nseq):
        pad_ref[s, 0:CF_PAD, :] = jnp.zeros((CF_PAD, CF_CH), F32)
        pad_ref[s, CF_PAD + slen:2 * CF_PAD + slen, :] = jnp.zeros((CF_PAD, CF_CH), F32)
        pad_ref[s, CF_PAD:CF_PAD + slen, :] = u[s * slen:(s + 1) * slen, :]
    for s in range(nseq):
        for cb in range(CF_CH // LANES):
            ch = slice(cb * LANES, (cb + 1) * LANES)
            acc = b_ref[:, ch] + w_ref[0:1, ch] * pad_ref[s, pl.ds(CF_PAD - half, slen), ch]
            for k in range(1, CF_KERNEL):
                acc = acc + w_ref[k:k + 1, ch] * pad_ref[s, pl.ds(CF_PAD - half + k, slen), ch]
            conv_ref[s * slen:(s + 1) * slen, ch] = acc
    y = conv_ref[...]
    mu = jnp.mean(y, axis=-1, keepdims=True)
    xc = y - mu
    var = jnp.mean(xc * xc, axis=-1, keepdims=True)
    y = xc * lax.rsqrt(var + EPS) * g_ref[...] + beta_ref[...]
    o_ref[0] = _silu(y).astype(o_ref.dtype)


def _conformer(p_cf, dw_w, dw_b, ln_g, ln_b, slen):
    bsz, seq, _ = p_cf.shape
    nseq = max(1, min(seq, 256) // slen)
    tb = nseq * slen
    kpad = -(-CF_KERNEL // SUBLANES) * SUBLANES
    return pl.pallas_call(
        functools.partial(_conformer_kernel, nseq=nseq, slen=slen),
        grid=(bsz, seq // tb),
        in_specs=[pl.BlockSpec((1, tb, 2 * CF_CH), lambda b, i: (b, i, 0)),
                  pl.BlockSpec((kpad, CF_CH), lambda b, i: (0, 0)),
                  pl.BlockSpec((1, CF_CH), lambda b, i: (0, 0)),
                  pl.BlockSpec((1, CF_CH), lambda b, i: (0, 0)),
                  pl.BlockSpec((1, CF_CH), lambda b, i: (0, 0))],
        out_specs=pl.BlockSpec((1, tb, CF_CH), lambda b, i: (b, i, 0)),
        out_shape=jax.ShapeDtypeStruct((bsz, seq, CF_CH), BF16),
        scratch_shapes=[pltpu.VMEM((nseq, slen + 2 * CF_PAD, CF_CH), F32), pltpu.VMEM((tb, CF_CH), F32)],
        compiler_params=_cparams(("parallel", "parallel")),
        name="conformer_conv",
    )(p_cf, jnp.pad(dw_w, ((0, kpad - CF_KERNEL), (0, 0))), dw_b.reshape(1, -1), ln_g.reshape(1, -1),
      ln_b.reshape(1, -1))


def _hgrn2_kernel(*refs, rev, nchunks, mode):
    if mode == "state":
        v_ref, f_ref, lb_ref, s0_ref, sfin_ref, st_ref = refs
    elif mode == "out":
        q_ref, v_ref, f_ref, lb_ref, s0_ref, o_ref, st_ref = refs
    else:
        q_ref, v_ref, f_ref, lb_ref, s0_ref, oin_ref, g_ref, ng_ref, o_ref, st_ref = refs
    step = pl.program_id(1)
    ck = HG_CHUNK

    @pl.when(step == 0)
    def _():
        st_ref[...] = s0_ref[0]

    lb = lb_ref[...]
    f = lb + (1.0 - lb) * jax.nn.sigmoid(f_ref[0])
    kk = 1.0 - f
    gcum = _dot01(_scan_tri(ck, rev).astype(BF16), jnp.log(f))
    last = 0 if rev else ck - 1
    g_end = gcum[last:last + 1, :]
    k_end = (kk * jnp.exp(g_end - gcum)).astype(BF16)
    dec_end = jnp.exp(g_end)
    v = v_ref[0]
    nt = (((1,), (1,)), ((), ()))
    tn = (((0,), (0,)), ((), ()))

    if mode != "state":
        mid_pos = HG_CHUNK // 2 - 1
        mid = (ck - 1 - mid_pos) if rev else mid_pos
        g_mid = gcum[mid:mid + 1, :]
        q = _silu(q_ref[0].astype(F32))
        q_rel = (q * jnp.exp(gcum - g_mid)).astype(BF16)
        k_rel = (kk * jnp.exp(g_mid - gcum)).astype(BF16)
        q_dec = (q * jnp.exp(gcum)).astype(BF16)
        r = lax.broadcasted_iota(jnp.int32, (ck, ck), 0)
        c = lax.broadcasted_iota(jnp.int32, (ck, ck), 1)
        causal = (c >= r) if rev else (c <= r)

    outs = []
    for h in range(HG_HEADS):
        ks = slice(h * HG_DK, (h + 1) * HG_DK)
        vs = slice(h * HG_DV, (h + 1) * HG_DV)
        s_in = st_ref[h]
        if mode != "state":
            att = lax.dot_general(q_rel[:, ks], k_rel[:, ks], nt, preferred_element_type=F32)
            att = jnp.where(causal, att, 0.0).astype(BF16)
            o_h = jnp.dot(att, v[:, vs], preferred_element_type=F32)
            o_h = o_h + lax.dot_general(q_dec[:, ks], s_in.astype(BF16), nt, preferred_element_type=F32)
            outs.append(o_h)
        chunk_state = lax.dot_general(v[:, vs], k_end[:, ks], tn, preferred_element_type=F32)
        st_ref[h] = s_in * dec_end[:, ks] + chunk_state

    if mode == "out":
        o_ref[0] = jnp.concatenate(outs, axis=1).astype(o_ref.dtype)
    elif mode == "readout":
        gate = _silu(g_ref[0].astype(F32))
        for h in range(HG_HEADS):
            vs = slice(h * HG_DV, (h + 1) * HG_DV)
            o_h = outs[h] + oin_ref[0, :, vs].astype(F32)
            ms = jnp.mean(o_h * o_h, axis=-1, keepdims=True)
            o_ref[0, :, vs] = ((o_h * lax.rsqrt(ms + EPS) * ng_ref[...]) * gate[:, vs]).astype(o_ref.dtype)
    else:
        @pl.when(step == nchunks - 1)
        def _():
            sfin_ref[0] = st_ref[...]


def _hgrn2_scan(v, f_raw, lb, s0, rev, mode, q=None, o_in=None, g=None, norm_g=None):
    bsz, seq, _ = v.shape
    nchunks = seq // HG_CHUNK
    dcol = 1 if rev else 0

    def cidx(i):
        return (nchunks - 1 - i) if rev else i

    tok = lambda width: pl.BlockSpec((1, HG_CHUNK, width), lambda b, i: (b, cidx(i), 0))
    state_spec = pl.BlockSpec((1, HG_HEADS, HG_DV, HG_DK), lambda b, i: (b, 0, 0, 0))
    f_spec = pl.BlockSpec((1, HG_CHUNK, HG_KEY), lambda b, i: (b, cidx(i), dcol))
    lb_spec = pl.BlockSpec((1, HG_KEY), lambda b, i: (0, 0))
    state_shape = jax.ShapeDtypeStruct((bsz, HG_HEADS, HG_DV, HG_DK), F32)
    if mode == "state":
        in_specs, args = [tok(HG_VAL), f_spec, lb_spec, state_spec], [v, f_raw, lb.reshape(1, -1), s0]
        out_specs, out_shape = state_spec, state_shape
    else:
        in_specs = [tok(HG_KEY), tok(HG_VAL), f_spec, lb_spec, state_spec]
        args = [q, v, f_raw, lb.reshape(1, -1), s0]
        if mode == "readout":
            in_specs += [tok(HG_VAL), tok(HG_VAL), pl.BlockSpec((1, HG_DV), lambda b, i: (0, 0))]
            args += [o_in, g, norm_g.reshape(1, -1)]
        out_specs, out_shape = tok(HG_VAL), jax.ShapeDtypeStruct((bsz, seq, HG_VAL), BF16)
    return pl.pallas_call(
        functools.partial(_hgrn2_kernel, rev=rev, nchunks=nchunks, mode=mode),
        grid=(bsz, nchunks),
        in_specs=in_specs, out_specs=out_specs, out_shape=out_shape,
        scratch_shapes=[pltpu.VMEM((HG_HEADS, HG_DV, HG_DK), F32)],
        compiler_params=_cparams(("parallel", "arbitrary")),
        name=f"hgrn2_{mode}_{'rev' if rev else 'fwd'}",
    )(*args)


def _first_argmax(vals):
    best, idx = vals[0], jnp.zeros(vals[0].shape, jnp.int32)
    for j in range(1, len(vals)):
        better = vals[j] > best
        idx = jnp.where(better, j, idx)
        best = jnp.where(better, vals[j], best)
    return idx, best


def _pick(idx, vals):
    out = vals[-1]
    for j in range(len(vals) - 2, -1, -1):
        out = jnp.where(idx == j, vals[j], out)
    return out


def _store_token_tiles(ref, v):
    n = v.shape[0]
    for j in range(SUBLANES):
        ref[pl.ds(j, n, stride=SUBLANES), :] = v[:, j * LANES:(j + 1) * LANES]


def _load_token_tiles(ref):
    n = ref.shape[0] // SUBLANES
    return jnp.concatenate([ref[pl.ds(j, n, stride=SUBLANES), :] for j in range(SUBLANES)], axis=1)


def _token_tile(ref, idx):
    return ref.at[pl.ds(pl.multiple_of(idx * SUBLANES, SUBLANES), SUBLANES)]


def _router_kernel(x_ref, g_ref, sh_ref, sc_ref, rw_ref, rb_ref, h_ref, slot_ref, wt_ref, cnt_ref, carry_ref):
    step = pl.program_id(0)

    @pl.when(step == 0)
    def _():
        carry_ref[...] = jnp.zeros_like(carry_ref)

    h = _normmod(x_ref[...], g_ref[...], sh_ref[0], sc_ref[0])
    _store_token_tiles(h_ref, h)
    scores = jax.nn.sigmoid(_dot_f32(rw_ref[...], h, _NT))
    sel = scores + rb_ref[...]
    srow = [sel[e:e + 1, :] for e in range(N_EXPERTS)]
    prow = [scores[e:e + 1, :] for e in range(N_EXPERTS)]
    gscore = []
    for gi in range(N_EXPERT_GROUPS):
        m = srow[gi * EXPERTS_PER_GROUP:(gi + 1) * EXPERTS_PER_GROUP]
        pair_sums = [m[i] + m[j] for i in range(EXPERTS_PER_GROUP) for j in range(i + 1, EXPERTS_PER_GROUP)]
        best = pair_sums[0]
        for p in pair_sums[1:]:
            best = jnp.maximum(best, p)
        gscore.append(best)
    gidx, _ = _first_argmax(gscore)
    in_sel = [_pick(gidx, [srow[gi * EXPERTS_PER_GROUP + j] for gi in range(N_EXPERT_GROUPS)])
              for j in range(EXPERTS_PER_GROUP)]
    in_p = [_pick(gidx, [prow[gi * EXPERTS_PER_GROUP + j] for gi in range(N_EXPERT_GROUPS)])
            for j in range(EXPERTS_PER_GROUP)]
    i1, _ = _first_argmax(in_sel)
    i2, _ = _first_argmax([jnp.where(i1 == j, -jnp.inf, in_sel[j]) for j in range(EXPERTS_PER_GROUP)])
    w1, w2 = _pick(i1, in_p), _pick(i2, in_p)
    den = w1 + w2
    e1, e2 = gidx * EXPERTS_PER_GROUP + i1, gidx * EXPERTS_PER_GROUP + i2
    tl = h.shape[0]
    erow = lax.broadcasted_iota(jnp.int32, (N_EXPERTS, tl), 0)
    oh1, oh2 = erow == e1, erow == e2
    cnt = (oh1 | oh2).astype(F32)
    r = lax.broadcasted_iota(jnp.int32, (tl, tl), 0)
    c = lax.broadcasted_iota(jnp.int32, (tl, tl), 1)
    before = jnp.dot(cnt.astype(BF16), (r < c).astype(BF16), preferred_element_type=F32)
    base = carry_ref[:, 0:1] + before
    rank1 = jnp.sum(jnp.where(oh1, base, 0.0), axis=0, keepdims=True).astype(jnp.int32)
    rank2 = jnp.sum(jnp.where(oh2, base, 0.0), axis=0, keepdims=True).astype(jnp.int32)
    carry_ref[...] = carry_ref[...] + jnp.sum(cnt, axis=1, keepdims=True)
    cnt_ref[...] = carry_ref[...].astype(jnp.int32)
    srow = lax.broadcasted_iota(jnp.int32, (SUBLANES, tl), 0)
    slot_ref[0] = jnp.where(srow == 0, e1, jnp.where(srow == 1, e2, jnp.where(srow == 2, rank1, rank2)))
    wrow = lax.broadcasted_iota(jnp.int32, (LANES, tl), 0)
    wt_ref[...] = jnp.where(wrow == 0, w1 / den, jnp.where(wrow == 1, w2 / den, 0.0)).T


def _ffn_norm_router(x, g, shift, scale, router_w, router_b, seq):
    t, d = x.shape
    tl = min(seq, MOE_TL)
    nblk = t // tl
    nb = shift.shape[0]
    mod_spec = pl.BlockSpec((1, 1, d), (lambda i: (i * tl // seq, 0, 0)) if nb > 1 else (lambda i: (0, 0, 0)))
    return pl.pallas_call(
        _router_kernel,
        grid=(nblk,),
        in_specs=[pl.BlockSpec((tl, d), lambda i: (i, 0)),
                  pl.BlockSpec((1, d), lambda i: (0, 0)),
                  mod_spec, mod_spec,
                  pl.BlockSpec((N_EXPERTS, d), lambda i: (0, 0)),
                  pl.BlockSpec((N_EXPERTS, 1), lambda i: (0, 0))],
        out_specs=[pl.BlockSpec((tl * SUBLANES, LANES), lambda i: (i, 0)),
                   pl.BlockSpec((1, SUBLANES, tl), lambda i: (i, 0, 0)),
                   pl.BlockSpec((tl, LANES), lambda i: (i, 0)),
                   pl.BlockSpec((N_EXPERTS, LANES), lambda i: (0, 0))],
        out_shape=[jax.ShapeDtypeStruct((t * SUBLANES, LANES), F32),
                   jax.ShapeDtypeStruct((nblk, SUBLANES, tl), jnp.int32),
                   jax.ShapeDtypeStruct((t, LANES), F32),
                   jax.ShapeDtypeStruct((N_EXPERTS, LANES), jnp.int32)],
        scratch_shapes=[pltpu.VMEM((N_EXPERTS, LANES), F32)],
        compiler_params=_cparams(("arbitrary",)),
        name="ffn_norm_router",
    )(x, g.reshape(1, d), shift, scale, router_w.T, router_b.reshape(N_EXPERTS, 1))


DMA_UNROLL = 8


def _dispatch_kernel(slot_ref, ends_ref, h_ref, hs_ref, zero_ref, sem, *, tl, max_tiles):
    @pl.when(pl.program_id(0) == 0)
    def _():
        zero_ref[...] = jnp.zeros_like(zero_ref)
        fills = []
        for e in range(N_EXPERTS):
            end = ends_ref[e]
            gap = (MOE_TM - (end & (MOE_TM - 1))) & (MOE_TM - 1)
            for bit in range(MOE_TM.bit_length() - 1):
                size = 1 << bit
                first_row = pl.multiple_of((end + (gap & (size - 1))) * SUBLANES, SUBLANES)
                fills.append(((gap & size) != 0,
                              pltpu.make_async_copy(zero_ref.at[pl.ds(0, size * SUBLANES)],
                                                    hs_ref.at[pl.ds(first_row, size * SUBLANES)], sem)))
        for n in range(N_EXPERTS):
            first_slot = ends_ref[N_EXPERTS] + n * MOE_TM
            first_row = pl.multiple_of(first_slot * SUBLANES, SUBLANES)
            fills.append((first_slot < max_tiles * MOE_TM,
                          pltpu.make_async_copy(zero_ref, hs_ref.at[pl.ds(first_row, MOE_TM * SUBLANES)], sem)))
        for cond, cp in fills:
            pl.when(cond)(cp.start)
        for cond, cp in fills:
            pl.when(cond)(cp.wait)

    def issue(t, carry):
        pltpu.make_async_copy(_token_tile(h_ref, t), _token_tile(hs_ref, slot_ref[0, 0, t]), sem).start()
        pltpu.make_async_copy(_token_tile(h_ref, t), _token_tile(hs_ref, slot_ref[0, 0, tl + t]), sem).start()
        return carry

    lax.fori_loop(0, tl, issue, 0, unroll=DMA_UNROLL)
    for _ in range(2):
        pltpu.make_async_copy(h_ref, hs_ref.at[pl.ds(0, tl * SUBLANES)], sem).wait()


def _dispatch(h, slot2, ends, tl, max_tiles):
    t = h.shape[0] // SUBLANES
    nblk = t // tl
    return pl.pallas_call(
        functools.partial(_dispatch_kernel, tl=tl, max_tiles=max_tiles),
        grid=(nblk,),
        in_specs=[pl.BlockSpec((1, 1, 2 * tl), lambda i: (i, 0, 0), memory_space=pltpu.SMEM),
                  pl.BlockSpec(memory_space=pltpu.SMEM),
                  pl.BlockSpec((tl * SUBLANES, LANES), lambda i: (i, 0))],
        out_specs=pl.BlockSpec(memory_space=pl.ANY),
        out_shape=jax.ShapeDtypeStruct((max_tiles * MOE_TM * SUBLANES, LANES), F32),
        scratch_shapes=[pltpu.VMEM((MOE_TM * SUBLANES, LANES), F32), pltpu.SemaphoreType.DMA(())],
        compiler_params=_cparams(("arbitrary",)),
        name="moe_dispatch",
    )(slot2, ends, h)


def _expert_kernel(te_ref, tf_ref, nt_ref, hs_ref, wg_ref, wu_ref, wd_ref, ys_ref, wgb_ref, wub_ref, wdb_ref):
    n = pl.program_id(0)

    @pl.when(n >= nt_ref[0])
    def _():
        ys_ref[...] = jnp.zeros_like(ys_ref)

    @pl.when(n < nt_ref[0])
    def _():
        @pl.when(tf_ref[n] == 1)
        def _():
            wgb_ref[...] = wg_ref[0].astype(BF16)
            wub_ref[...] = wu_ref[0].astype(BF16)
            wdb_ref[...] = wd_ref[0].astype(BF16)

        h = _load_token_tiles(hs_ref).astype(BF16)
        a = jnp.dot(h, wgb_ref[...], preferred_element_type=F32)
        b = jnp.dot(h, wub_ref[...], preferred_element_type=F32)
        y = jnp.dot((_silu(a) * b).astype(BF16), wdb_ref[...], preferred_element_type=F32)
        _store_token_tiles(ys_ref, y)


def _experts(hs, tile_expert, tile_first, n_tiles, w_gate, w_up, w_down, max_tiles):
    d = w_gate.shape[1]
    tile_spec = pl.BlockSpec((MOE_TM * SUBLANES, LANES), lambda n, te, tf, nt: (n, 0))
    grid_spec = pltpu.PrefetchScalarGridSpec(
        num_scalar_prefetch=3,
        grid=(max_tiles,),
        in_specs=[tile_spec,
                  pl.BlockSpec((1, d, D_EXPERT), lambda n, te, tf, nt: (te[n], 0, 0)),
                  pl.BlockSpec((1, d, D_EXPERT), lambda n, te, tf, nt: (te[n], 0, 0)),
                  pl.BlockSpec((1, D_EXPERT, d), lambda n, te, tf, nt: (te[n], 0, 0))],
        out_specs=tile_spec,
        scratch_shapes=[pltpu.VMEM((d, D_EXPERT), BF16), pltpu.VMEM((d, D_EXPERT), BF16),
                        pltpu.VMEM((D_EXPERT, d), BF16)])
    return pl.pallas_call(
        _expert_kernel,
        grid_spec=grid_spec,
        out_shape=jax.ShapeDtypeStruct((max_tiles * MOE_TM * SUBLANES, LANES), F32),
        compiler_params=_cparams(("arbitrary",)),
        name="moe_experts",
    )(tile_expert, tile_first, n_tiles, hs, w_gate, w_up, w_down)


def _combine_kernel(slot_ref, wt_ref, res_ref, gate_ref, fg_ref, ys_ref, o_ref, y1_ref, y2_ref, sem, *, tl, final):
    def issue(t, carry):
        pltpu.make_async_copy(_token_tile(ys_ref, slot_ref[0, 0, t]), _token_tile(y1_ref, t), sem).start()
        pltpu.make_async_copy(_token_tile(ys_ref, slot_ref[0, 0, tl + t]), _token_tile(y2_ref, t), sem).start()
        return carry

    lax.fori_loop(0, tl, issue, 0, unroll=DMA_UNROLL)
    pltpu.make_async_copy(ys_ref.at[pl.ds(0, tl * SUBLANES)], y1_ref, sem).wait()
    pltpu.make_async_copy(ys_ref.at[pl.ds(0, tl * SUBLANES)], y2_ref, sem).wait()
    wt = wt_ref[...]
    moe = wt[:, 0:1] * _load_token_tiles(y1_ref) + wt[:, 1:2] * _load_token_tiles(y2_ref)
    x = res_ref[...] + gate_ref[0] * moe
    if final:
        ms = jnp.mean(x * x, axis=-1, keepdims=True)
        x = x * lax.rsqrt(ms + EPS) * fg_ref[...]
    o_ref[...] = x


def _combine(ys, slot2, wt, res, gate, final_g, seq, tl):
    t, d = res.shape
    nblk = t // tl
    nb = gate.shape[0]
    final = final_g is not None
    fg = final_g.reshape(1, d) if final else jnp.ones((1, d), F32)
    return pl.pallas_call(
        functools.partial(_combine_kernel, tl=tl, final=final),
        grid=(nblk,),
        in_specs=[pl.BlockSpec((1, 1, 2 * tl), lambda i: (i, 0, 0), memory_space=pltpu.SMEM),
                  pl.BlockSpec((tl, LANES), lambda i: (i, 0)),
                  pl.BlockSpec((tl, d), lambda i: (i, 0)),
                  pl.BlockSpec((1, 1, d), (lambda i: (i * tl // seq, 0, 0)) if nb > 1 else (lambda i: (0, 0, 0))),
                  pl.BlockSpec((1, d), lambda i: (0, 0)),
                  pl.BlockSpec(memory_space=pl.ANY)],
        out_specs=pl.BlockSpec((tl, d), lambda i: (i, 0)),
        out_shape=jax.ShapeDtypeStruct((t, d), F32),
        scratch_shapes=[pltpu.VMEM((tl * SUBLANES, LANES), F32), pltpu.VMEM((tl * SUBLANES, LANES), F32),
                        pltpu.SemaphoreType.DMA(())],
        compiler_params=_cparams(("arbitrary",)),
        name="moe_combine",
    )(slot2, wt, res, gate, fg, ys)


def _even_layer_mixer(xs, mods, norm_g, w_in, conv_w, conv_b, dt_bias, a_log, d_skip, ssd_g,
                      cf_w, cf_b, cf_lng, cf_lnb, w_out, rows):
    s_lo = SSD_INNER
    dt_lo = SSD_INNER + SSD_CONV_DIM
    cf_lo = dt_lo + 2 * SSD_HEADS
    wb = w_in.astype(BF16)
    w_z, w_xbc, w_cf = wb[:, :s_lo], wb[:, s_lo:dt_lo], wb[:, cf_lo:]
    w_dt = [jnp.pad(wb[:, dt_lo + d * SSD_HEADS:dt_lo + (d + 1) * SSD_HEADS], ((0, 0), (0, LANES - SSD_HEADS)))
            for d in range(2)]
    w_out_b = w_out.astype(BF16)
    bsz = xs[0].shape[0]
    zero_state = jnp.zeros((bsz, SSD_GROUPS, SSD_STATE, SSD_GROUP_W), F32)
    proj = []
    for x, m in zip(xs, mods):
        h = _norm_mod(x, norm_g, m[0], m[1])
        proj.append(dict(z=_matmul(h, w_z, BF16), xbc=_matmul(h, w_xbc, BF16), cf=_matmul(h, w_cf, BF16),
                         dt=[_matmul(h, w_dt[d], F32) for d in range(2)]))
    o_ssd = [None, None]
    y_fwd = [None, None]
    state = zero_state
    for si in range(2):
        p = proj[si]
        y_fwd[si], state = _ssd_scan(p["xbc"], p["dt"][0], conv_w, conv_b, dt_bias[0], a_log[0], d_skip, state, False)
    state = zero_state
    for si in range(2):
        p = proj[si]
        o_ssd[si], state = _ssd_scan(p["xbc"], p["dt"][1], conv_w, conv_b, dt_bias[1], a_log[1], d_skip, state, True,
                                     y_in=y_fwd[si], z=p["z"], norm_g=ssd_g)
    outs = []
    for si, (x, m) in enumerate(zip(xs, mods)):
        slen = x.shape[1] if si == 0 else GRID_W
        o_cf = _conformer(proj[si]["cf"], cf_w, cf_b, cf_lng, cf_lnb, slen)
        cat = jnp.concatenate([o_ssd[si], o_cf], axis=-1)
        outs.append(_matmul_res(cat, w_out_b, x, m[2]))
    return outs


def _odd_layer_mixer(xc, xl, m_c, m_l, norm_g, w_in, lb, hg_g, w_out):
    st0 = HG_KEY + HG_VAL
    wb = w_in.astype(BF16)
    w_q, w_g, w_i, w_f = wb[:, :HG_KEY], wb[:, HG_KEY:st0], wb[:, st0:st0 + HG_VAL], wb[:, st0 + HG_VAL:]
    bsz = xl.shape[0]
    h_c = _norm_mod(xc, norm_g, m_c[0], m_c[1])
    h_l = _norm_mod_colmajor(xl, norm_g, m_l[0], m_l[1])
    v_c, f_c = _matmul(h_c, w_i, BF16), _matmul(h_c, w_f, F32)
    q_l, g_l = _matmul(h_l, w_q, BF16), _matmul(h_l, w_g, BF16)
    v_l, f_l = _matmul(h_l, w_i, BF16), _matmul(h_l, w_f, F32)
    zero_state = jnp.zeros((bsz, HG_HEADS, HG_DV, HG_DK), F32)
    s_f = _hgrn2_scan(v_c, f_c, lb, zero_state, False, "state")
    s_r = _hgrn2_scan(v_c, f_c, lb, zero_state, True, "state")
    o_f = _hgrn2_scan(v_l, f_l, lb, s_f, False, "out", q=q_l)
    o = _hgrn2_scan(v_l, f_l, lb, s_r, True, "readout", q=q_l, o_in=o_f, g=g_l, norm_g=hg_g)
    return _matmul_res_colmajor(o, w_out.astype(BF16), xl, m_l[2])


def _dispatch_layout(counts, slot, max_tiles):
    tiles = (counts + MOE_TM - 1) // MOE_TM
    tile_ends = jnp.cumsum(tiles)
    total = tile_ends[-1]
    starts = (tile_ends - tiles) * MOE_TM
    ends = jnp.concatenate([starts + counts, (total * MOE_TM).reshape(1)]).astype(jnp.int32)
    slots = jnp.concatenate([starts[slot[:, 0]] + slot[:, 2], starts[slot[:, 1]] + slot[:, 3]], axis=-1)
    n = jnp.arange(max_tiles, dtype=jnp.int32)
    nn = jnp.minimum(n, total - 1)
    expert = jnp.searchsorted(tile_ends, nn, side="right").astype(jnp.int32)
    is_first = ((nn == (tile_ends - tiles)[expert]) & (n < total)).astype(jnp.int32)
    return slots[:, None, :].astype(jnp.int32), ends, expert, is_first, total.reshape(1).astype(jnp.int32)


def _moe_block(x, m, norm_g, router_w, router_b, w_gate, w_up, w_down, final_g=None):
    bsz, seq, d = x.shape
    t = bsz * seq
    xf = x.reshape(t, d)
    tl = min(seq, MOE_TL)
    h, slot, wt, cnt = _ffn_norm_router(xf, norm_g, m[3], m[4], router_w, router_b, seq)
    max_tiles = 2 * t // MOE_TM + N_EXPERTS
    slot2, ends, tile_expert, tile_first, n_tiles = _dispatch_layout(cnt[:, 0], slot, max_tiles)
    hs = _dispatch(h, slot2, ends, tl, max_tiles)
    ys = _experts(hs, tile_expert, tile_first, n_tiles, w_gate, w_up, w_down, max_tiles)
    return _combine(ys, slot2, wt, xf, m[5], final_g, seq, tl).reshape(bsz, seq, d)


def kernel(x, c, ctx, c_ctx, mod_w, mod_b, norm_mix_g, norm_ffn_g, router_w, router_b, moe_w_gate, moe_w_up,
           moe_w_down, ab_w_in, ssd_conv_w, ssd_conv_b, ssd_dt_bias, ssd_a_log, ssd_d, ssd_norm_g, cf_dw_w,
           cf_dw_b, cf_ln_g, cf_ln_b, ab_w_out, hg_w_in, hg_lb, hg_norm_g, hg_w_out, final_norm_g):
    depth = mod_w.shape[0]
    assert depth == 2, "layer schedule below is written for one even and one odd layer"
    bsz, seq, d = x.shape
    rows = seq // GRID_W
    lb_all = jnp.cumsum(jax.nn.softmax(hg_lb.astype(F32), axis=0), axis=0)
    lb_all = lb_all - lb_all[0]

    nrow = -(-(bsz + 1) // SUBLANES) * SUBLANES
    cond = jnp.zeros((nrow, d), F32).at[:bsz].set(c).at[bsz].set(c_ctx)
    mod = _modulation(cond, mod_w, mod_b).reshape(depth, nrow, N_MOD, 1, d)

    def mods(l):
        m_l = [mod[l, :bsz, k] for k in range(N_MOD)]
        m_c = [mod[l, bsz:bsz + 1, k] for k in range(N_MOD)]
        return m_c, m_l

    m_c, m_l = mods(0)
    xc, xl = _even_layer_mixer((ctx, x), (m_c, m_l), norm_mix_g[0], ab_w_in[0], ssd_conv_w[0], ssd_conv_b[0],
                               ssd_dt_bias[0], ssd_a_log[0], ssd_d[0], ssd_norm_g[0], cf_dw_w[0], cf_dw_b[0],
                               cf_ln_g[0], cf_ln_b[0], ab_w_out[0], rows)
    xl = _moe_block(xl, m_l, norm_ffn_g[0], router_w, router_b, moe_w_gate[0], moe_w_up[0], moe_w_down[0])
    xc = _moe_block(xc, m_c, norm_ffn_g[0], router_w, router_b, moe_w_gate[0], moe_w_up[0], moe_w_down[0])

    m_c, m_l = mods(1)
    xl = _odd_layer_mixer(xc, xl, m_c, m_l, norm_mix_g[1], hg_w_in[0], lb_all[1], hg_norm_g[0], hg_w_out[0])
    return _moe_block(xl, m_l, norm_ffn_g[1], router_w, router_b, moe_w_gate[1], moe_w_up[1], moe_w_down[1],
                      final_g=final_norm_g)
```

```python
import functools

import jax
import jax.numpy as jnp
from jax import lax
from jax.experimental import pallas as pl
from jax.experimental.pallas import tpu as pltpu

F32 = jnp.float32
BF16 = jnp.bfloat16

D_MODEL = 1024
GRID_W = 64
EPS = 1e-6
N_MOD = 6

SSD_HEADS = 16
SSD_HEAD_DIM = 64
SSD_INNER = SSD_HEADS * SSD_HEAD_DIM
SSD_GROUPS = 4
SSD_STATE = 128
SSD_CONV = 5
SSD_CHUNK = 128
SSD_BC = SSD_GROUPS * SSD_STATE
SSD_CONV_DIM = SSD_INNER + 2 * SSD_BC
SSD_GROUP_W = SSD_INNER // SSD_GROUPS

CF_CH = 1024
CF_KERNEL = 31
CF_PAD = 16

HG_HEADS = 8
HG_DK = 128
HG_DV = 128
HG_KEY = HG_HEADS * HG_DK
HG_VAL = HG_HEADS * HG_DV
HG_CHUNK = 64

N_EXPERTS = 16
N_EXPERT_GROUPS = 4
EXPERTS_PER_GROUP = 4
D_EXPERT = 512
MOE_TL = 512
MOE_TM = 256

LANES = 128
SUBLANES = 8
HALO = 16
VMEM_LIMIT = 48 * 1024 * 1024


def _cparams(sem):
    return pltpu.CompilerParams(dimension_semantics=sem, vmem_limit_bytes=VMEM_LIMIT)


def _silu(x):
    return x * jax.nn.sigmoid(x)


def _split3(v):
    hi = v.astype(BF16)
    r1 = v - hi.astype(F32)
    mid = r1.astype(BF16)
    lo = (r1 - mid.astype(F32)).astype(BF16)
    return hi, mid, lo


def _dot01(m01, v):
    hi, mid, lo = _split3(v)
    out = jnp.dot(m01, lo, preferred_element_type=F32)
    out = out + jnp.dot(m01, mid, preferred_element_type=F32)
    return out + jnp.dot(m01, hi, preferred_element_type=F32)


def _dot_v01(v, m01):
    hi, mid, lo = _split3(v)
    out = jnp.dot(lo, m01, preferred_element_type=F32)
    out = out + jnp.dot(mid, m01, preferred_element_type=F32)
    return out + jnp.dot(hi, m01, preferred_element_type=F32)


_NN = (((1,), (0,)), ((), ()))
_NT = (((1,), (1,)), ((), ()))
_TN = (((0,), (0,)), ((), ()))


def _dot_f32(a, b, dn):
    a1, a2, a3 = _split3(a)
    b1, b2, b3 = _split3(b)
    out = lax.dot_general(a3, b1, dn, preferred_element_type=F32)
    out = out + lax.dot_general(a1, b3, dn, preferred_element_type=F32)
    out = out + lax.dot_general(a2, b2, dn, preferred_element_type=F32)
    out = out + lax.dot_general(a2, b1, dn, preferred_element_type=F32)
    out = out + lax.dot_general(a1, b2, dn, preferred_element_type=F32)
    return out + lax.dot_general(a1, b1, dn, preferred_element_type=F32)


def _scan_tri(n, rev):
    r = lax.broadcasted_iota(jnp.int32, (n, n), 0)
    c = lax.broadcasted_iota(jnp.int32, (n, n), 1)
    return ((c >= r) if rev else (c <= r))


def _mod_kernel(c_ref, w_ref, b_ref, o_ref):
    cond = _silu(c_ref[...])
    o_ref[0] = _dot_f32(cond, w_ref[0], _NN) + b_ref[0]


def _modulation(cond, mod_w, mod_b):
    depth, d, n = mod_w.shape
    r = cond.shape[0]
    tn = 512
    return pl.pallas_call(
        _mod_kernel,
        grid=(depth, n // tn),
        in_specs=[pl.BlockSpec((r, d), lambda l, j: (0, 0)),
                  pl.BlockSpec((1, d, tn), lambda l, j: (l, 0, j)),
                  pl.BlockSpec((1, 1, tn), lambda l, j: (l, 0, j))],
        out_specs=pl.BlockSpec((1, r, tn), lambda l, j: (l, 0, j)),
        out_shape=jax.ShapeDtypeStruct((depth, r, n), F32),
        compiler_params=_cparams(("parallel", "parallel")),
        name="modulation",
    )(cond, mod_w, mod_b.reshape(depth, 1, n))


def _normmod(x, g, shift, scale):
    ms = jnp.mean(x * x, axis=-1, keepdims=True)
    return (x * lax.rsqrt(ms + EPS) * g) * (1.0 + scale) + shift


def _normmod_kernel(x_ref, g_ref, sh_ref, sc_ref, o_ref):
    o_ref[0] = _normmod(x_ref[0], g_ref[...], sh_ref[0], sc_ref[0]).astype(o_ref.dtype)


def _mod_spec(nb):
    return pl.BlockSpec((1, 1, D_MODEL), (lambda b, *_: (b, 0, 0)) if nb > 1 else (lambda b, *_: (0, 0, 0)))


def _norm_mod(x, g, shift, scale):
    bsz, seq, d = x.shape
    tl = min(seq, 512)
    return pl.pallas_call(
        _normmod_kernel,
        grid=(bsz, seq // tl),
        in_specs=[pl.BlockSpec((1, tl, d), lambda b, i: (b, i, 0)),
                  pl.BlockSpec((1, d), lambda b, i: (0, 0)),
                  _mod_spec(shift.shape[0]), _mod_spec(scale.shape[0])],
        out_specs=pl.BlockSpec((1, tl, d), lambda b, i: (b, i, 0)),
        out_shape=jax.ShapeDtypeStruct((bsz, seq, d), BF16),
        compiler_params=_cparams(("parallel", "parallel")),
        name="norm_mod",
    )(x, g.reshape(1, d), shift, scale)


def _normmod_cm_kernel(x_ref, g_ref, sh_ref, sc_ref, o_ref, *, rows):
    for c in range(SUBLANES):
        piece = x_ref[0, :, c, :]
        o_ref[0, c * rows:(c + 1) * rows, :] = _normmod(piece, g_ref[...], sh_ref[0], sc_ref[0]).astype(o_ref.dtype)


def _norm_mod_colmajor(x, g, shift, scale):
    bsz, seq, d = x.shape
    rows = seq // GRID_W
    x4 = x.reshape(bsz, rows, GRID_W, d)
    return pl.pallas_call(
        functools.partial(_normmod_cm_kernel, rows=rows),
        grid=(bsz, GRID_W // SUBLANES),
        in_specs=[pl.BlockSpec((1, rows, SUBLANES, d), lambda b, i: (b, 0, i, 0)),
                  pl.BlockSpec((1, d), lambda b, i: (0, 0)),
                  _mod_spec(shift.shape[0]), _mod_spec(scale.shape[0])],
        out_specs=pl.BlockSpec((1, SUBLANES * rows, d), lambda b, i: (b, i, 0)),
        out_shape=jax.ShapeDtypeStruct((bsz, seq, d), BF16),
        compiler_params=_cparams(("parallel", "parallel")),
        name="norm_mod_colmajor",
    )(x4, g.reshape(1, d), shift, scale)


def _mm_kernel(a_ref, b_ref, o_ref):
    o_ref[0] = jnp.dot(a_ref[0], b_ref[...].astype(BF16), preferred_element_type=F32).astype(o_ref.dtype)


def _mm_res2_kernel(a1_ref, a2_ref, b1_ref, b2_ref, res_ref, gate_ref, o_ref):
    y = jnp.dot(a1_ref[0], b1_ref[...].astype(BF16), preferred_element_type=F32)
    y = y + jnp.dot(a2_ref[0], b2_ref[...].astype(BF16), preferred_element_type=F32)
    o_ref[0] = res_ref[0] + gate_ref[0] * y


def _mm_res_cm_kernel(a_ref, b_ref, res_ref, gate_ref, o_ref, *, rows):
    y = jnp.dot(a_ref[0], b_ref[...].astype(BF16), preferred_element_type=F32)
    for c in range(SUBLANES):
        o_ref[0, :, c, :] = res_ref[0, :, c, :] + gate_ref[0] * y[c * rows:(c + 1) * rows, :]


def _matmul(a, w, out_dtype, col0=0, ncols=None):
    bsz, seq, k = a.shape
    n = w.shape[1] - col0 if ncols is None else ncols
    tm, tn = min(seq, 1024), min(n, 512)
    assert col0 % tn == 0 and n % tn == 0
    jb = col0 // tn
    return pl.pallas_call(
        _mm_kernel,
        grid=(bsz, seq // tm, n // tn),
        in_specs=[pl.BlockSpec((1, tm, k), lambda b, i, j: (b, i, 0)),
                  pl.BlockSpec((k, tn), lambda b, i, j: (0, j + jb))],
        out_specs=pl.BlockSpec((1, tm, tn), lambda b, i, j: (b, i, j)),
        out_shape=jax.ShapeDtypeStruct((bsz, seq, n), out_dtype),
        compiler_params=_cparams(("parallel", "parallel", "parallel")),
        name="matmul",
    )(a, w)


def _matmul_res2(a1, a2, w, res, gate):
    bsz, seq, kh = a1.shape
    n = w.shape[1]
    tm, tn = min(seq, 1024), min(n, 512)
    nb = gate.shape[0]
    return pl.pallas_call(
        _mm_res2_kernel,
        grid=(bsz, seq // tm, n // tn),
        in_specs=[pl.BlockSpec((1, tm, kh), lambda b, i, j: (b, i, 0)),
                  pl.BlockSpec((1, tm, kh), lambda b, i, j: (b, i, 0)),
                  pl.BlockSpec((kh, tn), lambda b, i, j: (0, j)),
                  pl.BlockSpec((kh, tn), lambda b, i, j: (1, j)),
                  pl.BlockSpec((1, tm, tn), lambda b, i, j: (b, i, j)),
                  pl.BlockSpec((1, 1, tn), (lambda b, i, j: (b, 0, j)) if nb > 1 else (lambda b, i, j: (0, 0, j)))],
        out_specs=pl.BlockSpec((1, tm, tn), lambda b, i, j: (b, i, j)),
        out_shape=jax.ShapeDtypeStruct((bsz, seq, n), F32),
        compiler_params=_cparams(("parallel", "parallel", "parallel")),
        name="matmul_res",
    )(a1, a2, w, w, res, gate)


def _realign_kernel(w_ref, o_ref, *, col0):
    o_ref[...] = w_ref[:, col0:col0 + o_ref.shape[1]].astype(o_ref.dtype)


def _realign_cols(w, col0, ncols):
    k, n = w.shape
    tk = 128
    return pl.pallas_call(
        functools.partial(_realign_kernel, col0=col0),
        grid=(k // tk,),
        in_specs=[pl.BlockSpec((tk, n), lambda i: (i, 0))],
        out_specs=pl.BlockSpec((tk, ncols), lambda i: (i, 0)),
        out_shape=jax.ShapeDtypeStruct((k, ncols), BF16),
        compiler_params=_cparams(("parallel",)),
        name="realign_cols",
    )(w)


def _matmul_res_colmajor(a, w, res, gate):
    bsz, seq, k = a.shape
    n = w.shape[1]
    rows = seq // GRID_W
    tm, tn = SUBLANES * rows, min(n, 512)
    res4 = res.reshape(bsz, rows, GRID_W, n)
    out = pl.pallas_call(
        functools.partial(_mm_res_cm_kernel, rows=rows),
        grid=(bsz, seq // tm, n // tn),
        in_specs=[pl.BlockSpec((1, tm, k), lambda b, i, j: (b, i, 0)),
                  pl.BlockSpec((k, tn), lambda b, i, j: (0, j)),
                  pl.BlockSpec((1, rows, SUBLANES, tn), lambda b, i, j: (b, 0, i, j)),
                  pl.BlockSpec((1, 1, tn), lambda b, i, j: (b, 0, j))],
        out_specs=pl.BlockSpec((1, rows, SUBLANES, tn), lambda b, i, j: (b, 0, i, j)),
        out_shape=jax.ShapeDtypeStruct((bsz, rows, GRID_W, n), F32),
        compiler_params=_cparams(("parallel", "parallel", "parallel")),
        name="matmul_res_colmajor",
    )(a, w, res4, gate)
    return out.reshape(bsz, seq, n)


def _ssd_kernel(*refs, rev, nchunks, readout):
    if readout:
        act_ref, dt_ref, dtb_ref, alog_ref, s0_ref, yin_ref, z_ref, ng_ref, y_ref, sfin_ref, st_ref = refs
    else:
        (xbc_ref, prev_ref, next_ref, dt_ref, cw_ref, cb_ref, dtb_ref, alog_ref, dsk_ref, s0_ref,
         y_ref, act_ref, sfin_ref, pad_ref, st_ref) = refs
    step = pl.program_id(1)
    u = (nchunks - 1 - step) if rev else step
    ck = SSD_CHUNK
    hoff = SSD_HEADS if rev else 0

    @pl.when(step == 0)
    def _():
        st_ref[...] = s0_ref[0]

    if readout:
        xbc = act_ref[0].astype(F32)
    else:
        pad_ref[HALO:HALO + ck, :] = xbc_ref[0].astype(F32)
        pad_ref[0:HALO, :] = jnp.where(u > 0, prev_ref[0].astype(F32), 0.0)
        pad_ref[HALO + ck:2 * HALO + ck, :] = jnp.where(u < nchunks - 1, next_ref[0].astype(F32), 0.0)
        half = SSD_CONV // 2
        acc = cb_ref[...] + cw_ref[0:1, :] * pad_ref[pl.ds(HALO - half, ck), :]
        for k in range(1, SSD_CONV):
            acc = acc + cw_ref[k:k + 1, :] * pad_ref[pl.ds(HALO - half + k, ck), :]
        xbc = _silu(acc)
        act_ref[0] = xbc.astype(act_ref.dtype)
    xs = xbc[:, :SSD_INNER]

    hlane = lax.broadcasted_iota(jnp.int32, (ck, LANES), 1)
    is_head = (hlane >= hoff) & (hlane < hoff + SSD_HEADS)
    dt = jnp.where(is_head, jax.nn.softplus(dt_ref[0] + dtb_ref[...]), 0.0)
    la = dt * (-jnp.exp(alog_ref[...]))
    tri = _scan_tri(ck, rev).astype(BF16)
    acum = _dot01(tri, la)
    last = 0 if rev else ck - 1
    total = acum[last:last + 1, :]
    acum_t = acum.T

    hrow = lax.broadcasted_iota(jnp.int32, (LANES, SSD_INNER), 0)
    hcol = lax.broadcasted_iota(jnp.int32, (LANES, SSD_INNER), 1)
    expand = (hcol // SSD_HEAD_DIM + hoff == hrow).astype(BF16)
    dt_x = _dot_v01(dt, expand)
    ea_x = _dot_v01(jnp.exp(acum), expand)
    te_x = _dot_v01(jnp.exp(total - acum), expand)
    cd_x = _dot_v01(jnp.broadcast_to(jnp.exp(total), (SUBLANES, LANES)), expand)[0:1, :]

    xdt = xs * dt_x
    xdt_b = xdt.astype(BF16)
    xw_b = (xdt * te_x).astype(BF16)

    r = lax.broadcasted_iota(jnp.int32, (ck, ck), 0)
    c = lax.broadcasted_iota(jnp.int32, (ck, ck), 1)
    causal = (c >= r) if rev else (c <= r)
    lane = lax.broadcasted_iota(jnp.int32, (ck, LANES), 1)
    nt = (((1,), (1,)), ((), ()))
    tn = (((0,), (0,)), ((), ()))

    y_parts = []
    for g in range(SSD_GROUPS):
        b_g = xbc[:, SSD_INNER + g * SSD_STATE:SSD_INNER + (g + 1) * SSD_STATE].astype(BF16)
        c_g = xbc[:, SSD_INNER + SSD_BC + g * SSD_STATE:SSD_INNER + SSD_BC + (g + 1) * SSD_STATE].astype(BF16)
        cb = lax.dot_general(c_g, b_g, nt, preferred_element_type=F32)
        lo = g * SSD_GROUP_W
        s_in = st_ref[g]
        y_off = jnp.dot(c_g, s_in.astype(BF16), preferred_element_type=F32) * ea_x[:, lo:lo + SSD_GROUP_W]
        chunk_state = lax.dot_general(b_g, xw_b[:, lo:lo + SSD_GROUP_W], tn, preferred_element_type=F32)
        st_ref[g] = cd_x[:, lo:lo + SSD_GROUP_W] * s_in + chunk_state
        heads_per_group = SSD_HEADS // SSD_GROUPS
        for pair in range(heads_per_group // 2):
            plo = lo + pair * LANES
            xpair = xdt_b[:, plo:plo + LANES]
            y_pair = y_off[:, pair * LANES:(pair + 1) * LANES]
            for half_i in range(2):
                h = hoff + g * heads_per_group + pair * 2 + half_i
                seg = acum[:, h:h + 1] - acum_t[h:h + 1, :]
                m = (cb * jnp.where(causal, jnp.exp(seg), 0.0)).astype(BF16)
                in_half = (lane >= half_i * SSD_HEAD_DIM) & (lane < (half_i + 1) * SSD_HEAD_DIM)
                y_pair = y_pair + jnp.dot(m, jnp.where(in_half, xpair, jnp.zeros_like(xpair)),
                                          preferred_element_type=F32)
            y_parts.append(y_pair)
    y = jnp.concatenate(y_parts, axis=1)

    if readout:
        y = y + yin_ref[0].astype(F32)
        y = y * _silu(z_ref[0].astype(F32))
        ms = jnp.mean(y * y, axis=-1, keepdims=True)
        y_ref[0] = (y * lax.rsqrt(ms + EPS) * ng_ref[...]).astype(y_ref.dtype)
    else:
        y_ref[0] = (y + dsk_ref[...] * xs).astype(y_ref.dtype)

    @pl.when(step == nchunks - 1)
    def _():
        sfin_ref[0] = st_ref[...]


def _ssd_scan(xbc, dt_raw, dt_bias, a_log, s0, rev, conv_w=None, conv_b=None, d_skip=None,
              y_in=None, z=None, norm_g=None):
    bsz, seq, _ = xbc.shape
    nchunks = seq // SSD_CHUNK
    hb = SSD_CHUNK // HALO
    nhalo = seq // HALO
    hoff = SSD_HEADS if rev else 0

    def cidx(i):
        return (nchunks - 1 - i) if rev else i

    def head_lanes(v):
        return jnp.pad(v.reshape(1, -1), ((0, 0), (hoff, LANES - hoff - v.shape[-1])))

    chunk = lambda width: pl.BlockSpec((1, SSD_CHUNK, width), lambda b, i: (b, cidx(i), 0))
    row = lambda width: pl.BlockSpec((1, width), lambda b, i: (0, 0))
    state_spec = pl.BlockSpec((1, SSD_GROUPS, SSD_STATE, SSD_GROUP_W), lambda b, i: (b, 0, 0, 0))
    state_shape = jax.ShapeDtypeStruct((bsz, SSD_GROUPS, SSD_STATE, SSD_GROUP_W), F32)
    y_shape = jax.ShapeDtypeStruct((bsz, seq, SSD_INNER), BF16)
    state_scratch = pltpu.VMEM((SSD_GROUPS, SSD_STATE, SSD_GROUP_W), F32)
    if rev:
        in_specs = [chunk(SSD_CONV_DIM), chunk(LANES), row(LANES), row(LANES), state_spec,
                    chunk(SSD_INNER), chunk(SSD_INNER), row(SSD_INNER)]
        args = [xbc, dt_raw, head_lanes(dt_bias), head_lanes(a_log), s0, y_in, z, norm_g.reshape(1, -1)]
        out_specs, out_shape = [chunk(SSD_INNER), state_spec], [y_shape, state_shape]
        scratch = [state_scratch]
    else:
        in_specs = [
            chunk(SSD_CONV_DIM),
            pl.BlockSpec((1, HALO, SSD_CONV_DIM), lambda b, i: (b, jnp.maximum(i * hb - 1, 0), 0)),
            pl.BlockSpec((1, HALO, SSD_CONV_DIM), lambda b, i: (b, jnp.minimum((i + 1) * hb, nhalo - 1), 0)),
            chunk(LANES),
            pl.BlockSpec((SUBLANES, SSD_CONV_DIM), lambda b, i: (0, 0)), row(SSD_CONV_DIM),
            row(LANES), row(LANES), row(SSD_INNER), state_spec]
        args = [xbc, xbc, xbc, dt_raw,
                jnp.pad(conv_w, ((0, SUBLANES - SSD_CONV), (0, 0))), conv_b.reshape(1, -1),
                head_lanes(dt_bias), head_lanes(a_log), jnp.repeat(d_skip, SSD_HEAD_DIM).reshape(1, -1), s0]
        out_specs = [chunk(SSD_INNER), chunk(SSD_CONV_DIM), state_spec]
        out_shape = [y_shape, jax.ShapeDtypeStruct((bsz, seq, SSD_CONV_DIM), BF16), state_shape]
        scratch = [pltpu.VMEM((SSD_CHUNK + 2 * HALO, SSD_CONV_DIM), F32), state_scratch]
    return pl.pallas_call(
        functools.partial(_ssd_kernel, rev=rev, nchunks=nchunks, readout=rev),
        grid=(bsz, nchunks),
        in_specs=in_specs, out_specs=out_specs, out_shape=out_shape, scratch_shapes=scratch,
        compiler_params=_cparams(("parallel", "arbitrary")),
        name="ssd_scan_rev" if rev else "ssd_scan_fwd",
    )(*args)


def _conformer_kernel(p_ref, w_ref, b_ref, g_ref, beta_ref, o_ref, pad_ref, rot_ref, conv_ref, *, nseq, slen):
    v = p_ref[0, :, :CF_CH].astype(F32)
    gate = p_ref[0, :, CF_CH:].astype(F32)
    u = v * jax.nn.sigmoid(gate)
    first = CF_PAD - CF_KERNEL // 2
    span = rot_ref.shape[2]
    for s in range(nseq):
        pad_ref[s, 0:CF_PAD, :] = jnp.zeros((CF_PAD, CF_CH), F32)
        pad_ref[s, CF_PAD + slen:2 * CF_PAD + slen, :] = jnp.zeros((CF_PAD, CF_CH), F32)
        pad_ref[s, CF_PAD:CF_PAD + slen, :] = u[s * slen:(s + 1) * slen, :]
        for res in range(1, SUBLANES):
            rot_ref[res - 1, s] = pad_ref[s, pl.ds(res, span), :]
    for s in range(nseq):
        for cb in range(CF_CH // LANES):
            ch = slice(cb * LANES, (cb + 1) * LANES)
            acc = jnp.broadcast_to(b_ref[:, ch], (slen, LANES))
            for k in range(CF_KERNEL):
                res, lead = (first + k) % SUBLANES, (first + k) // SUBLANES * SUBLANES
                src = pad_ref[s, pl.ds(lead, slen), ch] if res == 0 else rot_ref[res - 1, s, pl.ds(lead, slen), ch]
                acc = acc + w_ref[k:k + 1, ch] * src
            conv_ref[s * slen:(s + 1) * slen, ch] = acc
    y = conv_ref[...]
    mu = jnp.mean(y, axis=-1, keepdims=True)
    xc = y - mu
    var = jnp.mean(xc * xc, axis=-1, keepdims=True)
    y = xc * lax.rsqrt(var + EPS) * g_ref[...] + beta_ref[...]
    o_ref[0] = _silu(y).astype(o_ref.dtype)


def _conformer(p_cf, dw_w, dw_b, ln_g, ln_b, slen):
    bsz, seq, _ = p_cf.shape
    nseq = max(1, min(seq, 256) // slen)
    tb = nseq * slen
    kpad = -(-CF_KERNEL // SUBLANES) * SUBLANES
    last_tap_row = CF_PAD - CF_KERNEL // 2 + CF_KERNEL - 1
    span = slen + last_tap_row // SUBLANES * SUBLANES
    return pl.pallas_call(
        functools.partial(_conformer_kernel, nseq=nseq, slen=slen),
        grid=(bsz, seq // tb),
        in_specs=[pl.BlockSpec((1, tb, 2 * CF_CH), lambda b, i: (b, i, 0)),
                  pl.BlockSpec((kpad, CF_CH), lambda b, i: (0, 0)),
                  pl.BlockSpec((1, CF_CH), lambda b, i: (0, 0)),
                  pl.BlockSpec((1, CF_CH), lambda b, i: (0, 0)),
                  pl.BlockSpec((1, CF_CH), lambda b, i: (0, 0))],
        out_specs=pl.BlockSpec((1, tb, CF_CH), lambda b, i: (b, i, 0)),
        out_shape=jax.ShapeDtypeStruct((bsz, seq, CF_CH), BF16),
        scratch_shapes=[pltpu.VMEM((nseq, slen + 2 * CF_PAD, CF_CH), F32),
                        pltpu.VMEM((SUBLANES - 1, nseq, span, CF_CH), F32),
                        pltpu.VMEM((tb, CF_CH), F32)],
        compiler_params=_cparams(("parallel", "parallel")),
        name="conformer_conv",
    )(p_cf, jnp.pad(dw_w, ((0, kpad - CF_KERNEL), (0, 0))), dw_b.reshape(1, -1), ln_g.reshape(1, -1),
      ln_b.reshape(1, -1))


def _hgrn2_kernel(*refs, rev, nchunks, mode):
    if mode == "state":
        v_ref, f_ref, lb_ref, s0_ref, sfin_ref, st_ref = refs
    elif mode == "out":
        q_ref, v_ref, f_ref, lb_ref, s0_ref, o_ref, st_ref = refs
    else:
        q_ref, v_ref, f_ref, lb_ref, s0_ref, oin_ref, g_ref, ng_ref, o_ref, st_ref = refs
    step = pl.program_id(1)
    ck = HG_CHUNK

    @pl.when(step == 0)
    def _():
        st_ref[...] = s0_ref[0]

    lb = lb_ref[...]
    f = lb + (1.0 - lb) * jax.nn.sigmoid(f_ref[0])
    kk = 1.0 - f
    gcum = _dot01(_scan_tri(ck, rev).astype(BF16), jnp.log(f))
    last = 0 if rev else ck - 1
    g_end = gcum[last:last + 1, :]
    k_end = (kk * jnp.exp(g_end - gcum)).astype(BF16)
    dec_end = jnp.exp(g_end)
    v = v_ref[0]
    nt = (((1,), (1,)), ((), ()))
    tn = (((0,), (0,)), ((), ()))

    if mode != "state":
        mid_pos = HG_CHUNK // 2 - 1
        mid = (ck - 1 - mid_pos) if rev else mid_pos
        g_mid = gcum[mid:mid + 1, :]
        q = _silu(q_ref[0].astype(F32))
        q_rel = (q * jnp.exp(gcum - g_mid)).astype(BF16)
        k_rel = (kk * jnp.exp(g_mid - gcum)).astype(BF16)
        q_dec = (q * jnp.exp(gcum)).astype(BF16)
        r = lax.broadcasted_iota(jnp.int32, (ck, ck), 0)
        c = lax.broadcasted_iota(jnp.int32, (ck, ck), 1)
        causal = (c >= r) if rev else (c <= r)

    outs = []
    for h in range(HG_HEADS):
        ks = slice(h * HG_DK, (h + 1) * HG_DK)
        vs = slice(h * HG_DV, (h + 1) * HG_DV)
        s_in = st_ref[h]
        if mode != "state":
            att = lax.dot_general(q_rel[:, ks], k_rel[:, ks], nt, preferred_element_type=F32)
            att = jnp.where(causal, att, 0.0).astype(BF16)
            o_h = jnp.dot(att, v[:, vs], preferred_element_type=F32)
            o_h = o_h + lax.dot_general(q_dec[:, ks], s_in.astype(BF16), nt, preferred_element_type=F32)
            outs.append(o_h)
        chunk_state = lax.dot_general(v[:, vs], k_end[:, ks], tn, preferred_element_type=F32)
        st_ref[h] = s_in * dec_end[:, ks] + chunk_state

    if mode == "out":
        o_ref[0] = jnp.concatenate(outs, axis=1).astype(o_ref.dtype)
    elif mode == "readout":
        gate = _silu(g_ref[0].astype(F32))
        for h in range(HG_HEADS):
            vs = slice(h * HG_DV, (h + 1) * HG_DV)
            o_h = outs[h] + oin_ref[0, :, vs].astype(F32)
            ms = jnp.mean(o_h * o_h, axis=-1, keepdims=True)
            o_ref[0, :, vs] = ((o_h * lax.rsqrt(ms + EPS) * ng_ref[...]) * gate[:, vs]).astype(o_ref.dtype)
    else:
        @pl.when(step == nchunks - 1)
        def _():
            sfin_ref[0] = st_ref[...]


def _hgrn2_scan(v, f_raw, lb, s0, rev, mode, q=None, o_in=None, g=None, norm_g=None):
    bsz, seq, _ = v.shape
    nchunks = seq // HG_CHUNK
    dcol = 1 if rev else 0

    def cidx(i):
        return (nchunks - 1 - i) if rev else i

    tok = lambda width: pl.BlockSpec((1, HG_CHUNK, width), lambda b, i: (b, cidx(i), 0))
    state_spec = pl.BlockSpec((1, HG_HEADS, HG_DV, HG_DK), lambda b, i: (b, 0, 0, 0))
    f_spec = pl.BlockSpec((1, HG_CHUNK, HG_KEY), lambda b, i: (b, cidx(i), dcol))
    lb_spec = pl.BlockSpec((1, HG_KEY), lambda b, i: (0, 0))
    state_shape = jax.ShapeDtypeStruct((bsz, HG_HEADS, HG_DV, HG_DK), F32)
    if mode == "state":
        in_specs, args = [tok(HG_VAL), f_spec, lb_spec, state_spec], [v, f_raw, lb.reshape(1, -1), s0]
        out_specs, out_shape = state_spec, state_shape
    else:
        in_specs = [tok(HG_KEY), tok(HG_VAL), f_spec, lb_spec, state_spec]
        args = [q, v, f_raw, lb.reshape(1, -1), s0]
        if mode == "readout":
            in_specs += [tok(HG_VAL), tok(HG_VAL), pl.BlockSpec((1, HG_DV), lambda b, i: (0, 0))]
            args += [o_in, g, norm_g.reshape(1, -1)]
        out_specs, out_shape = tok(HG_VAL), jax.ShapeDtypeStruct((bsz, seq, HG_VAL), BF16)
    return pl.pallas_call(
        functools.partial(_hgrn2_kernel, rev=rev, nchunks=nchunks, mode=mode),
        grid=(bsz, nchunks),
        in_specs=in_specs, out_specs=out_specs, out_shape=out_shape,
        scratch_shapes=[pltpu.VMEM((HG_HEADS, HG_DV, HG_DK), F32)],
        compiler_params=_cparams(("parallel", "arbitrary")),
        name=f"hgrn2_{mode}_{'rev' if rev else 'fwd'}",
    )(*args)


def _first_argmax(vals):
    best, idx = vals[0], jnp.zeros(vals[0].shape, jnp.int32)
    for j in range(1, len(vals)):
        better = vals[j] > best
        idx = jnp.where(better, j, idx)
        best = jnp.where(better, vals[j], best)
    return idx, best


def _pick(idx, vals):
    out = vals[-1]
    for j in range(len(vals) - 2, -1, -1):
        out = jnp.where(idx == j, vals[j], out)
    return out


def _store_token_tiles(ref, v):
    n = v.shape[0]
    for j in range(SUBLANES):
        ref[pl.ds(j, n, stride=SUBLANES), :] = v[:, j * LANES:(j + 1) * LANES]


def _load_token_tiles(ref):
    n = ref.shape[0] // SUBLANES
    return jnp.concatenate([ref[pl.ds(j, n, stride=SUBLANES), :] for j in range(SUBLANES)], axis=1)


def _token_tile(ref, idx):
    return ref.at[pl.ds(pl.multiple_of(idx * SUBLANES, SUBLANES), SUBLANES)]


def _router_kernel(x_ref, g_ref, sh_ref, sc_ref, rw_ref, rb_ref, h_ref, slot_ref, wt_ref, cnt_ref, carry_ref):
    step = pl.program_id(0)

    @pl.when(step == 0)
    def _():
        carry_ref[...] = jnp.zeros_like(carry_ref)

    h = _normmod(x_ref[...], g_ref[...], sh_ref[0], sc_ref[0])
    _store_token_tiles(h_ref, h)
    scores = jax.nn.sigmoid(_dot_f32(rw_ref[...], h, _NT))
    sel = scores + rb_ref[...]
    srow = [sel[e:e + 1, :] for e in range(N_EXPERTS)]
    prow = [scores[e:e + 1, :] for e in range(N_EXPERTS)]
    gscore = []
    for gi in range(N_EXPERT_GROUPS):
        m = srow[gi * EXPERTS_PER_GROUP:(gi + 1) * EXPERTS_PER_GROUP]
        pair_sums = [m[i] + m[j] for i in range(EXPERTS_PER_GROUP) for j in range(i + 1, EXPERTS_PER_GROUP)]
        best = pair_sums[0]
        for p in pair_sums[1:]:
            best = jnp.maximum(best, p)
        gscore.append(best)
    gidx, _ = _first_argmax(gscore)
    in_sel = [_pick(gidx, [srow[gi * EXPERTS_PER_GROUP + j] for gi in range(N_EXPERT_GROUPS)])
              for j in range(EXPERTS_PER_GROUP)]
    in_p = [_pick(gidx, [prow[gi * EXPERTS_PER_GROUP + j] for gi in range(N_EXPERT_GROUPS)])
            for j in range(EXPERTS_PER_GROUP)]
    i1, _ = _first_argmax(in_sel)
    i2, _ = _first_argmax([jnp.where(i1 == j, -jnp.inf, in_sel[j]) for j in range(EXPERTS_PER_GROUP)])
    w1, w2 = _pick(i1, in_p), _pick(i2, in_p)
    den = w1 + w2
    e1, e2 = gidx * EXPERTS_PER_GROUP + i1, gidx * EXPERTS_PER_GROUP + i2
    tl = h.shape[0]
    erow = lax.broadcasted_iota(jnp.int32, (N_EXPERTS, tl), 0)
    oh1, oh2 = erow == e1, erow == e2
    cnt = (oh1 | oh2).astype(F32)
    r = lax.broadcasted_iota(jnp.int32, (tl, tl), 0)
    c = lax.broadcasted_iota(jnp.int32, (tl, tl), 1)
    before = jnp.dot(cnt.astype(BF16), (r < c).astype(BF16), preferred_element_type=F32)
    base = carry_ref[:, 0:1] + before
    rank1 = jnp.sum(jnp.where(oh1, base, 0.0), axis=0, keepdims=True).astype(jnp.int32)
    rank2 = jnp.sum(jnp.where(oh2, base, 0.0), axis=0, keepdims=True).astype(jnp.int32)
    carry_ref[...] = carry_ref[...] + jnp.sum(cnt, axis=1, keepdims=True)
    cnt_ref[...] = carry_ref[...].astype(jnp.int32)
    srow = lax.broadcasted_iota(jnp.int32, (SUBLANES, tl), 0)
    slot_ref[0] = jnp.where(srow == 0, e1, jnp.where(srow == 1, e2, jnp.where(srow == 2, rank1, rank2)))
    wrow = lax.broadcasted_iota(jnp.int32, (LANES, tl), 0)
    wt_ref[...] = jnp.where(wrow == 0, w1 / den, jnp.where(wrow == 1, w2 / den, 0.0)).T


def _ffn_norm_router(x, g, shift, scale, router_w, router_b, seq):
    t, d = x.shape
    tl = min(seq, MOE_TL)
    nblk = t // tl
    nb = shift.shape[0]
    mod_spec = pl.BlockSpec((1, 1, d), (lambda i: (i * tl // seq, 0, 0)) if nb > 1 else (lambda i: (0, 0, 0)))
    return pl.pallas_call(
        _router_kernel,
        grid=(nblk,),
        in_specs=[pl.BlockSpec((tl, d), lambda i: (i, 0)),
                  pl.BlockSpec((1, d), lambda i: (0, 0)),
                  mod_spec, mod_spec,
                  pl.BlockSpec((N_EXPERTS, d), lambda i: (0, 0)),
                  pl.BlockSpec((N_EXPERTS, 1), lambda i: (0, 0))],
        out_specs=[pl.BlockSpec((tl * SUBLANES, LANES), lambda i: (i, 0)),
                   pl.BlockSpec((1, SUBLANES, tl), lambda i: (i, 0, 0)),
                   pl.BlockSpec((tl, LANES), lambda i: (i, 0)),
                   pl.BlockSpec((N_EXPERTS, LANES), lambda i: (0, 0))],
        out_shape=[jax.ShapeDtypeStruct((t * SUBLANES, LANES), F32),
                   jax.ShapeDtypeStruct((nblk, SUBLANES, tl), jnp.int32),
                   jax.ShapeDtypeStruct((t, LANES), F32),
                   jax.ShapeDtypeStruct((N_EXPERTS, LANES), jnp.int32)],
        scratch_shapes=[pltpu.VMEM((N_EXPERTS, LANES), F32)],
        compiler_params=_cparams(("arbitrary",)),
        name="ffn_norm_router",
    )(x, g.reshape(1, d), shift, scale, router_w.T, router_b.reshape(N_EXPERTS, 1))


DMA_UNROLL = 8


def _dispatch_kernel(slot_ref, ends_ref, h_ref, hs_ref, zero_ref, sem, *, tl, max_tiles):
    @pl.when(pl.program_id(0) == 0)
    def _():
        zero_ref[...] = jnp.zeros_like(zero_ref)
        fills = []
        for e in range(N_EXPERTS):
            end = ends_ref[e]
            gap = (MOE_TM - (end & (MOE_TM - 1))) & (MOE_TM - 1)
            for bit in range(MOE_TM.bit_length() - 1):
                size = 1 << bit
                first_row = pl.multiple_of((end + (gap & (size - 1))) * SUBLANES, SUBLANES)
                fills.append(((gap & size) != 0,
                              pltpu.make_async_copy(zero_ref.at[pl.ds(0, size * SUBLANES)],
                                                    hs_ref.at[pl.ds(first_row, size * SUBLANES)], sem)))
        for n in range(N_EXPERTS):
            first_slot = ends_ref[N_EXPERTS] + n * MOE_TM
            first_row = pl.multiple_of(first_slot * SUBLANES, SUBLANES)
            fills.append((first_slot < max_tiles * MOE_TM,
                          pltpu.make_async_copy(zero_ref, hs_ref.at[pl.ds(first_row, MOE_TM * SUBLANES)], sem)))
        for cond, cp in fills:
            pl.when(cond)(cp.start)
        for cond, cp in fills:
            pl.when(cond)(cp.wait)

    def issue(t, carry):
        pltpu.make_async_copy(_token_tile(h_ref, t), _token_tile(hs_ref, slot_ref[0, 0, t]), sem).start()
        pltpu.make_async_copy(_token_tile(h_ref, t), _token_tile(hs_ref, slot_ref[0, 0, tl + t]), sem).start(priority=1)
        return carry

    lax.fori_loop(0, tl, issue, 0, unroll=DMA_UNROLL)
    for _ in range(2):
        pltpu.make_async_copy(h_ref, hs_ref.at[pl.ds(0, tl * SUBLANES)], sem).wait()


def _dispatch(h, slot2, ends, tl, max_tiles):
    t = h.shape[0] // SUBLANES
    nblk = t // tl
    return pl.pallas_call(
        functools.partial(_dispatch_kernel, tl=tl, max_tiles=max_tiles),
        grid=(nblk,),
        in_specs=[pl.BlockSpec((1, 1, 2 * tl), lambda i: (i, 0, 0), memory_space=pltpu.SMEM),
                  pl.BlockSpec(memory_space=pltpu.SMEM),
                  pl.BlockSpec((tl * SUBLANES, LANES), lambda i: (i, 0))],
        out_specs=pl.BlockSpec(memory_space=pl.ANY),
        out_shape=jax.ShapeDtypeStruct((max_tiles * MOE_TM * SUBLANES, LANES), F32),
        scratch_shapes=[pltpu.VMEM((MOE_TM * SUBLANES, LANES), F32), pltpu.SemaphoreType.DMA(())],
        compiler_params=_cparams(("arbitrary",)),
        name="moe_dispatch",
    )(slot2, ends, h)


def _expert_kernel(te_ref, tf_ref, nt_ref, hs_ref, wg_ref, wu_ref, wd_ref, ys_ref, wgb_ref, wub_ref, wdb_ref):
    n = pl.program_id(0)

    @pl.when(n >= nt_ref[0])
    def _():
        ys_ref[...] = jnp.zeros_like(ys_ref)

    @pl.when(n < nt_ref[0])
    def _():
        @pl.when(tf_ref[n] == 1)
        def _():
            wgb_ref[...] = wg_ref[0].astype(BF16)
            wub_ref[...] = wu_ref[0].astype(BF16)
            wdb_ref[...] = wd_ref[0].astype(BF16)

        h = _load_token_tiles(hs_ref).astype(BF16)
        a = jnp.dot(h, wgb_ref[...], preferred_element_type=F32)
        b = jnp.dot(h, wub_ref[...], preferred_element_type=F32)
        y = jnp.dot((_silu(a) * b).astype(BF16), wdb_ref[...], preferred_element_type=F32)
        _store_token_tiles(ys_ref, y)


def _experts(hs, tile_expert, tile_first, n_tiles, w_gate, w_up, w_down, max_tiles):
    d = w_gate.shape[1]
    tile_spec = pl.BlockSpec((MOE_TM * SUBLANES, LANES), lambda n, te, tf, nt: (n, 0))
    grid_spec = pltpu.PrefetchScalarGridSpec(
        num_scalar_prefetch=3,
        grid=(max_tiles,),
        in_specs=[tile_spec,
                  pl.BlockSpec((1, d, D_EXPERT), lambda n, te, tf, nt: (te[n], 0, 0)),
                  pl.BlockSpec((1, d, D_EXPERT), lambda n, te, tf, nt: (te[n], 0, 0)),
                  pl.BlockSpec((1, D_EXPERT, d), lambda n, te, tf, nt: (te[n], 0, 0))],
        out_specs=tile_spec,
        scratch_shapes=[pltpu.VMEM((d, D_EXPERT), BF16), pltpu.VMEM((d, D_EXPERT), BF16),
                        pltpu.VMEM((D_EXPERT, d), BF16)])
    return pl.pallas_call(
        _expert_kernel,
        grid_spec=grid_spec,
        out_shape=jax.ShapeDtypeStruct((max_tiles * MOE_TM * SUBLANES, LANES), F32),
        compiler_params=_cparams(("arbitrary",)),
        name="moe_experts",
    )(tile_expert, tile_first, n_tiles, hs, w_gate, w_up, w_down)


def _combine_kernel(slot_ref, nslot_ref, wt_ref, res_ref, gate_ref, fg_ref, ys_ref, o_ref, y1_ref, y2_ref, sems, *,
                    tl, final):
    step, nsteps = pl.program_id(0), pl.num_programs(0)
    cur = step % 2

    def start_gather(sref, buf):
        def issue(t, carry):
            pltpu.make_async_copy(_token_tile(ys_ref, sref[0, 0, t]), _token_tile(y1_ref.at[buf], t),
                                  sems.at[buf]).start()
            pltpu.make_async_copy(_token_tile(ys_ref, sref[0, 0, tl + t]), _token_tile(y2_ref.at[buf], t),
                                  sems.at[buf]).start(priority=1)
            return carry

        lax.fori_loop(0, tl, issue, 0, unroll=DMA_UNROLL)

    @pl.when(step == 0)
    def _():
        start_gather(slot_ref, 0)

    @pl.when(step + 1 < nsteps)
    def _():
        start_gather(nslot_ref, 1 - cur)

    pltpu.make_async_copy(ys_ref.at[pl.ds(0, tl * SUBLANES)], y1_ref.at[cur], sems.at[cur]).wait()
    pltpu.make_async_copy(ys_ref.at[pl.ds(0, tl * SUBLANES)], y2_ref.at[cur], sems.at[cur]).wait()
    wt = wt_ref[...]
    moe = wt[:, 0:1] * _load_token_tiles(y1_ref.at[cur]) + wt[:, 1:2] * _load_token_tiles(y2_ref.at[cur])
    x = res_ref[...] + gate_ref[0] * moe
    if final:
        ms = jnp.mean(x * x, axis=-1, keepdims=True)
        x = x * lax.rsqrt(ms + EPS) * fg_ref[...]
    o_ref[...] = x


def _combine(ys, slot2, wt, res, gate, final_g, seq, tl):
    t, d = res.shape
    nblk = t // tl
    nb = gate.shape[0]
    final = final_g is not None
    fg = final_g.reshape(1, d) if final else jnp.ones((1, d), F32)
    return pl.pallas_call(
        functools.partial(_combine_kernel, tl=tl, final=final),
        grid=(nblk,),
        in_specs=[pl.BlockSpec((1, 1, 2 * tl), lambda i: (i, 0, 0), memory_space=pltpu.SMEM),
                  pl.BlockSpec((1, 1, 2 * tl), lambda i: (jnp.minimum(i + 1, nblk - 1), 0, 0),
                               memory_space=pltpu.SMEM),
                  pl.BlockSpec((tl, LANES), lambda i: (i, 0)),
                  pl.BlockSpec((tl, d), lambda i: (i, 0)),
                  pl.BlockSpec((1, 1, d), (lambda i: (i * tl // seq, 0, 0)) if nb > 1 else (lambda i: (0, 0, 0))),
                  pl.BlockSpec((1, d), lambda i: (0, 0)),
                  pl.BlockSpec(memory_space=pl.ANY)],
        out_specs=pl.BlockSpec((tl, d), lambda i: (i, 0)),
        out_shape=jax.ShapeDtypeStruct((t, d), F32),
        scratch_shapes=[pltpu.VMEM((2, tl * SUBLANES, LANES), F32), pltpu.VMEM((2, tl * SUBLANES, LANES), F32),
                        pltpu.SemaphoreType.DMA((2,))],
        compiler_params=_cparams(("arbitrary",)),
        name="moe_combine",
    )(slot2, slot2, wt, res, gate, fg, ys)


def _even_layer_mixer(xs, mods, norm_g, w_in, conv_w, conv_b, dt_bias, a_log, d_skip, ssd_g,
                      cf_w, cf_b, cf_lng, cf_lnb, w_out, rows):
    s_lo = SSD_INNER
    dt_lo = SSD_INNER + SSD_CONV_DIM
    cf_lo = dt_lo + 2 * SSD_HEADS
    w_cf = _realign_cols(w_in, cf_lo, 2 * CF_CH)
    bsz = xs[0].shape[0]
    zero_state = jnp.zeros((bsz, SSD_GROUPS, SSD_STATE, SSD_GROUP_W), F32)
    proj = []
    for x, m in zip(xs, mods):
        h = _norm_mod(x, norm_g, m[0], m[1])
        proj.append(dict(z=_matmul(h, w_in, BF16, 0, s_lo), xbc=_matmul(h, w_in, BF16, s_lo, SSD_CONV_DIM),
                         dt=_matmul(h, w_in, F32, dt_lo, LANES), cf=_matmul(h, w_cf, BF16)))
    y_fwd, act, o_ssd = [None, None], [None, None], [None, None]
    state = zero_state
    for si in range(2):
        p = proj[si]
        y_fwd[si], act[si], state = _ssd_scan(p["xbc"], p["dt"], dt_bias[0], a_log[0], state, False,
                                              conv_w=conv_w, conv_b=conv_b, d_skip=d_skip)
    state = zero_state
    for si in range(2):
        p = proj[si]
        o_ssd[si], state = _ssd_scan(act[si], p["dt"], dt_bias[1], a_log[1], state, True,
                                     y_in=y_fwd[si], z=p["z"], norm_g=ssd_g)
    outs = []
    for si, (x, m) in enumerate(zip(xs, mods)):
        slen = x.shape[1] if si == 0 else GRID_W
        o_cf = _conformer(proj[si]["cf"], cf_w, cf_b, cf_lng, cf_lnb, slen)
        outs.append(_matmul_res2(o_ssd[si], o_cf, w_out, x, m[2]))
    return outs


def _odd_layer_mixer(xc, xl, m_c, m_l, norm_g, w_in, lb, hg_g, w_out):
    st0 = HG_KEY + HG_VAL
    f0 = st0 + HG_VAL
    bsz = xl.shape[0]
    h_c = _norm_mod(xc, norm_g, m_c[0], m_c[1])
    h_l = _norm_mod_colmajor(xl, norm_g, m_l[0], m_l[1])
    v_c, f_c = _matmul(h_c, w_in, BF16, st0, HG_VAL), _matmul(h_c, w_in, F32, f0, 2 * HG_KEY)
    q_l, g_l = _matmul(h_l, w_in, BF16, 0, HG_KEY), _matmul(h_l, w_in, BF16, HG_KEY, HG_VAL)
    v_l, f_l = _matmul(h_l, w_in, BF16, st0, HG_VAL), _matmul(h_l, w_in, F32, f0, 2 * HG_KEY)
    zero_state = jnp.zeros((bsz, HG_HEADS, HG_DV, HG_DK), F32)
    s_f = _hgrn2_scan(v_c, f_c, lb, zero_state, False, "state")
    s_r = _hgrn2_scan(v_c, f_c, lb, zero_state, True, "state")
    o_f = _hgrn2_scan(v_l, f_l, lb, s_f, False, "out", q=q_l)
    o = _hgrn2_scan(v_l, f_l, lb, s_r, True, "readout", q=q_l, o_in=o_f, g=g_l, norm_g=hg_g)
    return _matmul_res_colmajor(o, w_out, xl, m_l[2])


def _dispatch_layout(counts, slot, max_tiles):
    tiles = (counts + MOE_TM - 1) // MOE_TM
    tile_ends = jnp.cumsum(tiles)
    tile_starts = tile_ends - tiles
    total = tile_ends[-1]
    starts = tile_starts * MOE_TM
    ends = jnp.concatenate([starts + counts, (total * MOE_TM).reshape(1)]).astype(jnp.int32)
    experts = jnp.arange(N_EXPERTS, dtype=jnp.int32)

    def lookup(table, idx):
        return jnp.sum(jnp.where(idx[..., None] == experts, table, 0), axis=-1)

    slots = jnp.concatenate([lookup(starts, slot[:, 0]) + slot[:, 2], lookup(starts, slot[:, 1]) + slot[:, 3]],
                            axis=-1)
    n = jnp.arange(max_tiles, dtype=jnp.int32)
    nn = jnp.minimum(n, total - 1)
    expert = jnp.sum((nn[:, None] >= tile_ends[None, :]).astype(jnp.int32), axis=1)
    is_first = ((nn == lookup(tile_starts, expert)) & (n < total)).astype(jnp.int32)
    return slots[:, None, :].astype(jnp.int32), ends, expert, is_first, total.reshape(1).astype(jnp.int32)


def _moe_block(x, m, norm_g, router_w, router_b, w_gate, w_up, w_down, final_g=None):
    bsz, seq, d = x.shape
    t = bsz * seq
    xf = x.reshape(t, d)
    tl = min(seq, MOE_TL)
    h, slot, wt, cnt = _ffn_norm_router(xf, norm_g, m[3], m[4], router_w, router_b, seq)
    max_tiles = 2 * t // MOE_TM + N_EXPERTS
    slot2, ends, tile_expert, tile_first, n_tiles = _dispatch_layout(cnt[:, 0], slot, max_tiles)
    hs = _dispatch(h, slot2, ends, tl, max_tiles)
    ys = _experts(hs, tile_expert, tile_first, n_tiles, w_gate, w_up, w_down, max_tiles)
    return _combine(ys, slot2, wt, xf, m[5], final_g, seq, tl).reshape(bsz, seq, d)


def kernel(x, c, ctx, c_ctx, mod_w, mod_b, norm_mix_g, norm_ffn_g, router_w, router_b, moe_w_gate, moe_w_up,
           moe_w_down, ab_w_in, ssd_conv_w, ssd_conv_b, ssd_dt_bias, ssd_a_log, ssd_d, ssd_norm_g, cf_dw_w,
           cf_dw_b, cf_ln_g, cf_ln_b, ab_w_out, hg_w_in, hg_lb, hg_norm_g, hg_w_out, final_norm_g):
    depth = mod_w.shape[0]
    assert depth == 2, "layer schedule below is written for one even and one odd layer"
    bsz, seq, d = x.shape
    rows = seq // GRID_W
    lb_all = jnp.cumsum(jax.nn.softmax(hg_lb.astype(F32), axis=0), axis=0)
    lb_all = lb_all - lb_all[0]

    nrow = -(-(bsz + 1) // SUBLANES) * SUBLANES
    cond = jnp.zeros((nrow, d), F32).at[:bsz].set(c).at[bsz].set(c_ctx)
    mod = _modulation(cond, mod_w, mod_b).reshape(depth, nrow, N_MOD, 1, d)

    def mods(l):
        m_l = [mod[l, :bsz, k] for k in range(N_MOD)]
        m_c = [mod[l, bsz:bsz + 1, k] for k in range(N_MOD)]
        return m_c, m_l

    m_c, m_l = mods(0)
    xc, xl = _even_layer_mixer((ctx, x), (m_c, m_l), norm_mix_g[0], ab_w_in[0], ssd_conv_w[0], ssd_conv_b[0],
                               ssd_dt_bias[0], ssd_a_log[0], ssd_d[0], ssd_norm_g[0], cf_dw_w[0], cf_dw_b[0],
                               cf_ln_g[0], cf_ln_b[0], ab_w_out[0], rows)
    xl = _moe_block(xl, m_l, norm_ffn_g[0], router_w, router_b, moe_w_gate[0], moe_w_up[0], moe_w_down[0])
    xc = _moe_block(xc, m_c, norm_ffn_g[0], router_w, router_b, moe_w_gate[0], moe_w_up[0], moe_w_down[0])

    m_c, m_l = mods(1)
    xl = _odd_layer_mixer(xc, xl, m_c, m_l, norm_mix_g[1], hg_w_in[0], lb_all[1], hg_norm_g[0], hg_w_out[0])
    return _moe_block(xl, m_l, norm_ffn_g[1], router_w, router_b, moe_w_gate[1], moe_w_up[1], moe_w_down[1],
                      final_g=final_norm_g)
```

```python
import functools

import jax
import jax.numpy as jnp
from jax import lax
from jax.experimental import pallas as pl
from jax.experimental.pallas import tpu as pltpu

F32 = jnp.float32
BF16 = jnp.bfloat16

D_MODEL = 1024
GRID_W = 64
EPS = 1e-6
N_MOD = 6

SSD_HEADS = 16
SSD_HEAD_DIM = 64
SSD_INNER = SSD_HEADS * SSD_HEAD_DIM
SSD_GROUPS = 4
SSD_STATE = 128
SSD_CONV = 5
SSD_CHUNK = 128
SSD_BC = SSD_GROUPS * SSD_STATE
SSD_CONV_DIM = SSD_INNER + 2 * SSD_BC
SSD_GROUP_W = SSD_INNER // SSD_GROUPS

CF_CH = 1024
CF_KERNEL = 31
CF_PAD = 16

HG_HEADS = 8
HG_DK = 128
HG_DV = 128
HG_KEY = HG_HEADS * HG_DK
HG_VAL = HG_HEADS * HG_DV
HG_CHUNK = 64
HG_STEP = 256

N_EXPERTS = 16
N_EXPERT_GROUPS = 4
EXPERTS_PER_GROUP = 4
D_EXPERT = 512
MOE_TL = 512
MOE_TM = 512

LANES = 128
SUBLANES = 8
HALO = 16
VMEM_LIMIT = 48 * 1024 * 1024


def _cparams(sem):
    return pltpu.CompilerParams(dimension_semantics=sem, vmem_limit_bytes=VMEM_LIMIT)


def _silu(x):
    return x * jax.nn.sigmoid(x)


def _split3(v):
    hi = v.astype(BF16)
    r1 = v - hi.astype(F32)
    mid = r1.astype(BF16)
    lo = (r1 - mid.astype(F32)).astype(BF16)
    return hi, mid, lo


def _dot01(m01, v):
    hi, mid, lo = _split3(v)
    out = jnp.dot(m01, lo, preferred_element_type=F32)
    out = out + jnp.dot(m01, mid, preferred_element_type=F32)
    return out + jnp.dot(m01, hi, preferred_element_type=F32)


def _dot_v01(v, m01):
    hi, mid, lo = _split3(v)
    out = jnp.dot(lo, m01, preferred_element_type=F32)
    out = out + jnp.dot(mid, m01, preferred_element_type=F32)
    return out + jnp.dot(hi, m01, preferred_element_type=F32)


_NN = (((1,), (0,)), ((), ()))
_NT = (((1,), (1,)), ((), ()))
_TN = (((0,), (0,)), ((), ()))


def _dot_f32(a, b, dn):
    a1, a2, a3 = _split3(a)
    b1, b2, b3 = _split3(b)
    out = lax.dot_general(a3, b1, dn, preferred_element_type=F32)
    out = out + lax.dot_general(a1, b3, dn, preferred_element_type=F32)
    out = out + lax.dot_general(a2, b2, dn, preferred_element_type=F32)
    out = out + lax.dot_general(a2, b1, dn, preferred_element_type=F32)
    out = out + lax.dot_general(a1, b2, dn, preferred_element_type=F32)
    return out + lax.dot_general(a1, b1, dn, preferred_element_type=F32)


def _scan_tri(n, rev):
    r = lax.broadcasted_iota(jnp.int32, (n, n), 0)
    c = lax.broadcasted_iota(jnp.int32, (n, n), 1)
    return ((c >= r) if rev else (c <= r))


def _mod_kernel(c_ref, w_ref, b_ref, o_ref):
    cond = _silu(c_ref[...])
    o_ref[0] = _dot_f32(cond, w_ref[0], _NN) + b_ref[0]


def _modulation(cond, mod_w, mod_b):
    depth, d, n = mod_w.shape
    r = cond.shape[0]
    tn = 512
    return pl.pallas_call(
        _mod_kernel,
        grid=(depth, n // tn),
        in_specs=[pl.BlockSpec((r, d), lambda l, j: (0, 0)),
                  pl.BlockSpec((1, d, tn), lambda l, j: (l, 0, j)),
                  pl.BlockSpec((1, 1, tn), lambda l, j: (l, 0, j))],
        out_specs=pl.BlockSpec((1, r, tn), lambda l, j: (l, 0, j)),
        out_shape=jax.ShapeDtypeStruct((depth, r, n), F32),
        compiler_params=_cparams(("parallel", "parallel")),
        name="modulation",
    )(cond, mod_w, mod_b.reshape(depth, 1, n))


def _normmod(x, g, shift, scale):
    ms = jnp.mean(x * x, axis=-1, keepdims=True)
    return (x * lax.rsqrt(ms + EPS) * g) * (1.0 + scale) + shift


def _normmod_kernel(x_ref, g_ref, sh_ref, sc_ref, o_ref):
    o_ref[0] = _normmod(x_ref[0], g_ref[...], sh_ref[0], sc_ref[0]).astype(o_ref.dtype)


def _mod_spec(nb):
    return pl.BlockSpec((1, 1, D_MODEL), (lambda b, *_: (b, 0, 0)) if nb > 1 else (lambda b, *_: (0, 0, 0)))


def _norm_mod(x, g, shift, scale):
    bsz, seq, d = x.shape
    tl = min(seq, 512)
    return pl.pallas_call(
        _normmod_kernel,
        grid=(bsz, seq // tl),
        in_specs=[pl.BlockSpec((1, tl, d), lambda b, i: (b, i, 0)),
                  pl.BlockSpec((1, d), lambda b, i: (0, 0)),
                  _mod_spec(shift.shape[0]), _mod_spec(scale.shape[0])],
        out_specs=pl.BlockSpec((1, tl, d), lambda b, i: (b, i, 0)),
        out_shape=jax.ShapeDtypeStruct((bsz, seq, d), BF16),
        compiler_params=_cparams(("parallel", "parallel")),
        name="norm_mod",
    )(x, g.reshape(1, d), shift, scale)


def _normmod_cm_kernel(x_ref, g_ref, sh_ref, sc_ref, o_ref, *, rows):
    for c in range(SUBLANES):
        piece = x_ref[0, :, c, :]
        o_ref[0, c * rows:(c + 1) * rows, :] = _normmod(piece, g_ref[...], sh_ref[0], sc_ref[0]).astype(o_ref.dtype)


def _norm_mod_colmajor(x, g, shift, scale):
    bsz, seq, d = x.shape
    rows = seq // GRID_W
    x4 = x.reshape(bsz, rows, GRID_W, d)
    return pl.pallas_call(
        functools.partial(_normmod_cm_kernel, rows=rows),
        grid=(bsz, GRID_W // SUBLANES),
        in_specs=[pl.BlockSpec((1, rows, SUBLANES, d), lambda b, i: (b, 0, i, 0)),
                  pl.BlockSpec((1, d), lambda b, i: (0, 0)),
                  _mod_spec(shift.shape[0]), _mod_spec(scale.shape[0])],
        out_specs=pl.BlockSpec((1, SUBLANES * rows, d), lambda b, i: (b, i, 0)),
        out_shape=jax.ShapeDtypeStruct((bsz, seq, d), BF16),
        compiler_params=_cparams(("parallel", "parallel")),
        name="norm_mod_colmajor",
    )(x4, g.reshape(1, d), shift, scale)


MM_SEMANTICS = ("arbitrary", "arbitrary", "arbitrary")


def _stage_weights(b_ref, wb_ref):
    @pl.when((pl.program_id(1) == 0) & (pl.program_id(2) == 0))
    def _():
        wb_ref[...] = b_ref[...].astype(BF16)


def _mm_kernel(a_ref, b_ref, o_ref, wb_ref):
    _stage_weights(b_ref, wb_ref)
    o_ref[0] = jnp.dot(a_ref[0], wb_ref[...], preferred_element_type=F32).astype(o_ref.dtype)


def _mm_res2_kernel(a1_ref, a2_ref, b1_ref, b2_ref, res_ref, gate_ref, o_ref, wb1_ref, wb2_ref):
    _stage_weights(b1_ref, wb1_ref)
    _stage_weights(b2_ref, wb2_ref)
    y = jnp.dot(a1_ref[0], wb1_ref[...], preferred_element_type=F32)
    y = y + jnp.dot(a2_ref[0], wb2_ref[...], preferred_element_type=F32)
    o_ref[0] = res_ref[0] + gate_ref[0] * y


def _mm_res_cm_kernel(a_ref, b_ref, res_ref, gate_ref, o_ref, wb_ref, *, rows):
    _stage_weights(b_ref, wb_ref)
    y = jnp.dot(a_ref[0], wb_ref[...], preferred_element_type=F32)
    for c in range(SUBLANES):
        o_ref[0, :, c, :] = res_ref[0, :, c, :] + gate_ref[0] * y[c * rows:(c + 1) * rows, :]


def _matmul(a, w, out_dtype, col0=0, ncols=None):
    bsz, seq, k = a.shape
    n = w.shape[1] - col0 if ncols is None else ncols
    tm, tn = min(seq, 1024), min(n, 1024)
    assert col0 % tn == 0 and n % tn == 0
    jb = col0 // tn
    return pl.pallas_call(
        _mm_kernel,
        grid=(n // tn, bsz, seq // tm),
        in_specs=[pl.BlockSpec((1, tm, k), lambda j, b, i: (b, i, 0)),
                  pl.BlockSpec((k, tn), lambda j, b, i: (0, j + jb))],
        out_specs=pl.BlockSpec((1, tm, tn), lambda j, b, i: (b, i, j)),
        out_shape=jax.ShapeDtypeStruct((bsz, seq, n), out_dtype),
        scratch_shapes=[pltpu.VMEM((k, tn), BF16)],
        compiler_params=_cparams(MM_SEMANTICS),
        name="matmul",
    )(a, w)


def _matmul_res2(a1, a2, w, res, gate):
    bsz, seq, kh = a1.shape
    n = w.shape[1]
    tm, tn = min(seq, 1024), min(n, 512)
    nb = gate.shape[0]
    return pl.pallas_call(
        _mm_res2_kernel,
        grid=(n // tn, bsz, seq // tm),
        in_specs=[pl.BlockSpec((1, tm, kh), lambda j, b, i: (b, i, 0)),
                  pl.BlockSpec((1, tm, kh), lambda j, b, i: (b, i, 0)),
                  pl.BlockSpec((kh, tn), lambda j, b, i: (0, j)),
                  pl.BlockSpec((kh, tn), lambda j, b, i: (1, j)),
                  pl.BlockSpec((1, tm, tn), lambda j, b, i: (b, i, j)),
                  pl.BlockSpec((1, 1, tn), (lambda j, b, i: (b, 0, j)) if nb > 1 else (lambda j, b, i: (0, 0, j)))],
        out_specs=pl.BlockSpec((1, tm, tn), lambda j, b, i: (b, i, j)),
        out_shape=jax.ShapeDtypeStruct((bsz, seq, n), F32),
        scratch_shapes=[pltpu.VMEM((kh, tn), BF16), pltpu.VMEM((kh, tn), BF16)],
        compiler_params=_cparams(MM_SEMANTICS),
        name="matmul_res",
    )(a1, a2, w, w, res, gate)


def _realign_kernel(w_ref, o_ref, *, col0):
    o_ref[...] = w_ref[:, col0:col0 + o_ref.shape[1]].astype(o_ref.dtype)


def _realign_cols(w, col0, ncols):
    k, n = w.shape
    tk = 128
    return pl.pallas_call(
        functools.partial(_realign_kernel, col0=col0),
        grid=(k // tk,),
        in_specs=[pl.BlockSpec((tk, n), lambda i: (i, 0))],
        out_specs=pl.BlockSpec((tk, ncols), lambda i: (i, 0)),
        out_shape=jax.ShapeDtypeStruct((k, ncols), BF16),
        compiler_params=_cparams(("parallel",)),
        name="realign_cols",
    )(w)


def _matmul_res_colmajor(a, w, res, gate):
    bsz, seq, k = a.shape
    n = w.shape[1]
    rows = seq // GRID_W
    tm, tn = SUBLANES * rows, min(n, 512)
    res4 = res.reshape(bsz, rows, GRID_W, n)
    out = pl.pallas_call(
        functools.partial(_mm_res_cm_kernel, rows=rows),
        grid=(n // tn, bsz, seq // tm),
        in_specs=[pl.BlockSpec((1, tm, k), lambda j, b, i: (b, i, 0)),
                  pl.BlockSpec((k, tn), lambda j, b, i: (0, j)),
                  pl.BlockSpec((1, rows, SUBLANES, tn), lambda j, b, i: (b, 0, i, j)),
                  pl.BlockSpec((1, 1, tn), lambda j, b, i: (b, 0, j))],
        out_specs=pl.BlockSpec((1, rows, SUBLANES, tn), lambda j, b, i: (b, 0, i, j)),
        out_shape=jax.ShapeDtypeStruct((bsz, rows, GRID_W, n), F32),
        scratch_shapes=[pltpu.VMEM((k, tn), BF16)],
        compiler_params=_cparams(MM_SEMANTICS),
        name="matmul_res_colmajor",
    )(a, w, res4, gate)
    return out.reshape(bsz, seq, n)


def _ssd_kernel(*refs, rev, nchunks, readout):
    if readout:
        act_ref, dt_ref, dtb_ref, alog_ref, s0_ref, yin_ref, z_ref, ng_ref, y_ref, sfin_ref, st_ref = refs
    else:
        (xbc_ref, prev_ref, next_ref, dt_ref, cw_ref, cb_ref, dtb_ref, alog_ref, dsk_ref, s0_ref,
         y_ref, act_ref, sfin_ref, pad_ref, st_ref) = refs
    step = pl.program_id(1)
    u = (nchunks - 1 - step) if rev else step
    ck = SSD_CHUNK
    hoff = SSD_HEADS if rev else 0

    @pl.when(step == 0)
    def _():
        st_ref[...] = s0_ref[0]

    if readout:
        xbc = act_ref[0].astype(F32)
    else:
        pad_ref[HALO:HALO + ck, :] = xbc_ref[0].astype(F32)
        pad_ref[0:HALO, :] = jnp.where(u > 0, prev_ref[0].astype(F32), 0.0)
        pad_ref[HALO + ck:2 * HALO + ck, :] = jnp.where(u < nchunks - 1, next_ref[0].astype(F32), 0.0)
        half = SSD_CONV // 2
        acc = cb_ref[...] + cw_ref[0:1, :] * pad_ref[pl.ds(HALO - half, ck), :]
        for k in range(1, SSD_CONV):
            acc = acc + cw_ref[k:k + 1, :] * pad_ref[pl.ds(HALO - half + k, ck), :]
        xbc = _silu(acc)
        act_ref[0] = xbc.astype(act_ref.dtype)
    xs = xbc[:, :SSD_INNER]

    hlane = lax.broadcasted_iota(jnp.int32, (ck, LANES), 1)
    is_head = (hlane >= hoff) & (hlane < hoff + SSD_HEADS)
    dt = jnp.where(is_head, jax.nn.softplus(dt_ref[0] + dtb_ref[...]), 0.0)
    la = dt * (-jnp.exp(alog_ref[...]))
    tri = _scan_tri(ck, rev).astype(BF16)
    acum = _dot01(tri, la)
    last = 0 if rev else ck - 1
    total = acum[last:last + 1, :]
    acum_t = acum.T

    hrow = lax.broadcasted_iota(jnp.int32, (LANES, SSD_INNER), 0)
    hcol = lax.broadcasted_iota(jnp.int32, (LANES, SSD_INNER), 1)
    expand = (hcol // SSD_HEAD_DIM + hoff == hrow).astype(BF16)
    dt_x = _dot_v01(dt, expand)
    ea_x = _dot_v01(jnp.exp(acum), expand)
    te_x = _dot_v01(jnp.exp(total - acum), expand)
    cd_x = _dot_v01(jnp.broadcast_to(jnp.exp(total), (SUBLANES, LANES)), expand)[0:1, :]

    xdt = xs * dt_x
    xdt_b = xdt.astype(BF16)
    xw_b = (xdt * te_x).astype(BF16)

    r = lax.broadcasted_iota(jnp.int32, (ck, ck), 0)
    c = lax.broadcasted_iota(jnp.int32, (ck, ck), 1)
    causal = (c >= r) if rev else (c <= r)
    lane = lax.broadcasted_iota(jnp.int32, (ck, LANES), 1)
    nt = (((1,), (1,)), ((), ()))
    tn = (((0,), (0,)), ((), ()))

    y_parts = []
    for g in range(SSD_GROUPS):
        b_g = xbc[:, SSD_INNER + g * SSD_STATE:SSD_INNER + (g + 1) * SSD_STATE].astype(BF16)
        c_g = xbc[:, SSD_INNER + SSD_BC + g * SSD_STATE:SSD_INNER + SSD_BC + (g + 1) * SSD_STATE].astype(BF16)
        cb = lax.dot_general(c_g, b_g, nt, preferred_element_type=F32)
        lo = g * SSD_GROUP_W
        s_in = st_ref[g]
        y_off = jnp.dot(c_g, s_in.astype(BF16), preferred_element_type=F32) * ea_x[:, lo:lo + SSD_GROUP_W]
        chunk_state = lax.dot_general(b_g, xw_b[:, lo:lo + SSD_GROUP_W], tn, preferred_element_type=F32)
        st_ref[g] = cd_x[:, lo:lo + SSD_GROUP_W] * s_in + chunk_state
        heads_per_group = SSD_HEADS // SSD_GROUPS
        for pair in range(heads_per_group // 2):
            plo = lo + pair * LANES
            xpair = xdt_b[:, plo:plo + LANES]
            y_pair = y_off[:, pair * LANES:(pair + 1) * LANES]
            for half_i in range(2):
                h = hoff + g * heads_per_group + pair * 2 + half_i
                seg = acum[:, h:h + 1] - acum_t[h:h + 1, :]
                m = (cb * jnp.where(causal, jnp.exp(seg), 0.0)).astype(BF16)
                in_half = (lane >= half_i * SSD_HEAD_DIM) & (lane < (half_i + 1) * SSD_HEAD_DIM)
                y_pair = y_pair + jnp.dot(m, jnp.where(in_half, xpair, jnp.zeros_like(xpair)),
                                          preferred_element_type=F32)
            y_parts.append(y_pair)
    y = jnp.concatenate(y_parts, axis=1)

    if readout:
        y = y + yin_ref[0].astype(F32)
        y = y * _silu(z_ref[0].astype(F32))
        ms = jnp.mean(y * y, axis=-1, keepdims=True)
        y_ref[0] = (y * lax.rsqrt(ms + EPS) * ng_ref[...]).astype(y_ref.dtype)
    else:
        y_ref[0] = (y + dsk_ref[...] * xs).astype(y_ref.dtype)

    @pl.when(step == nchunks - 1)
    def _():
        sfin_ref[0] = st_ref[...]


def _ssd_scan(xbc, dt_raw, dt_bias, a_log, s0, rev, conv_w=None, conv_b=None, d_skip=None,
              y_in=None, z=None, norm_g=None):
    bsz, seq, _ = xbc.shape
    nchunks = seq // SSD_CHUNK
    hb = SSD_CHUNK // HALO
    nhalo = seq // HALO
    hoff = SSD_HEADS if rev else 0

    def cidx(i):
        return (nchunks - 1 - i) if rev else i

    def head_lanes(v):
        return jnp.pad(v.reshape(1, -1), ((0, 0), (hoff, LANES - hoff - v.shape[-1])))

    chunk = lambda width: pl.BlockSpec((1, SSD_CHUNK, width), lambda b, i: (b, cidx(i), 0))
    row = lambda width: pl.BlockSpec((1, width), lambda b, i: (0, 0))
    state_spec = pl.BlockSpec((1, SSD_GROUPS, SSD_STATE, SSD_GROUP_W), lambda b, i: (b, 0, 0, 0))
    state_shape = jax.ShapeDtypeStruct((bsz, SSD_GROUPS, SSD_STATE, SSD_GROUP_W), F32)
    y_shape = jax.ShapeDtypeStruct((bsz, seq, SSD_INNER), BF16)
    state_scratch = pltpu.VMEM((SSD_GROUPS, SSD_STATE, SSD_GROUP_W), F32)
    if rev:
        in_specs = [chunk(SSD_CONV_DIM), chunk(LANES), row(LANES), row(LANES), state_spec,
                    chunk(SSD_INNER), chunk(SSD_INNER), row(SSD_INNER)]
        args = [xbc, dt_raw, head_lanes(dt_bias), head_lanes(a_log), s0, y_in, z, norm_g.reshape(1, -1)]
        out_specs, out_shape = [chunk(SSD_INNER), state_spec], [y_shape, state_shape]
        scratch = [state_scratch]
    else:
        in_specs = [
            chunk(SSD_CONV_DIM),
            pl.BlockSpec((1, HALO, SSD_CONV_DIM), lambda b, i: (b, jnp.maximum(i * hb - 1, 0), 0)),
            pl.BlockSpec((1, HALO, SSD_CONV_DIM), lambda b, i: (b, jnp.minimum((i + 1) * hb, nhalo - 1), 0)),
            chunk(LANES),
            pl.BlockSpec((SUBLANES, SSD_CONV_DIM), lambda b, i: (0, 0)), row(SSD_CONV_DIM),
            row(LANES), row(LANES), row(SSD_INNER), state_spec]
        args = [xbc, xbc, xbc, dt_raw,
                jnp.pad(conv_w, ((0, SUBLANES - SSD_CONV), (0, 0))), conv_b.reshape(1, -1),
                head_lanes(dt_bias), head_lanes(a_log), jnp.repeat(d_skip, SSD_HEAD_DIM).reshape(1, -1), s0]
        out_specs = [chunk(SSD_INNER), chunk(SSD_CONV_DIM), state_spec]
        out_shape = [y_shape, jax.ShapeDtypeStruct((bsz, seq, SSD_CONV_DIM), BF16), state_shape]
        scratch = [pltpu.VMEM((SSD_CHUNK + 2 * HALO, SSD_CONV_DIM), F32), state_scratch]
    return pl.pallas_call(
        functools.partial(_ssd_kernel, rev=rev, nchunks=nchunks, readout=rev),
        grid=(bsz, nchunks),
        in_specs=in_specs, out_specs=out_specs, out_shape=out_shape, scratch_shapes=scratch,
        compiler_params=_cparams(("parallel", "arbitrary")),
        name="ssd_scan_rev" if rev else "ssd_scan_fwd",
    )(*args)


def _conformer_kernel(p_ref, w_ref, b_ref, g_ref, beta_ref, o_ref, pad_ref, rot_ref, conv_ref, *, nseq, slen):
    v = p_ref[0, :, :CF_CH].astype(F32)
    gate = p_ref[0, :, CF_CH:].astype(F32)
    u = v * jax.nn.sigmoid(gate)
    first = CF_PAD - CF_KERNEL // 2
    span = rot_ref.shape[2]
    for s in range(nseq):
        pad_ref[s, 0:CF_PAD, :] = jnp.zeros((CF_PAD, CF_CH), F32)
        pad_ref[s, CF_PAD + slen:2 * CF_PAD + slen, :] = jnp.zeros((CF_PAD, CF_CH), F32)
        pad_ref[s, CF_PAD:CF_PAD + slen, :] = u[s * slen:(s + 1) * slen, :]
        for res in range(1, SUBLANES):
            rot_ref[res - 1, s] = pad_ref[s, pl.ds(res, span), :]
    for s in range(nseq):
        for cb in range(CF_CH // LANES):
            ch = slice(cb * LANES, (cb + 1) * LANES)
            acc = jnp.broadcast_to(b_ref[:, ch], (slen, LANES))
            for k in range(CF_KERNEL):
                res, lead = (first + k) % SUBLANES, (first + k) // SUBLANES * SUBLANES
                src = pad_ref[s, pl.ds(lead, slen), ch] if res == 0 else rot_ref[res - 1, s, pl.ds(lead, slen), ch]
                acc = acc + w_ref[k:k + 1, ch] * src
            conv_ref[s * slen:(s + 1) * slen, ch] = acc
    y = conv_ref[...]
    mu = jnp.mean(y, axis=-1, keepdims=True)
    xc = y - mu
    var = jnp.mean(xc * xc, axis=-1, keepdims=True)
    y = xc * lax.rsqrt(var + EPS) * g_ref[...] + beta_ref[...]
    o_ref[0] = _silu(y).astype(o_ref.dtype)


def _conformer(p_cf, dw_w, dw_b, ln_g, ln_b, slen):
    bsz, seq, _ = p_cf.shape
    nseq = max(1, min(seq, 256) // slen)
    tb = nseq * slen
    kpad = -(-CF_KERNEL // SUBLANES) * SUBLANES
    last_tap_row = CF_PAD - CF_KERNEL // 2 + CF_KERNEL - 1
    span = slen + last_tap_row // SUBLANES * SUBLANES
    return pl.pallas_call(
        functools.partial(_conformer_kernel, nseq=nseq, slen=slen),
        grid=(bsz, seq // tb),
        in_specs=[pl.BlockSpec((1, tb, 2 * CF_CH), lambda b, i: (b, i, 0)),
                  pl.BlockSpec((kpad, CF_CH), lambda b, i: (0, 0)),
                  pl.BlockSpec((1, CF_CH), lambda b, i: (0, 0)),
                  pl.BlockSpec((1, CF_CH), lambda b, i: (0, 0)),
                  pl.BlockSpec((1, CF_CH), lambda b, i: (0, 0))],
        out_specs=pl.BlockSpec((1, tb, CF_CH), lambda b, i: (b, i, 0)),
        out_shape=jax.ShapeDtypeStruct((bsz, seq, CF_CH), BF16),
        scratch_shapes=[pltpu.VMEM((nseq, slen + 2 * CF_PAD, CF_CH), F32),
                        pltpu.VMEM((SUBLANES - 1, nseq, span, CF_CH), F32),
                        pltpu.VMEM((tb, CF_CH), F32)],
        compiler_params=_cparams(("parallel", "parallel")),
        name="conformer_conv",
    )(p_cf, jnp.pad(dw_w, ((0, kpad - CF_KERNEL), (0, 0))), dw_b.reshape(1, -1), ln_g.reshape(1, -1),
      ln_b.reshape(1, -1))


def _hgrn2_kernel(*refs, rev, mode):
    if mode == "state":
        v_ref, f_ref, lb_ref, s0_ref, sfin_ref, st_ref = refs
    elif mode == "out":
        q_ref, v_ref, f_ref, lb_ref, s0_ref, o_ref, st_ref = refs
    else:
        q_ref, v_ref, f_ref, lb_ref, s0_ref, oin_ref, g_ref, ng_ref, o_ref, st_ref = refs
    step = pl.program_id(1)
    ck = HG_CHUNK
    per_step = v_ref.shape[1] // ck

    @pl.when(step == 0)
    def _():
        st_ref[...] = s0_ref[0]

    lb = lb_ref[...]
    tri = _scan_tri(ck, rev).astype(BF16)
    last = 0 if rev else ck - 1
    mid_pos = HG_CHUNK // 2 - 1
    mid = (ck - 1 - mid_pos) if rev else mid_pos
    r = lax.broadcasted_iota(jnp.int32, (ck, ck), 0)
    c = lax.broadcasted_iota(jnp.int32, (ck, ck), 1)
    causal = (c >= r) if rev else (c <= r)

    for ci in (range(per_step - 1, -1, -1) if rev else range(per_step)):
        rows = slice(ci * ck, (ci + 1) * ck)
        f = lb + (1.0 - lb) * jax.nn.sigmoid(f_ref[0, rows, :])
        kk = 1.0 - f
        gcum = _dot01(tri, jnp.log(f))
        g_end = gcum[last:last + 1, :]
        k_end = (kk * jnp.exp(g_end - gcum)).astype(BF16)
        dec_end = jnp.exp(g_end)
        v = v_ref[0, rows, :]
        if mode != "state":
            g_mid = gcum[mid:mid + 1, :]
            q = _silu(q_ref[0, rows, :].astype(F32))
            q_rel = (q * jnp.exp(gcum - g_mid)).astype(BF16)
            k_rel = (kk * jnp.exp(g_mid - gcum)).astype(BF16)
            q_dec = (q * jnp.exp(gcum)).astype(BF16)
        if mode == "readout":
            gate = _silu(g_ref[0, rows, :].astype(F32))

        for h in range(HG_HEADS):
            ks = slice(h * HG_DK, (h + 1) * HG_DK)
            vs = slice(h * HG_DV, (h + 1) * HG_DV)
            s_in = st_ref[h]
            if mode != "state":
                att = lax.dot_general(q_rel[:, ks], k_rel[:, ks], _NT, preferred_element_type=F32)
                att = jnp.where(causal, att, 0.0).astype(BF16)
                o_h = jnp.dot(att, v[:, vs], preferred_element_type=F32)
                o_h = o_h + lax.dot_general(q_dec[:, ks], s_in.astype(BF16), _NT, preferred_element_type=F32)
                if mode == "readout":
                    o_h = o_h + oin_ref[0, rows, vs].astype(F32)
                    ms = jnp.mean(o_h * o_h, axis=-1, keepdims=True)
                    o_h = (o_h * lax.rsqrt(ms + EPS) * ng_ref[...]) * gate[:, vs]
                o_ref[0, rows, vs] = o_h.astype(o_ref.dtype)
            chunk_state = lax.dot_general(v[:, vs], k_end[:, ks], _TN, preferred_element_type=F32)
            st_ref[h] = s_in * dec_end[:, ks] + chunk_state

    if mode == "state":
        @pl.when(step == pl.num_programs(1) - 1)
        def _():
            sfin_ref[0] = st_ref[...]


def _hgrn2_scan(v, f_raw, lb, s0, rev, mode, q=None, o_in=None, g=None, norm_g=None):
    bsz, seq, _ = v.shape
    tstep = min(seq, HG_STEP)
    nsteps = seq // tstep
    dcol = 1 if rev else 0

    def cidx(i):
        return (nsteps - 1 - i) if rev else i

    tok = lambda width: pl.BlockSpec((1, tstep, width), lambda b, i: (b, cidx(i), 0))
    state_spec = pl.BlockSpec((1, HG_HEADS, HG_DV, HG_DK), lambda b, i: (b, 0, 0, 0))
    f_spec = pl.BlockSpec((1, tstep, HG_KEY), lambda b, i: (b, cidx(i), dcol))
    lb_spec = pl.BlockSpec((1, HG_KEY), lambda b, i: (0, 0))
    state_shape = jax.ShapeDtypeStruct((bsz, HG_HEADS, HG_DV, HG_DK), F32)
    if mode == "state":
        in_specs, args = [tok(HG_VAL), f_spec, lb_spec, state_spec], [v, f_raw, lb.reshape(1, -1), s0]
        out_specs, out_shape = state_spec, state_shape
    else:
        in_specs = [tok(HG_KEY), tok(HG_VAL), f_spec, lb_spec, state_spec]
        args = [q, v, f_raw, lb.reshape(1, -1), s0]
        if mode == "readout":
            in_specs += [tok(HG_VAL), tok(HG_VAL), pl.BlockSpec((1, HG_DV), lambda b, i: (0, 0))]
            args += [o_in, g, norm_g.reshape(1, -1)]
        out_specs, out_shape = tok(HG_VAL), jax.ShapeDtypeStruct((bsz, seq, HG_VAL), BF16)
    return pl.pallas_call(
        functools.partial(_hgrn2_kernel, rev=rev, mode=mode),
        grid=(bsz, nsteps),
        in_specs=in_specs, out_specs=out_specs, out_shape=out_shape,
        scratch_shapes=[pltpu.VMEM((HG_HEADS, HG_DV, HG_DK), F32)],
        compiler_params=_cparams(("parallel", "arbitrary")),
        name=f"hgrn2_{mode}_{'rev' if rev else 'fwd'}",
    )(*args)


def _first_argmax(vals):
    best, idx = vals[0], jnp.zeros(vals[0].shape, jnp.int32)
    for j in range(1, len(vals)):
        better = vals[j] > best
        idx = jnp.where(better, j, idx)
        best = jnp.where(better, vals[j], best)
    return idx, best


def _pick(idx, vals):
    out = vals[-1]
    for j in range(len(vals) - 2, -1, -1):
        out = jnp.where(idx == j, vals[j], out)
    return out


def _store_token_tiles(ref, v):
    n = v.shape[0]
    for j in range(SUBLANES):
        ref[pl.ds(j, n, stride=SUBLANES), :] = v[:, j * LANES:(j + 1) * LANES]


def _load_token_tiles(ref):
    n = ref.shape[0] // SUBLANES
    return jnp.concatenate([ref[pl.ds(j, n, stride=SUBLANES), :] for j in range(SUBLANES)], axis=1)


def _token_tile(ref, idx):
    return ref.at[pl.ds(pl.multiple_of(idx * SUBLANES, SUBLANES), SUBLANES)]


def _router_kernel(x_ref, g_ref, sh_ref, sc_ref, rw_ref, rb_ref, h_ref, slot_ref, wt_ref, cnt_ref, carry_ref):
    step = pl.program_id(0)

    @pl.when(step == 0)
    def _():
        carry_ref[...] = jnp.zeros_like(carry_ref)

    h = _normmod(x_ref[...], g_ref[...], sh_ref[0], sc_ref[0])
    _store_token_tiles(h_ref, h)
    scores = jax.nn.sigmoid(_dot_f32(rw_ref[...], h, _NT))
    sel = scores + rb_ref[...]
    srow = [sel[e:e + 1, :] for e in range(N_EXPERTS)]
    prow = [scores[e:e + 1, :] for e in range(N_EXPERTS)]
    gscore = []
    for gi in range(N_EXPERT_GROUPS):
        m = srow[gi * EXPERTS_PER_GROUP:(gi + 1) * EXPERTS_PER_GROUP]
        pair_sums = [m[i] + m[j] for i in range(EXPERTS_PER_GROUP) for j in range(i + 1, EXPERTS_PER_GROUP)]
        best = pair_sums[0]
        for p in pair_sums[1:]:
            best = jnp.maximum(best, p)
        gscore.append(best)
    gidx, _ = _first_argmax(gscore)
    in_sel = [_pick(gidx, [srow[gi * EXPERTS_PER_GROUP + j] for gi in range(N_EXPERT_GROUPS)])
              for j in range(EXPERTS_PER_GROUP)]
    in_p = [_pick(gidx, [prow[gi * EXPERTS_PER_GROUP + j] for gi in range(N_EXPERT_GROUPS)])
            for j in range(EXPERTS_PER_GROUP)]
    i1, _ = _first_argmax(in_sel)
    i2, _ = _first_argmax([jnp.where(i1 == j, -jnp.inf, in_sel[j]) for j in range(EXPERTS_PER_GROUP)])
    w1, w2 = _pick(i1, in_p), _pick(i2, in_p)
    den = w1 + w2
    e1, e2 = gidx * EXPERTS_PER_GROUP + i1, gidx * EXPERTS_PER_GROUP + i2
    tl = h.shape[0]
    erow = lax.broadcasted_iota(jnp.int32, (N_EXPERTS, tl), 0)
    oh1, oh2 = erow == e1, erow == e2
    cnt = (oh1 | oh2).astype(F32)
    r = lax.broadcasted_iota(jnp.int32, (tl, tl), 0)
    c = lax.broadcasted_iota(jnp.int32, (tl, tl), 1)
    before = jnp.dot(cnt.astype(BF16), (r < c).astype(BF16), preferred_element_type=F32)
    base = carry_ref[:, 0:1] + before
    rank1 = jnp.sum(jnp.where(oh1, base, 0.0), axis=0, keepdims=True).astype(jnp.int32)
    rank2 = jnp.sum(jnp.where(oh2, base, 0.0), axis=0, keepdims=True).astype(jnp.int32)
    carry_ref[...] = carry_ref[...] + jnp.sum(cnt, axis=1, keepdims=True)
    cnt_ref[...] = carry_ref[...].astype(jnp.int32)
    srow = lax.broadcasted_iota(jnp.int32, (SUBLANES, tl), 0)
    slot_ref[0] = jnp.where(srow == 0, e1, jnp.where(srow == 1, e2, jnp.where(srow == 2, rank1, rank2)))
    wrow = lax.broadcasted_iota(jnp.int32, (LANES, tl), 0)
    wt_ref[...] = jnp.where(wrow == 0, w1 / den, jnp.where(wrow == 1, w2 / den, 0.0)).T


def _ffn_norm_router(x, g, shift, scale, router_w, router_b, seq):
    t, d = x.shape
    tl = min(seq, MOE_TL)
    nblk = t // tl
    nb = shift.shape[0]
    mod_spec = pl.BlockSpec((1, 1, d), (lambda i: (i * tl // seq, 0, 0)) if nb > 1 else (lambda i: (0, 0, 0)))
    return pl.pallas_call(
        _router_kernel,
        grid=(nblk,),
        in_specs=[pl.BlockSpec((tl, d), lambda i: (i, 0)),
                  pl.BlockSpec((1, d), lambda i: (0, 0)),
                  mod_spec, mod_spec,
                  pl.BlockSpec((N_EXPERTS, d), lambda i: (0, 0)),
                  pl.BlockSpec((N_EXPERTS, 1), lambda i: (0, 0))],
        out_specs=[pl.BlockSpec((tl * SUBLANES, LANES), lambda i: (i, 0)),
                   pl.BlockSpec((1, SUBLANES, tl), lambda i: (i, 0, 0)),
                   pl.BlockSpec((tl, LANES), lambda i: (i, 0)),
                   pl.BlockSpec((N_EXPERTS, LANES), lambda i: (0, 0))],
        out_shape=[jax.ShapeDtypeStruct((t * SUBLANES, LANES), F32),
                   jax.ShapeDtypeStruct((nblk, SUBLANES, tl), jnp.int32),
                   jax.ShapeDtypeStruct((t, LANES), F32),
                   jax.ShapeDtypeStruct((N_EXPERTS, LANES), jnp.int32)],
        scratch_shapes=[pltpu.VMEM((N_EXPERTS, LANES), F32)],
        compiler_params=_cparams(("arbitrary",)),
        name="ffn_norm_router",
    )(x, g.reshape(1, d), shift, scale, router_w.T, router_b.reshape(N_EXPERTS, 1))


DMA_UNROLL = 8


def _dispatch_kernel(slot_ref, ends_ref, h_ref, hs_ref, zero_ref, sem, *, tl, max_tiles):
    @pl.when(pl.program_id(0) == 0)
    def _():
        zero_ref[...] = jnp.zeros_like(zero_ref)
        fills = []
        for e in range(N_EXPERTS):
            end = ends_ref[e]
            gap = (MOE_TM - (end & (MOE_TM - 1))) & (MOE_TM - 1)
            for bit in range(MOE_TM.bit_length() - 1):
                size = 1 << bit
                first_row = pl.multiple_of((end + (gap & (size - 1))) * SUBLANES, SUBLANES)
                fills.append(((gap & size) != 0,
                              pltpu.make_async_copy(zero_ref.at[pl.ds(0, size * SUBLANES)],
                                                    hs_ref.at[pl.ds(first_row, size * SUBLANES)], sem)))
        for n in range(N_EXPERTS):
            first_slot = ends_ref[N_EXPERTS] + n * MOE_TM
            first_row = pl.multiple_of(first_slot * SUBLANES, SUBLANES)
            fills.append((first_slot < max_tiles * MOE_TM,
                          pltpu.make_async_copy(zero_ref, hs_ref.at[pl.ds(first_row, MOE_TM * SUBLANES)], sem)))
        for cond, cp in fills:
            pl.when(cond)(cp.start)
        for cond, cp in fills:
            pl.when(cond)(cp.wait)

    def issue(t, carry):
        pltpu.make_async_copy(_token_tile(h_ref, t), _token_tile(hs_ref, slot_ref[0, 0, t]), sem).start()
        pltpu.make_async_copy(_token_tile(h_ref, t), _token_tile(hs_ref, slot_ref[0, 0, tl + t]), sem).start(priority=1)
        return carry

    lax.fori_loop(0, tl, issue, 0, unroll=DMA_UNROLL)
    for _ in range(2):
        pltpu.make_async_copy(h_ref, hs_ref.at[pl.ds(0, tl * SUBLANES)], sem).wait()


def _dispatch(h, slot2, ends, tl, max_tiles):
    t = h.shape[0] // SUBLANES
    nblk = t // tl
    return pl.pallas_call(
        functools.partial(_dispatch_kernel, tl=tl, max_tiles=max_tiles),
        grid=(nblk,),
        in_specs=[pl.BlockSpec((1, 1, 2 * tl), lambda i: (i, 0, 0), memory_space=pltpu.SMEM),
                  pl.BlockSpec(memory_space=pltpu.SMEM),
                  pl.BlockSpec((tl * SUBLANES, LANES), lambda i: (i, 0))],
        out_specs=pl.BlockSpec(memory_space=pl.ANY),
        out_shape=jax.ShapeDtypeStruct((max_tiles * MOE_TM * SUBLANES, LANES), F32),
        scratch_shapes=[pltpu.VMEM((MOE_TM * SUBLANES, LANES), F32), pltpu.SemaphoreType.DMA(())],
        compiler_params=_cparams(("arbitrary",)),
        name="moe_dispatch",
    )(slot2, ends, h)


def _expert_kernel(te_ref, tf_ref, nt_ref, hs_ref, wg_ref, wu_ref, wd_ref, ys_ref, wgb_ref, wub_ref, wdb_ref):
    n = pl.program_id(0)

    @pl.when(n >= nt_ref[0])
    def _():
        ys_ref[...] = jnp.zeros_like(ys_ref)

    @pl.when(n < nt_ref[0])
    def _():
        @pl.when(tf_ref[n] == 1)
        def _():
            wgb_ref[...] = wg_ref[0].astype(BF16)
            wub_ref[...] = wu_ref[0].astype(BF16)
            wdb_ref[...] = wd_ref[0].astype(BF16)

        h = _load_token_tiles(hs_ref).astype(BF16)
        a = jnp.dot(h, wgb_ref[...], preferred_element_type=F32)
        b = jnp.dot(h, wub_ref[...], preferred_element_type=F32)
        y = jnp.dot((_silu(a) * b).astype(BF16), wdb_ref[...], preferred_element_type=F32)
        _store_token_tiles(ys_ref, y)


def _experts(hs, tile_expert, tile_first, n_tiles, w_gate, w_up, w_down, layer, max_tiles):
    d = w_gate.shape[2]
    tile_spec = pl.BlockSpec((MOE_TM * SUBLANES, LANES), lambda n, te, tf, nt: (n, 0))
    grid_spec = pltpu.PrefetchScalarGridSpec(
        num_scalar_prefetch=3,
        grid=(max_tiles,),
        in_specs=[tile_spec,
                  pl.BlockSpec((None, 1, d, D_EXPERT), lambda n, te, tf, nt: (layer, te[n], 0, 0)),
                  pl.BlockSpec((None, 1, d, D_EXPERT), lambda n, te, tf, nt: (layer, te[n], 0, 0)),
                  pl.BlockSpec((None, 1, D_EXPERT, d), lambda n, te, tf, nt: (layer, te[n], 0, 0))],
        out_specs=tile_spec,
        scratch_shapes=[pltpu.VMEM((d, D_EXPERT), BF16), pltpu.VMEM((d, D_EXPERT), BF16),
                        pltpu.VMEM((D_EXPERT, d), BF16)])
    return pl.pallas_call(
        _expert_kernel,
        grid_spec=grid_spec,
        out_shape=jax.ShapeDtypeStruct((max_tiles * MOE_TM * SUBLANES, LANES), F32),
        compiler_params=_cparams(("arbitrary",)),
        name="moe_experts",
    )(tile_expert, tile_first, n_tiles, hs, w_gate, w_up, w_down)


def _combine_kernel(slot_ref, nslot_ref, wt_ref, res_ref, gate_ref, fg_ref, ys_ref, o_ref, y1_ref, y2_ref, sems, *,
                    tl, final):
    step, nsteps = pl.program_id(0), pl.num_programs(0)
    cur = step % 2

    def start_gather(sref, buf):
        def issue(t, carry):
            pltpu.make_async_copy(_token_tile(ys_ref, sref[0, 0, t]), _token_tile(y1_ref.at[buf], t),
                                  sems.at[buf]).start()
            pltpu.make_async_copy(_token_tile(ys_ref, sref[0, 0, tl + t]), _token_tile(y2_ref.at[buf], t),
                                  sems.at[buf]).start(priority=1)
            return carry

        lax.fori_loop(0, tl, issue, 0, unroll=DMA_UNROLL)

    @pl.when(step == 0)
    def _():
        start_gather(slot_ref, 0)

    @pl.when(step + 1 < nsteps)
    def _():
        start_gather(nslot_ref, 1 - cur)

    pltpu.make_async_copy(ys_ref.at[pl.ds(0, tl * SUBLANES)], y1_ref.at[cur], sems.at[cur]).wait()
    pltpu.make_async_copy(ys_ref.at[pl.ds(0, tl * SUBLANES)], y2_ref.at[cur], sems.at[cur]).wait()
    wt = wt_ref[...]
    moe = wt[:, 0:1] * _load_token_tiles(y1_ref.at[cur]) + wt[:, 1:2] * _load_token_tiles(y2_ref.at[cur])
    x = res_ref[...] + gate_ref[0] * moe
    if final:
        ms = jnp.mean(x * x, axis=-1, keepdims=True)
        x = x * lax.rsqrt(ms + EPS) * fg_ref[...]
    o_ref[...] = x


def _combine(ys, slot2, wt, res, gate, final_g, seq, tl):
    t, d = res.shape
    nblk = t // tl
    nb = gate.shape[0]
    final = final_g is not None
    fg = final_g.reshape(1, d) if final else jnp.ones((1, d), F32)
    return pl.pallas_call(
        functools.partial(_combine_kernel, tl=tl, final=final),
        grid=(nblk,),
        in_specs=[pl.BlockSpec((1, 1, 2 * tl), lambda i: (i, 0, 0), memory_space=pltpu.SMEM),
                  pl.BlockSpec((1, 1, 2 * tl), lambda i: (jnp.minimum(i + 1, nblk - 1), 0, 0),
                               memory_space=pltpu.SMEM),
                  pl.BlockSpec((tl, LANES), lambda i: (i, 0)),
                  pl.BlockSpec((tl, d), lambda i: (i, 0)),
                  pl.BlockSpec((1, 1, d), (lambda i: (i * tl // seq, 0, 0)) if nb > 1 else (lambda i: (0, 0, 0))),
                  pl.BlockSpec((1, d), lambda i: (0, 0)),
                  pl.BlockSpec(memory_space=pl.ANY)],
        out_specs=pl.BlockSpec((tl, d), lambda i: (i, 0)),
        out_shape=jax.ShapeDtypeStruct((t, d), F32),
        scratch_shapes=[pltpu.VMEM((2, tl * SUBLANES, LANES), F32), pltpu.VMEM((2, tl * SUBLANES, LANES), F32),
                        pltpu.SemaphoreType.DMA((2,))],
        compiler_params=_cparams(("arbitrary",)),
        name="moe_combine",
    )(slot2, slot2, wt, res, gate, fg, ys)


def _even_layer_mixer(xs, mods, norm_g, w_in, conv_w, conv_b, dt_bias, a_log, d_skip, ssd_g,
                      cf_w, cf_b, cf_lng, cf_lnb, w_out, rows):
    s_lo = SSD_INNER
    dt_lo = SSD_INNER + SSD_CONV_DIM
    cf_lo = dt_lo + 2 * SSD_HEADS
    w_cf = _realign_cols(w_in, cf_lo, 2 * CF_CH)
    bsz = xs[0].shape[0]
    zero_state = jnp.zeros((bsz, SSD_GROUPS, SSD_STATE, SSD_GROUP_W), F32)
    proj = []
    for x, m in zip(xs, mods):
        h = _norm_mod(x, norm_g, m[0], m[1])
        proj.append(dict(z=_matmul(h, w_in, BF16, 0, s_lo), xbc=_matmul(h, w_in, BF16, s_lo, SSD_CONV_DIM),
                         dt=_matmul(h, w_in, F32, dt_lo, LANES), cf=_matmul(h, w_cf, BF16)))
    y_fwd, act, o_ssd = [None, None], [None, None], [None, None]
    state = zero_state
    for si in range(2):
        p = proj[si]
        y_fwd[si], act[si], state = _ssd_scan(p["xbc"], p["dt"], dt_bias[0], a_log[0], state, False,
                                              conv_w=conv_w, conv_b=conv_b, d_skip=d_skip)
    state = zero_state
    for si in range(2):
        p = proj[si]
        o_ssd[si], state = _ssd_scan(act[si], p["dt"], dt_bias[1], a_log[1], state, True,
                                     y_in=y_fwd[si], z=p["z"], norm_g=ssd_g)
    outs = []
    for si, (x, m) in enumerate(zip(xs, mods)):
        slen = x.shape[1] if si == 0 else GRID_W
        o_cf = _conformer(proj[si]["cf"], cf_w, cf_b, cf_lng, cf_lnb, slen)
        outs.append(_matmul_res2(o_ssd[si], o_cf, w_out, x, m[2]))
    return outs


def _odd_layer_mixer(xc, xl, m_c, m_l, norm_g, w_in, lb, hg_g, w_out):
    st0 = HG_KEY + HG_VAL
    f0 = st0 + HG_VAL
    bsz = xl.shape[0]
    h_c = _norm_mod(xc, norm_g, m_c[0], m_c[1])
    h_l = _norm_mod_colmajor(xl, norm_g, m_l[0], m_l[1])
    v_c, f_c = _matmul(h_c, w_in, BF16, st0, HG_VAL), _matmul(h_c, w_in, F32, f0, 2 * HG_KEY)
    q_l, g_l = _matmul(h_l, w_in, BF16, 0, HG_KEY), _matmul(h_l, w_in, BF16, HG_KEY, HG_VAL)
    v_l, f_l = _matmul(h_l, w_in, BF16, st0, HG_VAL), _matmul(h_l, w_in, F32, f0, 2 * HG_KEY)
    zero_state = jnp.zeros((bsz, HG_HEADS, HG_DV, HG_DK), F32)
    s_f = _hgrn2_scan(v_c, f_c, lb, zero_state, False, "state")
    s_r = _hgrn2_scan(v_c, f_c, lb, zero_state, True, "state")
    o_f = _hgrn2_scan(v_l, f_l, lb, s_f, False, "out", q=q_l)
    o = _hgrn2_scan(v_l, f_l, lb, s_r, True, "readout", q=q_l, o_in=o_f, g=g_l, norm_g=hg_g)
    return _matmul_res_colmajor(o, w_out, xl, m_l[2])


def _dispatch_layout(counts, slot, max_tiles):
    tiles = (counts + MOE_TM - 1) // MOE_TM
    tile_ends = jnp.cumsum(tiles)
    tile_starts = tile_ends - tiles
    total = tile_ends[-1]
    starts = tile_starts * MOE_TM
    ends = jnp.concatenate([starts + counts, (total * MOE_TM).reshape(1)]).astype(jnp.int32)
    experts = jnp.arange(N_EXPERTS, dtype=jnp.int32)

    def lookup(table, idx):
        return jnp.sum(jnp.where(idx[..., None] == experts, table, 0), axis=-1)

    slots = jnp.concatenate([lookup(starts, slot[:, 0]) + slot[:, 2], lookup(starts, slot[:, 1]) + slot[:, 3]],
                            axis=-1)
    n = jnp.arange(max_tiles, dtype=jnp.int32)
    nn = jnp.minimum(n, total - 1)
    expert = jnp.sum((nn[:, None] >= tile_ends[None, :]).astype(jnp.int32), axis=1)
    is_first = ((nn == lookup(tile_starts, expert)) & (n < total)).astype(jnp.int32)
    return slots[:, None, :].astype(jnp.int32), ends, expert, is_first, total.reshape(1).astype(jnp.int32)


def _moe_block(x, m, norm_g, router_w, router_b, w_gate, w_up, w_down, layer, final_g=None):
    bsz, seq, d = x.shape
    t = bsz * seq
    xf = x.reshape(t, d)
    tl = min(seq, MOE_TL)
    h, slot, wt, cnt = _ffn_norm_router(xf, norm_g, m[3], m[4], router_w, router_b, seq)
    max_tiles = 2 * t // MOE_TM + N_EXPERTS
    slot2, ends, tile_expert, tile_first, n_tiles = _dispatch_layout(cnt[:, 0], slot, max_tiles)
    hs = _dispatch(h, slot2, ends, tl, max_tiles)
    ys = _experts(hs, tile_expert, tile_first, n_tiles, w_gate, w_up, w_down, layer, max_tiles)
    return _combine(ys, slot2, wt, xf, m[5], final_g, seq, tl).reshape(bsz, seq, d)


def kernel(x, c, ctx, c_ctx, mod_w, mod_b, norm_mix_g, norm_ffn_g, router_w, router_b, moe_w_gate, moe_w_up,
           moe_w_down, ab_w_in, ssd_conv_w, ssd_conv_b, ssd_dt_bias, ssd_a_log, ssd_d, ssd_norm_g, cf_dw_w,
           cf_dw_b, cf_ln_g, cf_ln_b, ab_w_out, hg_w_in, hg_lb, hg_norm_g, hg_w_out, final_norm_g):
    depth = mod_w.shape[0]
    assert depth == 2, "layer schedule below is written for one even and one odd layer"
    bsz, seq, d = x.shape
    rows = seq // GRID_W
    lb_all = jnp.cumsum(jax.nn.softmax(hg_lb.astype(F32), axis=0), axis=0)
    lb_all = lb_all - lb_all[0]

    nrow = -(-(bsz + 1) // SUBLANES) * SUBLANES
    cond = jnp.zeros((nrow, d), F32).at[:bsz].set(c).at[bsz].set(c_ctx)
    mod = _modulation(cond, mod_w, mod_b).reshape(depth, nrow, N_MOD, 1, d)

    def mods(l):
        m_l = [mod[l, :bsz, k] for k in range(N_MOD)]
        m_c = [mod[l, bsz:bsz + 1, k] for k in range(N_MOD)]
        return m_c, m_l

    m_c, m_l = mods(0)
    xc, xl = _even_layer_mixer((ctx, x), (m_c, m_l), norm_mix_g[0], ab_w_in[0], ssd_conv_w[0], ssd_conv_b[0],
                               ssd_dt_bias[0], ssd_a_log[0], ssd_d[0], ssd_norm_g[0], cf_dw_w[0], cf_dw_b[0],
                               cf_ln_g[0], cf_ln_b[0], ab_w_out[0], rows)
    xl = _moe_block(xl, m_l, norm_ffn_g[0], router_w, router_b, moe_w_gate, moe_w_up, moe_w_down, 0)
    xc = _moe_block(xc, m_c, norm_ffn_g[0], router_w, router_b, moe_w_gate, moe_w_up, moe_w_down, 0)

    m_c, m_l = mods(1)
    xl = _odd_layer_mixer(xc, xl, m_c, m_l, norm_mix_g[1], hg_w_in[0], lb_all[1], hg_norm_g[0], hg_w_out[0])
    return _moe_block(xl, m_l, norm_ffn_g[1], router_w, router_b, moe_w_gate, moe_w_up, moe_w_down, 1,
                      final_g=final_norm_g)
```

```python
import functools

import jax
import jax.numpy as jnp
from jax import lax
from jax.experimental import pallas as pl
from jax.experimental.pallas import tpu as pltpu

F32 = jnp.float32
BF16 = jnp.bfloat16

D_MODEL = 1024
GRID_W = 64
EPS = 1e-6
N_MOD = 6

SSD_HEADS = 16
SSD_HEAD_DIM = 64
SSD_INNER = SSD_HEADS * SSD_HEAD_DIM
SSD_GROUPS = 4
SSD_STATE = 128
SSD_CONV = 5
SSD_CHUNK = 128
SSD_BC = SSD_GROUPS * SSD_STATE
SSD_CONV_DIM = SSD_INNER + 2 * SSD_BC
SSD_GROUP_W = SSD_INNER // SSD_GROUPS

CF_CH = 1024
CF_KERNEL = 31
CF_PAD = 16

HG_HEADS = 8
HG_DK = 128
HG_DV = 128
HG_KEY = HG_HEADS * HG_DK
HG_VAL = HG_HEADS * HG_DV
HG_CHUNK = 64
HG_STEP = 512

N_EXPERTS = 16
N_EXPERT_GROUPS = 4
EXPERTS_PER_GROUP = 4
D_EXPERT = 512
MOE_TL = 512
MOE_TM = 512

LANES = 128
SUBLANES = 8
HALO = 16
CM_COLS = 16
VMEM_LIMIT = 48 * 1024 * 1024


def _cparams(sem):
    return pltpu.CompilerParams(dimension_semantics=sem, vmem_limit_bytes=VMEM_LIMIT)


def _silu(x):
    return x * jax.nn.sigmoid(x)


def _split3(v):
    hi = v.astype(BF16)
    r1 = v - hi.astype(F32)
    mid = r1.astype(BF16)
    lo = (r1 - mid.astype(F32)).astype(BF16)
    return hi, mid, lo


def _dot01(m01, v):
    hi, mid, lo = _split3(v)
    out = jnp.dot(m01, lo, preferred_element_type=F32)
    out = out + jnp.dot(m01, mid, preferred_element_type=F32)
    return out + jnp.dot(m01, hi, preferred_element_type=F32)


def _dot_v01(v, m01):
    hi, mid, lo = _split3(v)
    out = jnp.dot(lo, m01, preferred_element_type=F32)
    out = out + jnp.dot(mid, m01, preferred_element_type=F32)
    return out + jnp.dot(hi, m01, preferred_element_type=F32)


_NN = (((1,), (0,)), ((), ()))
_NT = (((1,), (1,)), ((), ()))
_TN = (((0,), (0,)), ((), ()))


def _dot_f32(a, b, dn):
    a1, a2, a3 = _split3(a)
    b1, b2, b3 = _split3(b)
    out = lax.dot_general(a3, b1, dn, preferred_element_type=F32)
    out = out + lax.dot_general(a1, b3, dn, preferred_element_type=F32)
    out = out + lax.dot_general(a2, b2, dn, preferred_element_type=F32)
    out = out + lax.dot_general(a2, b1, dn, preferred_element_type=F32)
    out = out + lax.dot_general(a1, b2, dn, preferred_element_type=F32)
    return out + lax.dot_general(a1, b1, dn, preferred_element_type=F32)


def _scan_tri(n, rev):
    r = lax.broadcasted_iota(jnp.int32, (n, n), 0)
    c = lax.broadcasted_iota(jnp.int32, (n, n), 1)
    return ((c >= r) if rev else (c <= r))


def _mod_kernel(c_ref, w_ref, b_ref, o_ref):
    cond = _silu(c_ref[...])
    o_ref[0] = _dot_f32(cond, w_ref[0], _NN) + b_ref[0]


def _modulation(cond, mod_w, mod_b):
    depth, d, n = mod_w.shape
    r = cond.shape[0]
    tn = 512
    return pl.pallas_call(
        _mod_kernel,
        grid=(depth, n // tn),
        in_specs=[pl.BlockSpec((r, d), lambda l, j: (0, 0)),
                  pl.BlockSpec((1, d, tn), lambda l, j: (l, 0, j)),
                  pl.BlockSpec((1, 1, tn), lambda l, j: (l, 0, j))],
        out_specs=pl.BlockSpec((1, r, tn), lambda l, j: (l, 0, j)),
        out_shape=jax.ShapeDtypeStruct((depth, r, n), F32),
        compiler_params=_cparams(("parallel", "parallel")),
        name="modulation",
    )(cond, mod_w, mod_b.reshape(depth, 1, n))


def _normmod(x, g, shift, scale):
    ms = jnp.mean(x * x, axis=-1, keepdims=True)
    return (x * lax.rsqrt(ms + EPS) * g) * (1.0 + scale) + shift


def _normmod_kernel(x_ref, g_ref, sh_ref, sc_ref, o_ref):
    o_ref[0] = _normmod(x_ref[0], g_ref[...], sh_ref[0], sc_ref[0]).astype(o_ref.dtype)


def _mod_spec(nb):
    return pl.BlockSpec((1, 1, D_MODEL), (lambda b, *_: (b, 0, 0)) if nb > 1 else (lambda b, *_: (0, 0, 0)))


def _norm_mod(x, g, shift, scale):
    bsz, seq, d = x.shape
    tl = min(seq, 512)
    return pl.pallas_call(
        _normmod_kernel,
        grid=(bsz, seq // tl),
        in_specs=[pl.BlockSpec((1, tl, d), lambda b, i: (b, i, 0)),
                  pl.BlockSpec((1, d), lambda b, i: (0, 0)),
                  _mod_spec(shift.shape[0]), _mod_spec(scale.shape[0])],
        out_specs=pl.BlockSpec((1, tl, d), lambda b, i: (b, i, 0)),
        out_shape=jax.ShapeDtypeStruct((bsz, seq, d), BF16),
        compiler_params=_cparams(("parallel", "parallel")),
        name="norm_mod",
    )(x, g.reshape(1, d), shift, scale)


def _normmod_cm_kernel(x_ref, g_ref, sh_ref, sc_ref, o_ref, h_ref, *, rows):
    h = _normmod(x_ref[0], g_ref[...], sh_ref[0], sc_ref[0])
    nlt = h.shape[-1] // LANES
    for r in range(rows):
        for j in range(nlt):
            h_ref[j, r * CM_COLS:(r + 1) * CM_COLS, :] = h[r, :, j * LANES:(j + 1) * LANES]
    for c in range(CM_COLS):
        for j in range(nlt):
            piece = h_ref[j, pl.ds(c, rows, stride=CM_COLS), :]
            o_ref[0, c * rows:(c + 1) * rows, j * LANES:(j + 1) * LANES] = piece.astype(o_ref.dtype)


def _norm_mod_colmajor(x, g, shift, scale):
    bsz, seq, d = x.shape
    rows = seq // GRID_W
    x4 = x.reshape(bsz, rows, GRID_W, d)
    return pl.pallas_call(
        functools.partial(_normmod_cm_kernel, rows=rows),
        grid=(bsz, GRID_W // CM_COLS),
        in_specs=[pl.BlockSpec((1, rows, CM_COLS, d), lambda b, i: (b, 0, i, 0)),
                  pl.BlockSpec((1, d), lambda b, i: (0, 0)),
                  _mod_spec(shift.shape[0]), _mod_spec(scale.shape[0])],
        out_specs=pl.BlockSpec((1, CM_COLS * rows, d), lambda b, i: (b, i, 0)),
        out_shape=jax.ShapeDtypeStruct((bsz, seq, d), BF16),
        scratch_shapes=[pltpu.VMEM((d // LANES, CM_COLS * rows, LANES), F32)],
        compiler_params=_cparams(("parallel", "parallel")),
        name="norm_mod_colmajor",
    )(x4, g.reshape(1, d), shift, scale)


MM_SEMANTICS = ("arbitrary", "arbitrary", "arbitrary")


def _stage_weights(b_ref, wb_ref):
    @pl.when((pl.program_id(1) == 0) & (pl.program_id(2) == 0))
    def _():
        wb_ref[...] = b_ref[...].astype(BF16)


def _mm_kernel(a_ref, b_ref, o_ref, wb_ref):
    _stage_weights(b_ref, wb_ref)
    o_ref[0] = jnp.dot(a_ref[0], wb_ref[...], preferred_element_type=F32).astype(o_ref.dtype)


def _mm_res2_kernel(a1_ref, a2_ref, b1_ref, b2_ref, res_ref, gate_ref, o_ref, wb1_ref, wb2_ref):
    _stage_weights(b1_ref, wb1_ref)
    _stage_weights(b2_ref, wb2_ref)
    y = jnp.dot(a1_ref[0], wb1_ref[...], preferred_element_type=F32)
    y = y + jnp.dot(a2_ref[0], wb2_ref[...], preferred_element_type=F32)
    o_ref[0] = res_ref[0] + gate_ref[0] * y


def _mm_res_kernel(a_ref, b_ref, res_ref, gate_ref, o_ref, wb_ref):
    _stage_weights(b_ref, wb_ref)
    o_ref[0] = res_ref[0] + gate_ref[0] * jnp.dot(a_ref[0], wb_ref[...], preferred_element_type=F32)


def _matmul(a, w, out_dtype, col0=0, ncols=None):
    bsz, seq, k = a.shape
    n = w.shape[1] - col0 if ncols is None else ncols
    tm, tn = min(seq, 1024), min(n, 1024)
    assert col0 % tn == 0 and n % tn == 0
    jb = col0 // tn
    return pl.pallas_call(
        _mm_kernel,
        grid=(n // tn, bsz, seq // tm),
        in_specs=[pl.BlockSpec((1, tm, k), lambda j, b, i: (b, i, 0)),
                  pl.BlockSpec((k, tn), lambda j, b, i: (0, j + jb))],
        out_specs=pl.BlockSpec((1, tm, tn), lambda j, b, i: (b, i, j)),
        out_shape=jax.ShapeDtypeStruct((bsz, seq, n), out_dtype),
        scratch_shapes=[pltpu.VMEM((k, tn), BF16)],
        compiler_params=_cparams(MM_SEMANTICS),
        name="matmul",
    )(a, w)


def _matmul_res2(a1, a2, w, res, gate):
    bsz, seq, kh = a1.shape
    n = w.shape[1]
    tm, tn = min(seq, 1024), min(n, 512)
    nb = gate.shape[0]
    return pl.pallas_call(
        _mm_res2_kernel,
        grid=(n // tn, bsz, seq // tm),
        in_specs=[pl.BlockSpec((1, tm, kh), lambda j, b, i: (b, i, 0)),
                  pl.BlockSpec((1, tm, kh), lambda j, b, i: (b, i, 0)),
                  pl.BlockSpec((kh, tn), lambda j, b, i: (0, j)),
                  pl.BlockSpec((kh, tn), lambda j, b, i: (1, j)),
                  pl.BlockSpec((1, tm, tn), lambda j, b, i: (b, i, j)),
                  pl.BlockSpec((1, 1, tn), (lambda j, b, i: (b, 0, j)) if nb > 1 else (lambda j, b, i: (0, 0, j)))],
        out_specs=pl.BlockSpec((1, tm, tn), lambda j, b, i: (b, i, j)),
        out_shape=jax.ShapeDtypeStruct((bsz, seq, n), F32),
        scratch_shapes=[pltpu.VMEM((kh, tn), BF16), pltpu.VMEM((kh, tn), BF16)],
        compiler_params=_cparams(MM_SEMANTICS),
        name="matmul_res",
    )(a1, a2, w, w, res, gate)


def _realign_kernel(w_ref, o_ref, *, col0):
    o_ref[...] = w_ref[:, col0:col0 + o_ref.shape[1]].astype(o_ref.dtype)


def _realign_cols(w, col0, ncols):
    k, n = w.shape
    tk = 128
    return pl.pallas_call(
        functools.partial(_realign_kernel, col0=col0),
        grid=(k // tk,),
        in_specs=[pl.BlockSpec((tk, n), lambda i: (i, 0))],
        out_specs=pl.BlockSpec((tk, ncols), lambda i: (i, 0)),
        out_shape=jax.ShapeDtypeStruct((k, ncols), BF16),
        compiler_params=_cparams(("parallel",)),
        name="realign_cols",
    )(w)


def _matmul_res(a, w, res, gate):
    bsz, seq, k = a.shape
    n = w.shape[1]
    tm, tn = min(seq, 1024), min(n, 512)
    return pl.pallas_call(
        _mm_res_kernel,
        grid=(n // tn, bsz, seq // tm),
        in_specs=[pl.BlockSpec((1, tm, k), lambda j, b, i: (b, i, 0)),
                  pl.BlockSpec((k, tn), lambda j, b, i: (0, j)),
                  pl.BlockSpec((1, tm, tn), lambda j, b, i: (b, i, j)),
                  pl.BlockSpec((1, 1, tn), lambda j, b, i: (b, 0, j))],
        out_specs=pl.BlockSpec((1, tm, tn), lambda j, b, i: (b, i, j)),
        out_shape=jax.ShapeDtypeStruct((bsz, seq, n), F32),
        scratch_shapes=[pltpu.VMEM((k, tn), BF16)],
        compiler_params=_cparams(MM_SEMANTICS),
        name="matmul_res1",
    )(a, w, res, gate)


def _ssd_kernel(*refs, rev, nchunks, readout):
    if readout:
        act_ref, dt_ref, dtb_ref, alog_ref, s0_ref, yin_ref, z_ref, ng_ref, y_ref, sfin_ref, st_ref = refs
    else:
        (xbc_ref, prev_ref, next_ref, dt_ref, cw_ref, cb_ref, dtb_ref, alog_ref, dsk_ref, s0_ref,
         y_ref, act_ref, sfin_ref, pad_ref, st_ref) = refs
    step = pl.program_id(1)
    u = (nchunks - 1 - step) if rev else step
    ck = SSD_CHUNK
    hoff = SSD_HEADS if rev else 0

    @pl.when(step == 0)
    def _():
        st_ref[...] = s0_ref[0]

    if readout:
        xbc = act_ref[0].astype(F32)
    else:
        pad_ref[HALO:HALO + ck, :] = xbc_ref[0].astype(F32)
        pad_ref[0:HALO, :] = jnp.where(u > 0, prev_ref[0].astype(F32), 0.0)
        pad_ref[HALO + ck:2 * HALO + ck, :] = jnp.where(u < nchunks - 1, next_ref[0].astype(F32), 0.0)
        half = SSD_CONV // 2
        acc = cb_ref[...] + cw_ref[0:1, :] * pad_ref[pl.ds(HALO - half, ck), :]
        for k in range(1, SSD_CONV):
            acc = acc + cw_ref[k:k + 1, :] * pad_ref[pl.ds(HALO - half + k, ck), :]
        xbc = _silu(acc)
        act_ref[0] = xbc.astype(act_ref.dtype)
    xs = xbc[:, :SSD_INNER]

    hlane = lax.broadcasted_iota(jnp.int32, (ck, LANES), 1)
    is_head = (hlane >= hoff) & (hlane < hoff + SSD_HEADS)
    dt = jnp.where(is_head, jax.nn.softplus(dt_ref[0] + dtb_ref[...]), 0.0)
    la = dt * (-jnp.exp(alog_ref[...]))
    tri = _scan_tri(ck, rev).astype(BF16)
    acum = _dot01(tri, la)
    last = 0 if rev else ck - 1
    total = acum[last:last + 1, :]
    acum_t = acum.T

    hrow = lax.broadcasted_iota(jnp.int32, (LANES, SSD_INNER), 0)
    hcol = lax.broadcasted_iota(jnp.int32, (LANES, SSD_INNER), 1)
    expand = (hcol // SSD_HEAD_DIM + hoff == hrow).astype(BF16)
    dt_x = _dot_v01(dt, expand)
    ea_x = _dot_v01(jnp.exp(acum), expand)
    te_x = _dot_v01(jnp.exp(total - acum), expand)
    cd_x = _dot_v01(jnp.broadcast_to(jnp.exp(total), (SUBLANES, LANES)), expand)[0:1, :]

    xdt = xs * dt_x
    xdt_b = xdt.astype(BF16)
    xw_b = (xdt * te_x).astype(BF16)

    r = lax.broadcasted_iota(jnp.int32, (ck, ck), 0)
    c = lax.broadcasted_iota(jnp.int32, (ck, ck), 1)
    causal = (c >= r) if rev else (c <= r)
    lane = lax.broadcasted_iota(jnp.int32, (ck, LANES), 1)
    nt = (((1,), (1,)), ((), ()))
    tn = (((0,), (0,)), ((), ()))

    y_parts = []
    for g in range(SSD_GROUPS):
        b_g = xbc[:, SSD_INNER + g * SSD_STATE:SSD_INNER + (g + 1) * SSD_STATE].astype(BF16)
        c_g = xbc[:, SSD_INNER + SSD_BC + g * SSD_STATE:SSD_INNER + SSD_BC + (g + 1) * SSD_STATE].astype(BF16)
        cb = lax.dot_general(c_g, b_g, nt, preferred_element_type=F32)
        lo = g * SSD_GROUP_W
        s_in = st_ref[g]
        y_off = jnp.dot(c_g, s_in.astype(BF16), preferred_element_type=F32) * ea_x[:, lo:lo + SSD_GROUP_W]
        chunk_state = lax.dot_general(b_g, xw_b[:, lo:lo + SSD_GROUP_W], tn, preferred_element_type=F32)
        st_ref[g] = cd_x[:, lo:lo + SSD_GROUP_W] * s_in + chunk_state
        heads_per_group = SSD_HEADS // SSD_GROUPS
        for pair in range(heads_per_group // 2):
            plo = lo + pair * LANES
            xpair = xdt_b[:, plo:plo + LANES]
            y_pair = y_off[:, pair * LANES:(pair + 1) * LANES]
            for half_i in range(2):
                h = hoff + g * heads_per_group + pair * 2 + half_i
                seg = acum[:, h:h + 1] - acum_t[h:h + 1, :]
                m = (cb * jnp.where(causal, jnp.exp(seg), 0.0)).astype(BF16)
                in_half = (lane >= half_i * SSD_HEAD_DIM) & (lane < (half_i + 1) * SSD_HEAD_DIM)
                y_pair = y_pair + jnp.dot(m, jnp.where(in_half, xpair, jnp.zeros_like(xpair)),
                                          preferred_element_type=F32)
            y_parts.append(y_pair)
    y = jnp.concatenate(y_parts, axis=1)

    if readout:
        y = y + yin_ref[0].astype(F32)
        y = y * _silu(z_ref[0].astype(F32))
        ms = jnp.mean(y * y, axis=-1, keepdims=True)
        y_ref[0] = (y * lax.rsqrt(ms + EPS) * ng_ref[...]).astype(y_ref.dtype)
    else:
        y_ref[0] = (y + dsk_ref[...] * xs).astype(y_ref.dtype)

    @pl.when(step == nchunks - 1)
    def _():
        sfin_ref[0] = st_ref[...]


def _ssd_scan(xbc, dt_raw, dt_bias, a_log, s0, rev, conv_w=None, conv_b=None, d_skip=None,
              y_in=None, z=None, norm_g=None):
    bsz, seq, _ = xbc.shape
    nchunks = seq // SSD_CHUNK
    hb = SSD_CHUNK // HALO
    nhalo = seq // HALO
    hoff = SSD_HEADS if rev else 0

    def cidx(i):
        return (nchunks - 1 - i) if rev else i

    def head_lanes(v):
        return jnp.pad(v.reshape(1, -1), ((0, 0), (hoff, LANES - hoff - v.shape[-1])))

    chunk = lambda width: pl.BlockSpec((1, SSD_CHUNK, width), lambda b, i: (b, cidx(i), 0))
    row = lambda width: pl.BlockSpec((1, width), lambda b, i: (0, 0))
    state_spec = pl.BlockSpec((1, SSD_GROUPS, SSD_STATE, SSD_GROUP_W), lambda b, i: (b, 0, 0, 0))
    state_shape = jax.ShapeDtypeStruct((bsz, SSD_GROUPS, SSD_STATE, SSD_GROUP_W), F32)
    y_shape = jax.ShapeDtypeStruct((bsz, seq, SSD_INNER), BF16)
    state_scratch = pltpu.VMEM((SSD_GROUPS, SSD_STATE, SSD_GROUP_W), F32)
    if rev:
        in_specs = [chunk(SSD_CONV_DIM), chunk(LANES), row(LANES), row(LANES), state_spec,
                    chunk(SSD_INNER), chunk(SSD_INNER), row(SSD_INNER)]
        args = [xbc, dt_raw, head_lanes(dt_bias), head_lanes(a_log), s0, y_in, z, norm_g.reshape(1, -1)]
        out_specs, out_shape = [chunk(SSD_INNER), state_spec], [y_shape, state_shape]
        scratch = [state_scratch]
    else:
        in_specs = [
            chunk(SSD_CONV_DIM),
            pl.BlockSpec((1, HALO, SSD_CONV_DIM), lambda b, i: (b, jnp.maximum(i * hb - 1, 0), 0)),
            pl.BlockSpec((1, HALO, SSD_CONV_DIM), lambda b, i: (b, jnp.minimum((i + 1) * hb, nhalo - 1), 0)),
            chunk(LANES),
            pl.BlockSpec((SUBLANES, SSD_CONV_DIM), lambda b, i: (0, 0)), row(SSD_CONV_DIM),
            row(LANES), row(LANES), row(SSD_INNER), state_spec]
        args = [xbc, xbc, xbc, dt_raw,
                jnp.pad(conv_w, ((0, SUBLANES - SSD_CONV), (0, 0))), conv_b.reshape(1, -1),
                head_lanes(dt_bias), head_lanes(a_log), jnp.repeat(d_skip, SSD_HEAD_DIM).reshape(1, -1), s0]
        out_specs = [chunk(SSD_INNER), chunk(SSD_CONV_DIM), state_spec]
        out_shape = [y_shape, jax.ShapeDtypeStruct((bsz, seq, SSD_CONV_DIM), BF16), state_shape]
        scratch = [pltpu.VMEM((SSD_CHUNK + 2 * HALO, SSD_CONV_DIM), F32), state_scratch]
    return pl.pallas_call(
        functools.partial(_ssd_kernel, rev=rev, nchunks=nchunks, readout=rev),
        grid=(bsz, nchunks),
        in_specs=in_specs, out_specs=out_specs, out_shape=out_shape, scratch_shapes=scratch,
        compiler_params=_cparams(("parallel", "arbitrary")),
        name="ssd_scan_rev" if rev else "ssd_scan_fwd",
    )(*args)


def _conformer_kernel(p_ref, w_ref, b_ref, g_ref, beta_ref, o_ref, pad_ref, rot_ref, conv_ref, *, nseq, slen):
    v = p_ref[0, :, :CF_CH].astype(F32)
    gate = p_ref[0, :, CF_CH:].astype(F32)
    u = v * jax.nn.sigmoid(gate)
    first = CF_PAD - CF_KERNEL // 2
    span = rot_ref.shape[2]
    for s in range(nseq):
        pad_ref[s, 0:CF_PAD, :] = jnp.zeros((CF_PAD, CF_CH), F32)
        pad_ref[s, CF_PAD + slen:2 * CF_PAD + slen, :] = jnp.zeros((CF_PAD, CF_CH), F32)
        pad_ref[s, CF_PAD:CF_PAD + slen, :] = u[s * slen:(s + 1) * slen, :]
        for res in range(1, SUBLANES):
            rot_ref[res - 1, s] = pad_ref[s, pl.ds(res, span), :]
    for s in range(nseq):
        for cb in range(CF_CH // LANES):
            ch = slice(cb * LANES, (cb + 1) * LANES)
            acc = jnp.broadcast_to(b_ref[:, ch], (slen, LANES))
            for k in range(CF_KERNEL):
                res, lead = (first + k) % SUBLANES, (first + k) // SUBLANES * SUBLANES
                src = pad_ref[s, pl.ds(lead, slen), ch] if res == 0 else rot_ref[res - 1, s, pl.ds(lead, slen), ch]
                acc = acc + w_ref[k:k + 1, ch] * src
            conv_ref[s * slen:(s + 1) * slen, ch] = acc
    y = conv_ref[...]
    mu = jnp.mean(y, axis=-1, keepdims=True)
    xc = y - mu
    var = jnp.mean(xc * xc, axis=-1, keepdims=True)
    y = xc * lax.rsqrt(var + EPS) * g_ref[...] + beta_ref[...]
    o_ref[0] = _silu(y).astype(o_ref.dtype)


def _conformer(p_cf, dw_w, dw_b, ln_g, ln_b, slen):
    bsz, seq, _ = p_cf.shape
    nseq = max(1, min(seq, 256) // slen)
    tb = nseq * slen
    kpad = -(-CF_KERNEL // SUBLANES) * SUBLANES
    last_tap_row = CF_PAD - CF_KERNEL // 2 + CF_KERNEL - 1
    span = slen + last_tap_row // SUBLANES * SUBLANES
    return pl.pallas_call(
        functools.partial(_conformer_kernel, nseq=nseq, slen=slen),
        grid=(bsz, seq // tb),
        in_specs=[pl.BlockSpec((1, tb, 2 * CF_CH), lambda b, i: (b, i, 0)),
                  pl.BlockSpec((kpad, CF_CH), lambda b, i: (0, 0)),
                  pl.BlockSpec((1, CF_CH), lambda b, i: (0, 0)),
                  pl.BlockSpec((1, CF_CH), lambda b, i: (0, 0)),
                  pl.BlockSpec((1, CF_CH), lambda b, i: (0, 0))],
        out_specs=pl.BlockSpec((1, tb, CF_CH), lambda b, i: (b, i, 0)),
        out_shape=jax.ShapeDtypeStruct((bsz, seq, CF_CH), BF16),
        scratch_shapes=[pltpu.VMEM((nseq, slen + 2 * CF_PAD, CF_CH), F32),
                        pltpu.VMEM((SUBLANES - 1, nseq, span, CF_CH), F32),
                        pltpu.VMEM((tb, CF_CH), F32)],
        compiler_params=_cparams(("parallel", "parallel")),
        name="conformer_conv",
    )(p_cf, jnp.pad(dw_w, ((0, kpad - CF_KERNEL), (0, 0))), dw_b.reshape(1, -1), ln_g.reshape(1, -1),
      ln_b.reshape(1, -1))


def _hgrn2_kernel(*refs, rev, mode):
    if mode == "state":
        v_ref, f_ref, lb_ref, s0_ref, sfin_ref, st_ref = refs
    elif mode == "out":
        q_ref, v_ref, f_ref, lb_ref, s0_ref, o_ref, st_ref = refs
    else:
        q_ref, v_ref, f_ref, lb_ref, s0_ref, oin_ref, g_ref, ng_ref, perm_ref, o_ref, st_ref, osc_ref = refs
    step = pl.program_id(1)
    ck = HG_CHUNK
    per_step = v_ref.shape[1] // ck

    @pl.when(step == 0)
    def _():
        st_ref[...] = s0_ref[0]

    lb = lb_ref[...]
    tri = _scan_tri(ck, rev).astype(BF16)
    last = 0 if rev else ck - 1
    mid_pos = HG_CHUNK // 2 - 1
    mid = (ck - 1 - mid_pos) if rev else mid_pos
    r = lax.broadcasted_iota(jnp.int32, (ck, ck), 0)
    c = lax.broadcasted_iota(jnp.int32, (ck, ck), 1)
    causal = (c >= r) if rev else (c <= r)

    for ci in (range(per_step - 1, -1, -1) if rev else range(per_step)):
        rows = slice(ci * ck, (ci + 1) * ck)
        f = lb + (1.0 - lb) * jax.nn.sigmoid(f_ref[0, rows, :])
        kk = 1.0 - f
        gcum = _dot01(tri, jnp.log(f))
        g_end = gcum[last:last + 1, :]
        k_end = (kk * jnp.exp(g_end - gcum)).astype(BF16)
        dec_end = jnp.exp(g_end)
        v = v_ref[0, rows, :]
        if mode != "state":
            g_mid = gcum[mid:mid + 1, :]
            q = _silu(q_ref[0, rows, :].astype(F32))
            q_rel = (q * jnp.exp(gcum - g_mid)).astype(BF16)
            k_rel = (kk * jnp.exp(g_mid - gcum)).astype(BF16)
            q_dec = (q * jnp.exp(gcum)).astype(BF16)
        if mode == "readout":
            gate = _silu(g_ref[0, rows, :].astype(F32))

        for h in range(HG_HEADS):
            ks = slice(h * HG_DK, (h + 1) * HG_DK)
            vs = slice(h * HG_DV, (h + 1) * HG_DV)
            s_in = st_ref[h]
            if mode != "state":
                att = lax.dot_general(q_rel[:, ks], k_rel[:, ks], _NT, preferred_element_type=F32)
                att = jnp.where(causal, att, 0.0).astype(BF16)
                o_h = jnp.dot(att, v[:, vs], preferred_element_type=F32)
                o_h = o_h + lax.dot_general(q_dec[:, ks], s_in.astype(BF16), _NT, preferred_element_type=F32)
                if mode == "readout":
                    o_h = o_h + oin_ref[0, rows, vs].astype(F32)
                    ms = jnp.mean(o_h * o_h, axis=-1, keepdims=True)
                    o_h = (o_h * lax.rsqrt(ms + EPS) * ng_ref[...]) * gate[:, vs]
                    osc_ref[rows, vs] = o_h.astype(osc_ref.dtype)
                else:
                    o_ref[0, rows, vs] = o_h.astype(o_ref.dtype)
            chunk_state = lax.dot_general(v[:, vs], k_end[:, ks], _TN, preferred_element_type=F32)
            st_ref[h] = s_in * dec_end[:, ks] + chunk_state

    if mode == "readout":
        nat = jnp.dot(perm_ref[...], osc_ref[...], preferred_element_type=F32).astype(o_ref.dtype)
        ncols = o_ref.shape[2]
        for gr in range(o_ref.shape[1]):
            o_ref[0, gr] = nat[gr * ncols:(gr + 1) * ncols, :]
    if mode == "state":
        @pl.when(step == pl.num_programs(1) - 1)
        def _():
            sfin_ref[0] = st_ref[...]


def _hgrn2_scan(v, f_raw, lb, s0, rev, mode, q=None, o_in=None, g=None, norm_g=None):
    bsz, seq, _ = v.shape
    tstep = min(seq, HG_STEP)
    nsteps = seq // tstep
    dcol = 1 if rev else 0

    def cidx(i):
        return (nsteps - 1 - i) if rev else i

    tok = lambda width: pl.BlockSpec((1, tstep, width), lambda b, i: (b, cidx(i), 0))
    state_spec = pl.BlockSpec((1, HG_HEADS, HG_DV, HG_DK), lambda b, i: (b, 0, 0, 0))
    f_spec = pl.BlockSpec((1, tstep, HG_KEY), lambda b, i: (b, cidx(i), dcol))
    lb_spec = pl.BlockSpec((1, HG_KEY), lambda b, i: (0, 0))
    state_shape = jax.ShapeDtypeStruct((bsz, HG_HEADS, HG_DV, HG_DK), F32)
    scratch = [pltpu.VMEM((HG_HEADS, HG_DV, HG_DK), F32)]
    if mode == "state":
        in_specs, args = [tok(HG_VAL), f_spec, lb_spec, state_spec], [v, f_raw, lb.reshape(1, -1), s0]
        out_specs, out_shape = state_spec, state_shape
    else:
        in_specs = [tok(HG_KEY), tok(HG_VAL), f_spec, lb_spec, state_spec]
        args = [q, v, f_raw, lb.reshape(1, -1), s0]
        out_specs, out_shape = tok(HG_VAL), jax.ShapeDtypeStruct((bsz, seq, HG_VAL), BF16)
        if mode == "readout":
            grid_rows = seq // GRID_W
            ncols = tstep // grid_rows
            tok_id = jnp.arange(tstep)
            src = (tok_id % ncols) * grid_rows + tok_id // ncols
            perm = (src[:, None] == tok_id[None, :]).astype(BF16)
            in_specs += [tok(HG_VAL), tok(HG_VAL), pl.BlockSpec((1, HG_DV), lambda b, i: (0, 0)),
                         pl.BlockSpec((tstep, tstep), lambda b, i: (0, 0))]
            args += [o_in, g, norm_g.reshape(1, -1), perm]
            out_specs = pl.BlockSpec((1, grid_rows, ncols, HG_VAL), lambda b, i: (b, 0, cidx(i), 0))
            out_shape = jax.ShapeDtypeStruct((bsz, grid_rows, GRID_W, HG_VAL), BF16)
            scratch.append(pltpu.VMEM((tstep, HG_VAL), BF16))
    out = pl.pallas_call(
        functools.partial(_hgrn2_kernel, rev=rev, mode=mode),
        grid=(bsz, nsteps),
        in_specs=in_specs, out_specs=out_specs, out_shape=out_shape,
        scratch_shapes=scratch,
        compiler_params=_cparams(("parallel", "arbitrary")),
        name=f"hgrn2_{mode}_{'rev' if rev else 'fwd'}",
    )(*args)
    return out.reshape(bsz, seq, HG_VAL) if mode == "readout" else out


def _first_argmax(vals):
    best, idx = vals[0], jnp.zeros(vals[0].shape, jnp.int32)
    for j in range(1, len(vals)):
        better = vals[j] > best
        idx = jnp.where(better, j, idx)
        best = jnp.where(better, vals[j], best)
    return idx, best


def _pick(idx, vals):
    out = vals[-1]
    for j in range(len(vals) - 2, -1, -1):
        out = jnp.where(idx == j, vals[j], out)
    return out


def _store_token_tiles(ref, v):
    n = v.shape[0]
    for j in range(SUBLANES):
        ref[pl.ds(j, n, stride=SUBLANES), :] = v[:, j * LANES:(j + 1) * LANES]


def _load_token_tiles(ref):
    n = ref.shape[0] // SUBLANES
    return jnp.concatenate([ref[pl.ds(j, n, stride=SUBLANES), :] for j in range(SUBLANES)], axis=1)


def _token_tile(ref, idx):
    return ref.at[pl.ds(pl.multiple_of(idx * SUBLANES, SUBLANES), SUBLANES)]


def _router_kernel(x_ref, g_ref, sh_ref, sc_ref, rw_ref, rb_ref, h_ref, slot_ref, wt_ref, cnt_ref, carry_ref):
    step = pl.program_id(0)

    @pl.when(step == 0)
    def _():
        carry_ref[...] = jnp.zeros_like(carry_ref)

    h = _normmod(x_ref[...], g_ref[...], sh_ref[0], sc_ref[0])
    _store_token_tiles(h_ref, h)
    scores = jax.nn.sigmoid(_dot_f32(rw_ref[...], h, _NT))
    sel = scores + rb_ref[...]
    srow = [sel[e:e + 1, :] for e in range(N_EXPERTS)]
    prow = [scores[e:e + 1, :] for e in range(N_EXPERTS)]
    gscore = []
    for gi in range(N_EXPERT_GROUPS):
        m = srow[gi * EXPERTS_PER_GROUP:(gi + 1) * EXPERTS_PER_GROUP]
        pair_sums = [m[i] + m[j] for i in range(EXPERTS_PER_GROUP) for j in range(i + 1, EXPERTS_PER_GROUP)]
        best = pair_sums[0]
        for p in pair_sums[1:]:
            best = jnp.maximum(best, p)
        gscore.append(best)
    gidx, _ = _first_argmax(gscore)
    in_sel = [_pick(gidx, [srow[gi * EXPERTS_PER_GROUP + j] for gi in range(N_EXPERT_GROUPS)])
              for j in range(EXPERTS_PER_GROUP)]
    in_p = [_pick(gidx, [prow[gi * EXPERTS_PER_GROUP + j] for gi in range(N_EXPERT_GROUPS)])
            for j in range(EXPERTS_PER_GROUP)]
    i1, _ = _first_argmax(in_sel)
    i2, _ = _first_argmax([jnp.where(i1 == j, -jnp.inf, in_sel[j]) for j in range(EXPERTS_PER_GROUP)])
    w1, w2 = _pick(i1, in_p), _pick(i2, in_p)
    den = w1 + w2
    e1, e2 = gidx * EXPERTS_PER_GROUP + i1, gidx * EXPERTS_PER_GROUP + i2
    tl = h.shape[0]
    erow = lax.broadcasted_iota(jnp.int32, (N_EXPERTS, tl), 0)
    oh1, oh2 = erow == e1, erow == e2
    cnt = (oh1 | oh2).astype(F32)
    r = lax.broadcasted_iota(jnp.int32, (tl, tl), 0)
    c = lax.broadcasted_iota(jnp.int32, (tl, tl), 1)
    before = jnp.dot(cnt.astype(BF16), (r < c).astype(BF16), preferred_element_type=F32)
    base = carry_ref[:, 0:1] + before
    rank1 = jnp.sum(jnp.where(oh1, base, 0.0), axis=0, keepdims=True).astype(jnp.int32)
    rank2 = jnp.sum(jnp.where(oh2, base, 0.0), axis=0, keepdims=True).astype(jnp.int32)
    carry_ref[...] = carry_ref[...] + jnp.sum(cnt, axis=1, keepdims=True)
    cnt_ref[...] = carry_ref[...].astype(jnp.int32)
    srow = lax.broadcasted_iota(jnp.int32, (SUBLANES, tl), 0)
    slot_ref[0] = jnp.where(srow == 0, e1, jnp.where(srow == 1, e2, jnp.where(srow == 2, rank1, rank2)))
    wrow = lax.broadcasted_iota(jnp.int32, (LANES, tl), 0)
    wt_ref[...] = jnp.where(wrow == 0, w1 / den, jnp.where(wrow == 1, w2 / den, 0.0)).T


def _ffn_norm_router(x, g, shift, scale, router_w, router_b, seq):
    t, d = x.shape
    tl = min(seq, MOE_TL)
    nblk = t // tl
    nb = shift.shape[0]
    mod_spec = pl.BlockSpec((1, 1, d), (lambda i: (i * tl // seq, 0, 0)) if nb > 1 else (lambda i: (0, 0, 0)))
    return pl.pallas_call(
        _router_kernel,
        grid=(nblk,),
        in_specs=[pl.BlockSpec((tl, d), lambda i: (i, 0)),
                  pl.BlockSpec((1, d), lambda i: (0, 0)),
                  mod_spec, mod_spec,
                  pl.BlockSpec((N_EXPERTS, d), lambda i: (0, 0)),
                  pl.BlockSpec((N_EXPERTS, 1), lambda i: (0, 0))],
        out_specs=[pl.BlockSpec((tl * SUBLANES, LANES), lambda i: (i, 0)),
                   pl.BlockSpec((1, SUBLANES, tl), lambda i: (i, 0, 0)),
                   pl.BlockSpec((tl, LANES), lambda i: (i, 0)),
                   pl.BlockSpec((N_EXPERTS, LANES), lambda i: (0, 0))],
        out_shape=[jax.ShapeDtypeStruct((t * SUBLANES, LANES), F32),
                   jax.ShapeDtypeStruct((nblk, SUBLANES, tl), jnp.int32),
                   jax.ShapeDtypeStruct((t, LANES), F32),
                   jax.ShapeDtypeStruct((N_EXPERTS, LANES), jnp.int32)],
        scratch_shapes=[pltpu.VMEM((N_EXPERTS, LANES), F32)],
        compiler_params=_cparams(("arbitrary",)),
        name="ffn_norm_router",
    )(x, g.reshape(1, d), shift, scale, router_w.T, router_b.reshape(N_EXPERTS, 1))


DMA_UNROLL = 8


def _dispatch_kernel(slot_ref, ends_ref, h_ref, hs_ref, zero_ref, sem, *, tl, max_tiles):
    @pl.when(pl.program_id(0) == 0)
    def _():
        zero_ref[...] = jnp.zeros_like(zero_ref)
        fills = []
        for e in range(N_EXPERTS):
            end = ends_ref[e]
            gap = (MOE_TM - (end & (MOE_TM - 1))) & (MOE_TM - 1)
            for bit in range(MOE_TM.bit_length() - 1):
                size = 1 << bit
                first_row = pl.multiple_of((end + (gap & (size - 1))) * SUBLANES, SUBLANES)
                fills.append(((gap & size) != 0,
                              pltpu.make_async_copy(zero_ref.at[pl.ds(0, size * SUBLANES)],
                                                    hs_ref.at[pl.ds(first_row, size * SUBLANES)], sem)))
        for n in range(N_EXPERTS):
            first_slot = ends_ref[N_EXPERTS] + n * MOE_TM
            first_row = pl.multiple_of(first_slot * SUBLANES, SUBLANES)
            fills.append((first_slot < max_tiles * MOE_TM,
                          pltpu.make_async_copy(zero_ref, hs_ref.at[pl.ds(first_row, MOE_TM * SUBLANES)], sem)))
        for cond, cp in fills:
            pl.when(cond)(cp.start)
        for cond, cp in fills:
            pl.when(cond)(cp.wait)

    def issue(t, carry):
        pltpu.make_async_copy(_token_tile(h_ref, t), _token_tile(hs_ref, slot_ref[0, 0, t]), sem).start()
        pltpu.make_async_copy(_token_tile(h_ref, t), _token_tile(hs_ref, slot_ref[0, 0, tl + t]), sem).start(priority=1)
        return carry

    lax.fori_loop(0, tl, issue, 0, unroll=DMA_UNROLL)
    for _ in range(2):
        pltpu.make_async_copy(h_ref, hs_ref.at[pl.ds(0, tl * SUBLANES)], sem).wait()


def _dispatch(h, slot2, ends, tl, max_tiles):
    t = h.shape[0] // SUBLANES
    nblk = t // tl
    return pl.pallas_call(
        functools.partial(_dispatch_kernel, tl=tl, max_tiles=max_tiles),
        grid=(nblk,),
        in_specs=[pl.BlockSpec((1, 1, 2 * tl), lambda i: (i, 0, 0), memory_space=pltpu.SMEM),
                  pl.BlockSpec(memory_space=pltpu.SMEM),
                  pl.BlockSpec((tl * SUBLANES, LANES), lambda i: (i, 0))],
        out_specs=pl.BlockSpec(memory_space=pl.ANY),
        out_shape=jax.ShapeDtypeStruct((max_tiles * MOE_TM * SUBLANES, LANES), F32),
        scratch_shapes=[pltpu.VMEM((MOE_TM * SUBLANES, LANES), F32), pltpu.SemaphoreType.DMA(())],
        compiler_params=_cparams(("arbitrary",)),
        name="moe_dispatch",
    )(slot2, ends, h)


def _expert_kernel(te_ref, tf_ref, nt_ref, hs_ref, wg_ref, wu_ref, wd_ref, ys_ref, wgb_ref, wub_ref, wdb_ref):
    n = pl.program_id(0)

    @pl.when(n >= nt_ref[0])
    def _():
        ys_ref[...] = jnp.zeros_like(ys_ref)

    @pl.when(n < nt_ref[0])
    def _():
        @pl.when(tf_ref[n] == 1)
        def _():
            wgb_ref[...] = wg_ref[0].astype(BF16)
            wub_ref[...] = wu_ref[0].astype(BF16)
            wdb_ref[...] = wd_ref[0].astype(BF16)

        h = _load_token_tiles(hs_ref).astype(BF16)
        a = jnp.dot(h, wgb_ref[...], preferred_element_type=F32)
        b = jnp.dot(h, wub_ref[...], preferred_element_type=F32)
        y = jnp.dot((_silu(a) * b).astype(BF16), wdb_ref[...], preferred_element_type=F32)
        _store_token_tiles(ys_ref, y)


def _experts(hs, tile_expert, tile_first, n_tiles, w_gate, w_up, w_down, layer, max_tiles):
    d = w_gate.shape[2]
    tile_spec = pl.BlockSpec((MOE_TM * SUBLANES, LANES), lambda n, te, tf, nt: (n, 0))
    grid_spec = pltpu.PrefetchScalarGridSpec(
        num_scalar_prefetch=3,
        grid=(max_tiles,),
        in_specs=[tile_spec,
                  pl.BlockSpec((None, 1, d, D_EXPERT), lambda n, te, tf, nt: (layer, te[n], 0, 0)),
                  pl.BlockSpec((None, 1, d, D_EXPERT), lambda n, te, tf, nt: (layer, te[n], 0, 0)),
                  pl.BlockSpec((None, 1, D_EXPERT, d), lambda n, te, tf, nt: (layer, te[n], 0, 0))],
        out_specs=tile_spec,
        scratch_shapes=[pltpu.VMEM((d, D_EXPERT), BF16), pltpu.VMEM((d, D_EXPERT), BF16),
                        pltpu.VMEM((D_EXPERT, d), BF16)])
    return pl.pallas_call(
        _expert_kernel,
        grid_spec=grid_spec,
        out_shape=jax.ShapeDtypeStruct((max_tiles * MOE_TM * SUBLANES, LANES), F32),
        compiler_params=_cparams(("arbitrary",)),
        name="moe_experts",
    )(tile_expert, tile_first, n_tiles, hs, w_gate, w_up, w_down)


def _combine_kernel(slot_ref, nslot_ref, wt_ref, res_ref, gate_ref, fg_ref, ys_ref, o_ref, y1_ref, y2_ref, sems, *,
                    tl, final):
    step, nsteps = pl.program_id(0), pl.num_programs(0)
    cur = step % 2

    def start_gather(sref, buf):
        def issue(t, carry):
            pltpu.make_async_copy(_token_tile(ys_ref, sref[0, 0, t]), _token_tile(y1_ref.at[buf], t),
                                  sems.at[buf]).start()
            pltpu.make_async_copy(_token_tile(ys_ref, sref[0, 0, tl + t]), _token_tile(y2_ref.at[buf], t),
                                  sems.at[buf]).start(priority=1)
            return carry

        lax.fori_loop(0, tl, issue, 0, unroll=DMA_UNROLL)

    @pl.when(step == 0)
    def _():
        start_gather(slot_ref, 0)

    @pl.when(step + 1 < nsteps)
    def _():
        start_gather(nslot_ref, 1 - cur)

    pltpu.make_async_copy(ys_ref.at[pl.ds(0, tl * SUBLANES)], y1_ref.at[cur], sems.at[cur]).wait()
    pltpu.make_async_copy(ys_ref.at[pl.ds(0, tl * SUBLANES)], y2_ref.at[cur], sems.at[cur]).wait()
    wt = wt_ref[...]
    moe = wt[:, 0:1] * _load_token_tiles(y1_ref.at[cur]) + wt[:, 1:2] * _load_token_tiles(y2_ref.at[cur])
    x = res_ref[...] + gate_ref[0] * moe
    if final:
        ms = jnp.mean(x * x, axis=-1, keepdims=True)
        x = x * lax.rsqrt(ms + EPS) * fg_ref[...]
    o_ref[...] = x


def _combine(ys, slot2, wt, res, gate, final_g, seq, tl):
    t, d = res.shape
    nblk = t // tl
    nb = gate.shape[0]
    final = final_g is not None
    fg = final_g.reshape(1, d) if final else jnp.ones((1, d), F32)
    return pl.pallas_call(
        functools.partial(_combine_kernel, tl=tl, final=final),
        grid=(nblk,),
        in_specs=[pl.BlockSpec((1, 1, 2 * tl), lambda i: (i, 0, 0), memory_space=pltpu.SMEM),
                  pl.BlockSpec((1, 1, 2 * tl), lambda i: (jnp.minimum(i + 1, nblk - 1), 0, 0),
                               memory_space=pltpu.SMEM),
                  pl.BlockSpec((tl, LANES), lambda i: (i, 0)),
                  pl.BlockSpec((tl, d), lambda i: (i, 0)),
                  pl.BlockSpec((1, 1, d), (lambda i: (i * tl // seq, 0, 0)) if nb > 1 else (lambda i: (0, 0, 0))),
                  pl.BlockSpec((1, d), lambda i: (0, 0)),
                  pl.BlockSpec(memory_space=pl.ANY)],
        out_specs=pl.BlockSpec((tl, d), lambda i: (i, 0)),
        out_shape=jax.ShapeDtypeStruct((t, d), F32),
        scratch_shapes=[pltpu.VMEM((2, tl * SUBLANES, LANES), F32), pltpu.VMEM((2, tl * SUBLANES, LANES), F32),
                        pltpu.SemaphoreType.DMA((2,))],
        compiler_params=_cparams(("arbitrary",)),
        name="moe_combine",
    )(slot2, slot2, wt, res, gate, fg, ys)


def _even_layer_mixer(xs, mods, norm_g, w_in, conv_w, conv_b, dt_bias, a_log, d_skip, ssd_g,
                      cf_w, cf_b, cf_lng, cf_lnb, w_out, rows):
    s_lo = SSD_INNER
    dt_lo = SSD_INNER + SSD_CONV_DIM
    cf_lo = dt_lo + 2 * SSD_HEADS
    w_cf = _realign_cols(w_in, cf_lo, 2 * CF_CH)
    bsz = xs[0].shape[0]
    zero_state = jnp.zeros((bsz, SSD_GROUPS, SSD_STATE, SSD_GROUP_W), F32)
    proj = []
    for x, m in zip(xs, mods):
        h = _norm_mod(x, norm_g, m[0], m[1])
        proj.append(dict(z=_matmul(h, w_in, BF16, 0, s_lo), xbc=_matmul(h, w_in, BF16, s_lo, SSD_CONV_DIM),
                         dt=_matmul(h, w_in, F32, dt_lo, LANES), cf=_matmul(h, w_cf, BF16)))
    y_fwd, act, o_ssd = [None, None], [None, None], [None, None]
    state = zero_state
    for si in range(2):
        p = proj[si]
        y_fwd[si], act[si], state = _ssd_scan(p["xbc"], p["dt"], dt_bias[0], a_log[0], state, False,
                                              conv_w=conv_w, conv_b=conv_b, d_skip=d_skip)
    state = zero_state
    for si in range(2):
        p = proj[si]
        o_ssd[si], state = _ssd_scan(act[si], p["dt"], dt_bias[1], a_log[1], state, True,
                                     y_in=y_fwd[si], z=p["z"], norm_g=ssd_g)
    outs = []
    for si, (x, m) in enumerate(zip(xs, mods)):
        slen = x.shape[1] if si == 0 else GRID_W
        o_cf = _conformer(proj[si]["cf"], cf_w, cf_b, cf_lng, cf_lnb, slen)
        outs.append(_matmul_res2(o_ssd[si], o_cf, w_out, x, m[2]))
    return outs


def _odd_layer_mixer(xc, xl, m_c, m_l, norm_g, w_in, lb, hg_g, w_out):
    st0 = HG_KEY + HG_VAL
    f0 = st0 + HG_VAL
    bsz = xl.shape[0]
    h_c = _norm_mod(xc, norm_g, m_c[0], m_c[1])
    h_l = _norm_mod_colmajor(xl, norm_g, m_l[0], m_l[1])
    v_c, f_c = _matmul(h_c, w_in, BF16, st0, HG_VAL), _matmul(h_c, w_in, F32, f0, 2 * HG_KEY)
    q_l, g_l = _matmul(h_l, w_in, BF16, 0, HG_KEY), _matmul(h_l, w_in, BF16, HG_KEY, HG_VAL)
    v_l, f_l = _matmul(h_l, w_in, BF16, st0, HG_VAL), _matmul(h_l, w_in, F32, f0, 2 * HG_KEY)
    zero_state = jnp.zeros((bsz, HG_HEADS, HG_DV, HG_DK), F32)
    s_f = _hgrn2_scan(v_c, f_c, lb, zero_state, False, "state")
    s_r = _hgrn2_scan(v_c, f_c, lb, zero_state, True, "state")
    o_f = _hgrn2_scan(v_l, f_l, lb, s_f, False, "out", q=q_l)
    o = _hgrn2_scan(v_l, f_l, lb, s_r, True, "readout", q=q_l, o_in=o_f, g=g_l, norm_g=hg_g)
    return _matmul_res(o, w_out, xl, m_l[2])


def _dispatch_layout(counts, slot, max_tiles):
    tiles = (counts + MOE_TM - 1) // MOE_TM
    tile_ends = jnp.cumsum(tiles)
    tile_starts = tile_ends - tiles
    total = tile_ends[-1]
    starts = tile_starts * MOE_TM
    ends = jnp.concatenate([starts + counts, (total * MOE_TM).reshape(1)]).astype(jnp.int32)
    experts = jnp.arange(N_EXPERTS, dtype=jnp.int32)

    def lookup(table, idx):
        return jnp.sum(jnp.where(idx[..., None] == experts, table, 0), axis=-1)

    slots = jnp.concatenate([lookup(starts, slot[:, 0]) + slot[:, 2], lookup(starts, slot[:, 1]) + slot[:, 3]],
                            axis=-1)
    n = jnp.arange(max_tiles, dtype=jnp.int32)
    nn = jnp.minimum(n, total - 1)
    expert = jnp.sum((nn[:, None] >= tile_ends[None, :]).astype(jnp.int32), axis=1)
    is_first = ((nn == lookup(tile_starts, expert)) & (n < total)).astype(jnp.int32)
    return slots[:, None, :].astype(jnp.int32), ends, expert, is_first, total.reshape(1).astype(jnp.int32)


def _moe_block(x, m, norm_g, router_w, router_b, w_gate, w_up, w_down, layer, final_g=None):
    bsz, seq, d = x.shape
    t = bsz * seq
    xf = x.reshape(t, d)
    tl = min(seq, MOE_TL)
    h, slot, wt, cnt = _ffn_norm_router(xf, norm_g, m[3], m[4], router_w, router_b, seq)
    max_tiles = 2 * t // MOE_TM + N_EXPERTS
    slot2, ends, tile_expert, tile_first, n_tiles = _dispatch_layout(cnt[:, 0], slot, max_tiles)
    hs = _dispatch(h, slot2, ends, tl, max_tiles)
    ys = _experts(hs, tile_expert, tile_first, n_tiles, w_gate, w_up, w_down, layer, max_tiles)
    return _combine(ys, slot2, wt, xf, m[5], final_g, seq, tl).reshape(bsz, seq, d)


def kernel(x, c, ctx, c_ctx, mod_w, mod_b, norm_mix_g, norm_ffn_g, router_w, router_b, moe_w_gate, moe_w_up,
           moe_w_down, ab_w_in, ssd_conv_w, ssd_conv_b, ssd_dt_bias, ssd_a_log, ssd_d, ssd_norm_g, cf_dw_w,
           cf_dw_b, cf_ln_g, cf_ln_b, ab_w_out, hg_w_in, hg_lb, hg_norm_g, hg_w_out, final_norm_g):
    depth = mod_w.shape[0]
    assert depth == 2, "layer schedule below is written for one even and one odd layer"
    bsz, seq, d = x.shape
    rows = seq // GRID_W
    lb_all = jnp.cumsum(jax.nn.softmax(hg_lb.astype(F32), axis=0), axis=0)
    lb_all = lb_all - lb_all[0]

    nrow = -(-(bsz + 1) // SUBLANES) * SUBLANES
    cond = jnp.zeros((nrow, d), F32).at[:bsz].set(c).at[bsz].set(c_ctx)
    mod = _modulation(cond, mod_w, mod_b).reshape(depth, nrow, N_MOD, 1, d)

    def mods(l):
        m_l = [mod[l, :bsz, k] for k in range(N_MOD)]
        m_c = [mod[l, bsz:bsz + 1, k] for k in range(N_MOD)]
        return m_c, m_l

    m_c, m_l = mods(0)
    xc, xl = _even_layer_mixer((ctx, x), (m_c, m_l), norm_mix_g[0], ab_w_in[0], ssd_conv_w[0], ssd_conv_b[0],
                               ssd_dt_bias[0], ssd_a_log[0], ssd_d[0], ssd_norm_g[0], cf_dw_w[0], cf_dw_b[0],
                               cf_ln_g[0], cf_ln_b[0], ab_w_out[0], rows)
    xl = _moe_block(xl, m_l, norm_ffn_g[0], router_w, router_b, moe_w_gate, moe_w_up, moe_w_down, 0)
    xc = _moe_block(xc, m_c, norm_ffn_g[0], router_w, router_b, moe_w_gate, moe_w_up, moe_w_down, 0)

    m_c, m_l = mods(1)
    xl = _odd_layer_mixer(xc, xl, m_c, m_l, norm_mix_g[1], hg_w_in[0], lb_all[1], hg_norm_g[0], hg_w_out[0])
    return _moe_block(xl, m_l, norm_ffn_g[1], router_w, router_b, moe_w_gate, moe_w_up, moe_w_down, 1,
                      final_g=final_norm_g)
```

```python
import functools

import jax
import jax.numpy as jnp
from jax import lax
from jax.experimental import pallas as pl
from jax.experimental.pallas import tpu as pltpu

F32 = jnp.float32
BF16 = jnp.bfloat16

D_MODEL = 1024
GRID_W = 64
EPS = 1e-6
N_MOD = 6

SSD_HEADS = 16
SSD_HEAD_DIM = 64
SSD_INNER = SSD_HEADS * SSD_HEAD_DIM
SSD_GROUPS = 4
SSD_STATE = 128
SSD_CONV = 5
SSD_CHUNK = 128
SSD_STEP = 256
SSD_BC = SSD_GROUPS * SSD_STATE
SSD_CONV_DIM = SSD_INNER + 2 * SSD_BC
SSD_GROUP_W = SSD_INNER // SSD_GROUPS

CF_CH = 1024
CF_KERNEL = 31
CF_PAD = 16

HG_HEADS = 8
HG_DK = 128
HG_DV = 128
HG_KEY = HG_HEADS * HG_DK
HG_VAL = HG_HEADS * HG_DV
HG_CHUNK = 64
HG_STEP = 512

N_EXPERTS = 16
N_EXPERT_GROUPS = 4
EXPERTS_PER_GROUP = 4
D_EXPERT = 512
MOE_TL = 512
MOE_TM = 512

LANES = 128
SUBLANES = 8
HALO = 16
CM_COLS = 16
VMEM_LIMIT = 48 * 1024 * 1024


def _cparams(sem):
    return pltpu.CompilerParams(dimension_semantics=sem, vmem_limit_bytes=VMEM_LIMIT)


def _silu(x):
    return x * jax.nn.sigmoid(x)


def _split3(v):
    hi = v.astype(BF16)
    r1 = v - hi.astype(F32)
    mid = r1.astype(BF16)
    lo = (r1 - mid.astype(F32)).astype(BF16)
    return hi, mid, lo


def _dot01(m01, v):
    hi, mid, lo = _split3(v)
    out = jnp.dot(m01, lo, preferred_element_type=F32)
    out = out + jnp.dot(m01, mid, preferred_element_type=F32)
    return out + jnp.dot(m01, hi, preferred_element_type=F32)


def _dot_v01(v, m01):
    hi, mid, lo = _split3(v)
    out = jnp.dot(lo, m01, preferred_element_type=F32)
    out = out + jnp.dot(mid, m01, preferred_element_type=F32)
    return out + jnp.dot(hi, m01, preferred_element_type=F32)


_NN = (((1,), (0,)), ((), ()))
_NT = (((1,), (1,)), ((), ()))
_TN = (((0,), (0,)), ((), ()))


def _dot_f32(a, b, dn):
    a1, a2, a3 = _split3(a)
    b1, b2, b3 = _split3(b)
    out = lax.dot_general(a3, b1, dn, preferred_element_type=F32)
    out = out + lax.dot_general(a1, b3, dn, preferred_element_type=F32)
    out = out + lax.dot_general(a2, b2, dn, preferred_element_type=F32)
    out = out + lax.dot_general(a2, b1, dn, preferred_element_type=F32)
    out = out + lax.dot_general(a1, b2, dn, preferred_element_type=F32)
    return out + lax.dot_general(a1, b1, dn, preferred_element_type=F32)


def _scan_tri(n, rev):
    r = lax.broadcasted_iota(jnp.int32, (n, n), 0)
    c = lax.broadcasted_iota(jnp.int32, (n, n), 1)
    return ((c >= r) if rev else (c <= r))


def _mod_kernel(c_ref, w_ref, b_ref, o_ref):
    cond = _silu(c_ref[...])
    o_ref[0] = _dot_f32(cond, w_ref[0], _NN) + b_ref[0]


def _modulation(cond, mod_w, mod_b):
    depth, d, n = mod_w.shape
    r = cond.shape[0]
    tn = 512
    return pl.pallas_call(
        _mod_kernel,
        grid=(depth, n // tn),
        in_specs=[pl.BlockSpec((r, d), lambda l, j: (0, 0)),
                  pl.BlockSpec((1, d, tn), lambda l, j: (l, 0, j)),
                  pl.BlockSpec((1, 1, tn), lambda l, j: (l, 0, j))],
        out_specs=pl.BlockSpec((1, r, tn), lambda l, j: (l, 0, j)),
        out_shape=jax.ShapeDtypeStruct((depth, r, n), F32),
        compiler_params=_cparams(("parallel", "parallel")),
        name="modulation",
    )(cond, mod_w, mod_b.reshape(depth, 1, n))


def _normmod(x, g, shift, scale):
    ms = jnp.mean(x * x, axis=-1, keepdims=True)
    return (x * lax.rsqrt(ms + EPS) * g) * (1.0 + scale) + shift


def _normmod_kernel(x_ref, g_ref, sh_ref, sc_ref, o_ref):
    o_ref[0] = _normmod(x_ref[0], g_ref[...], sh_ref[0], sc_ref[0]).astype(o_ref.dtype)


def _mod_spec(nb):
    return pl.BlockSpec((1, 1, D_MODEL), (lambda b, *_: (b, 0, 0)) if nb > 1 else (lambda b, *_: (0, 0, 0)))


def _norm_mod(x, g, shift, scale):
    bsz, seq, d = x.shape
    tl = min(seq, 512)
    return pl.pallas_call(
        _normmod_kernel,
        grid=(bsz, seq // tl),
        in_specs=[pl.BlockSpec((1, tl, d), lambda b, i: (b, i, 0)),
                  pl.BlockSpec((1, d), lambda b, i: (0, 0)),
                  _mod_spec(shift.shape[0]), _mod_spec(scale.shape[0])],
        out_specs=pl.BlockSpec((1, tl, d), lambda b, i: (b, i, 0)),
        out_shape=jax.ShapeDtypeStruct((bsz, seq, d), BF16),
        compiler_params=_cparams(("parallel", "parallel")),
        name="norm_mod",
    )(x, g.reshape(1, d), shift, scale)


def _normmod_cm_kernel(x_ref, g_ref, sh_ref, sc_ref, o_ref, h_ref, *, rows):
    h = _normmod(x_ref[0], g_ref[...], sh_ref[0], sc_ref[0])
    nlt = h.shape[-1] // LANES
    for r in range(rows):
        for j in range(nlt):
            h_ref[j, r * CM_COLS:(r + 1) * CM_COLS, :] = h[r, :, j * LANES:(j + 1) * LANES]
    for c in range(CM_COLS):
        for j in range(nlt):
            piece = h_ref[j, pl.ds(c, rows, stride=CM_COLS), :]
            o_ref[0, c * rows:(c + 1) * rows, j * LANES:(j + 1) * LANES] = piece.astype(o_ref.dtype)


def _norm_mod_colmajor(x, g, shift, scale):
    bsz, seq, d = x.shape
    rows = seq // GRID_W
    x4 = x.reshape(bsz, rows, GRID_W, d)
    return pl.pallas_call(
        functools.partial(_normmod_cm_kernel, rows=rows),
        grid=(bsz, GRID_W // CM_COLS),
        in_specs=[pl.BlockSpec((1, rows, CM_COLS, d), lambda b, i: (b, 0, i, 0)),
                  pl.BlockSpec((1, d), lambda b, i: (0, 0)),
                  _mod_spec(shift.shape[0]), _mod_spec(scale.shape[0])],
        out_specs=pl.BlockSpec((1, CM_COLS * rows, d), lambda b, i: (b, i, 0)),
        out_shape=jax.ShapeDtypeStruct((bsz, seq, d), BF16),
        scratch_shapes=[pltpu.VMEM((d // LANES, CM_COLS * rows, LANES), F32)],
        compiler_params=_cparams(("parallel", "parallel")),
        name="norm_mod_colmajor",
    )(x4, g.reshape(1, d), shift, scale)


MM_SEMANTICS = ("arbitrary", "arbitrary", "arbitrary")


def _stage_weights(b_ref, wb_ref):
    @pl.when((pl.program_id(1) == 0) & (pl.program_id(2) == 0))
    def _():
        wb_ref[...] = b_ref[...].astype(BF16)


def _mm_kernel(a_ref, b_ref, o_ref, wb_ref):
    _stage_weights(b_ref, wb_ref)
    o_ref[0] = jnp.dot(a_ref[0], wb_ref[...], preferred_element_type=F32).astype(o_ref.dtype)


def _mm_res2_kernel(a1_ref, a2_ref, b1_ref, b2_ref, res_ref, gate_ref, o_ref, wb1_ref, wb2_ref):
    _stage_weights(b1_ref, wb1_ref)
    _stage_weights(b2_ref, wb2_ref)
    y = jnp.dot(a1_ref[0], wb1_ref[...], preferred_element_type=F32)
    y = y + jnp.dot(a2_ref[0], wb2_ref[...], preferred_element_type=F32)
    o_ref[0] = res_ref[0] + gate_ref[0] * y


def _mm_res_kernel(a_ref, b_ref, res_ref, gate_ref, o_ref, wb_ref):
    _stage_weights(b_ref, wb_ref)
    o_ref[0] = res_ref[0] + gate_ref[0] * jnp.dot(a_ref[0], wb_ref[...], preferred_element_type=F32)


def _matmul(a, w, out_dtype, col0=0, ncols=None):
    bsz, seq, k = a.shape
    n = w.shape[1] - col0 if ncols is None else ncols
    tm, tn = min(seq, 1024), min(n, 1024)
    assert col0 % tn == 0 and n % tn == 0
    jb = col0 // tn
    return pl.pallas_call(
        _mm_kernel,
        grid=(n // tn, bsz, seq // tm),
        in_specs=[pl.BlockSpec((1, tm, k), lambda j, b, i: (b, i, 0)),
                  pl.BlockSpec((k, tn), lambda j, b, i: (0, j + jb))],
        out_specs=pl.BlockSpec((1, tm, tn), lambda j, b, i: (b, i, j)),
        out_shape=jax.ShapeDtypeStruct((bsz, seq, n), out_dtype),
        scratch_shapes=[pltpu.VMEM((k, tn), BF16)],
        compiler_params=_cparams(MM_SEMANTICS),
        name="matmul",
    )(a, w)


def _matmul_res2(a1, a2, w, res, gate):
    bsz, seq, kh = a1.shape
    n = w.shape[1]
    tm, tn = min(seq, 1024), min(n, 512)
    nb = gate.shape[0]
    return pl.pallas_call(
        _mm_res2_kernel,
        grid=(n // tn, bsz, seq // tm),
        in_specs=[pl.BlockSpec((1, tm, kh), lambda j, b, i: (b, i, 0)),
                  pl.BlockSpec((1, tm, kh), lambda j, b, i: (b, i, 0)),
                  pl.BlockSpec((kh, tn), lambda j, b, i: (0, j)),
                  pl.BlockSpec((kh, tn), lambda j, b, i: (1, j)),
                  pl.BlockSpec((1, tm, tn), lambda j, b, i: (b, i, j)),
                  pl.BlockSpec((1, 1, tn), (lambda j, b, i: (b, 0, j)) if nb > 1 else (lambda j, b, i: (0, 0, j)))],
        out_specs=pl.BlockSpec((1, tm, tn), lambda j, b, i: (b, i, j)),
        out_shape=jax.ShapeDtypeStruct((bsz, seq, n), F32),
        scratch_shapes=[pltpu.VMEM((kh, tn), BF16), pltpu.VMEM((kh, tn), BF16)],
        compiler_params=_cparams(MM_SEMANTICS),
        name="matmul_res",
    )(a1, a2, w, w, res, gate)


def _realign_kernel(w_ref, o_ref, *, col0):
    o_ref[...] = w_ref[:, col0:col0 + o_ref.shape[1]].astype(o_ref.dtype)


def _realign_cols(w, col0, ncols):
    k, n = w.shape
    tk = 128
    return pl.pallas_call(
        functools.partial(_realign_kernel, col0=col0),
        grid=(k // tk,),
        in_specs=[pl.BlockSpec((tk, n), lambda i: (i, 0))],
        out_specs=pl.BlockSpec((tk, ncols), lambda i: (i, 0)),
        out_shape=jax.ShapeDtypeStruct((k, ncols), BF16),
        compiler_params=_cparams(("parallel",)),
        name="realign_cols",
    )(w)


def _matmul_res(a, w, res, gate):
    bsz, seq, k = a.shape
    n = w.shape[1]
    tm, tn = min(seq, 1024), min(n, 512)
    return pl.pallas_call(
        _mm_res_kernel,
        grid=(n // tn, bsz, seq // tm),
        in_specs=[pl.BlockSpec((1, tm, k), lambda j, b, i: (b, i, 0)),
                  pl.BlockSpec((k, tn), lambda j, b, i: (0, j)),
                  pl.BlockSpec((1, tm, tn), lambda j, b, i: (b, i, j)),
                  pl.BlockSpec((1, 1, tn), lambda j, b, i: (b, 0, j))],
        out_specs=pl.BlockSpec((1, tm, tn), lambda j, b, i: (b, i, j)),
        out_shape=jax.ShapeDtypeStruct((bsz, seq, n), F32),
        scratch_shapes=[pltpu.VMEM((k, tn), BF16)],
        compiler_params=_cparams(MM_SEMANTICS),
        name="matmul_res1",
    )(a, w, res, gate)


def _ssd_kernel(*refs, rev, nsteps, readout):
    if readout:
        act_ref, dt_ref, dtb_ref, alog_ref, s0_ref, yin_ref, z_ref, ng_ref, y_ref, sfin_ref, st_ref = refs
    else:
        (xbc_ref, prev_ref, next_ref, dt_ref, cw_ref, cb_ref, dtb_ref, alog_ref, dsk_ref, s0_ref,
         y_ref, act_ref, sfin_ref, pad_ref, st_ref) = refs
    step = pl.program_id(1)
    u = (nsteps - 1 - step) if rev else step
    ck = SSD_CHUNK
    tstep = dt_ref.shape[1]
    per_step = tstep // ck
    hoff = SSD_HEADS if rev else 0

    @pl.when(step == 0)
    def _():
        st_ref[...] = s0_ref[0]

    if not readout:
        pad_ref[HALO:HALO + tstep, :] = xbc_ref[0].astype(F32)
        pad_ref[0:HALO, :] = jnp.where(u > 0, prev_ref[0].astype(F32), 0.0)
        pad_ref[HALO + tstep:2 * HALO + tstep, :] = jnp.where(u < nsteps - 1, next_ref[0].astype(F32), 0.0)

    hlane = lax.broadcasted_iota(jnp.int32, (ck, LANES), 1)
    is_head = (hlane >= hoff) & (hlane < hoff + SSD_HEADS)
    neg_a = -jnp.exp(alog_ref[...])
    tri = _scan_tri(ck, rev).astype(BF16)
    last = 0 if rev else ck - 1
    hrow = lax.broadcasted_iota(jnp.int32, (LANES, SSD_INNER), 0)
    hcol = lax.broadcasted_iota(jnp.int32, (LANES, SSD_INNER), 1)
    expand = (hcol // SSD_HEAD_DIM + hoff == hrow).astype(BF16)
    r = lax.broadcasted_iota(jnp.int32, (ck, ck), 0)
    c = lax.broadcasted_iota(jnp.int32, (ck, ck), 1)
    causal = (c >= r) if rev else (c <= r)
    lane = lax.broadcasted_iota(jnp.int32, (ck, LANES), 1)
    groups = range(SSD_GROUPS)
    gw = [slice(g * SSD_GROUP_W, (g + 1) * SSD_GROUP_W) for g in groups]
    heads_per_group = SSD_HEADS // SSD_GROUPS

    def prepare(ci):
        rows = slice(ci * ck, (ci + 1) * ck)
        if readout:
            xbc = act_ref[0, rows, :].astype(F32)
        else:
            half = SSD_CONV // 2
            acc = cb_ref[...] + cw_ref[0:1, :] * pad_ref[pl.ds(ci * ck + HALO - half, ck), :]
            for k in range(1, SSD_CONV):
                acc = acc + cw_ref[k:k + 1, :] * pad_ref[pl.ds(ci * ck + HALO - half + k, ck), :]
            xbc = _silu(acc)
            act_ref[0, rows, :] = xbc.astype(act_ref.dtype)
        xs = xbc[:, :SSD_INNER]
        dt = jnp.where(is_head, jax.nn.softplus(dt_ref[0, rows, :] + dtb_ref[...]), 0.0)
        acum = _dot01(tri, dt * neg_a)
        total = acum[last:last + 1, :]
        acum_t = acum.T
        dt_x = _dot_v01(dt, expand)
        te_x = _dot_v01(jnp.exp(total - acum), expand)
        xdt = xs * dt_x
        xdt_b = xdt.astype(BF16)
        p = dict(xs=xs, ea_x=_dot_v01(jnp.exp(acum), expand), xw_b=(xdt * te_x).astype(BF16),
                 cd_x=_dot_v01(jnp.broadcast_to(jnp.exp(total), (SUBLANES, LANES)), expand)[0:1, :],
                 b_g=[xbc[:, SSD_INNER + g * SSD_STATE:SSD_INNER + (g + 1) * SSD_STATE].astype(BF16) for g in groups],
                 c_g=[xbc[:, SSD_INNER + SSD_BC + g * SSD_STATE:SSD_INNER + SSD_BC + (g + 1) * SSD_STATE].astype(BF16)
                      for g in groups],
                 decay=[], masked_x=[])
        for hd in range(SSD_HEADS):
            tile, half_i = hd // 2, hd % 2
            h = hoff + hd
            seg = acum[:, h:h + 1] - acum_t[h:h + 1, :]
            p["decay"].append(jnp.where(causal, jnp.exp(seg), 0.0))
            xpair = xdt_b[:, tile * LANES:(tile + 1) * LANES]
            in_half = (lane >= half_i * SSD_HEAD_DIM) & (lane < (half_i + 1) * SSD_HEAD_DIM)
            p["masked_x"].append(jnp.where(in_half, xpair, jnp.zeros_like(xpair)))
        return p

    order = list(range(per_step - 1, -1, -1) if rev else range(per_step))
    ready = prepare(order[0])
    for n, ci in enumerate(order):
        rows = slice(ci * ck, (ci + 1) * ck)
        p = ready
        if n + 1 < len(order):
            ready = prepare(order[n + 1])
        b_g, c_g = p["b_g"], p["c_g"]
        s_in = [st_ref[g] for g in groups]
        cb = [lax.dot_general(c_g[g], b_g[g], _NT, preferred_element_type=F32) for g in groups]
        y_off = [jnp.dot(c_g[g], s_in[g].astype(BF16), preferred_element_type=F32) for g in groups]
        chunk_state = [lax.dot_general(b_g[g], p["xw_b"][:, gw[g]], _TN, preferred_element_type=F32) for g in groups]
        for g in groups:
            st_ref[g] = p["cd_x"][:, gw[g]] * s_in[g] + chunk_state[g]
        decay_cb = [(cb[hd // heads_per_group] * p["decay"][hd]).astype(BF16) for hd in range(SSD_HEADS)]
        y_diag = [jnp.dot(decay_cb[hd], p["masked_x"][hd], preferred_element_type=F32) for hd in range(SSD_HEADS)]
        y_parts = []
        for tile in range(SSD_HEADS // 2):
            g, within = (2 * tile) // heads_per_group, tile % (heads_per_group // 2)
            y_tile = y_off[g][:, within * LANES:(within + 1) * LANES] * p["ea_x"][:, tile * LANES:(tile + 1) * LANES]
            y_parts.append(y_tile + y_diag[2 * tile] + y_diag[2 * tile + 1])
        y = jnp.concatenate(y_parts, axis=1)

        if readout:
            y = y + yin_ref[0, rows, :].astype(F32)
            y = y * _silu(z_ref[0, rows, :].astype(F32))
            ms = jnp.mean(y * y, axis=-1, keepdims=True)
            y_ref[0, rows, :] = (y * lax.rsqrt(ms + EPS) * ng_ref[...]).astype(y_ref.dtype)
        else:
            y_ref[0, rows, :] = (y + dsk_ref[...] * p["xs"]).astype(y_ref.dtype)

    @pl.when(step == nsteps - 1)
    def _():
        sfin_ref[0] = st_ref[...]


def _ssd_scan(xbc, dt_raw, dt_bias, a_log, s0, rev, conv_w=None, conv_b=None, d_skip=None,
              y_in=None, z=None, norm_g=None):
    bsz, seq, _ = xbc.shape
    tstep = min(seq, SSD_STEP)
    nsteps = seq // tstep
    hb = tstep // HALO
    nhalo = seq // HALO
    hoff = SSD_HEADS if rev else 0

    def cidx(i):
        return (nsteps - 1 - i) if rev else i

    def head_lanes(v):
        return jnp.pad(v.reshape(1, -1), ((0, 0), (hoff, LANES - hoff - v.shape[-1])))

    chunk = lambda width: pl.BlockSpec((1, tstep, width), lambda b, i: (b, cidx(i), 0))
    row = lambda width: pl.BlockSpec((1, width), lambda b, i: (0, 0))
    state_spec = pl.BlockSpec((1, SSD_GROUPS, SSD_STATE, SSD_GROUP_W), lambda b, i: (b, 0, 0, 0))
    state_shape = jax.ShapeDtypeStruct((bsz, SSD_GROUPS, SSD_STATE, SSD_GROUP_W), F32)
    y_shape = jax.ShapeDtypeStruct((bsz, seq, SSD_INNER), BF16)
    state_scratch = pltpu.VMEM((SSD_GROUPS, SSD_STATE, SSD_GROUP_W), F32)
    if rev:
        in_specs = [chunk(SSD_CONV_DIM), chunk(LANES), row(LANES), row(LANES), state_spec,
                    chunk(SSD_INNER), chunk(SSD_INNER), row(SSD_INNER)]
        args = [xbc, dt_raw, head_lanes(dt_bias), head_lanes(a_log), s0, y_in, z, norm_g.reshape(1, -1)]
        out_specs, out_shape = [chunk(SSD_INNER), state_spec], [y_shape, state_shape]
        scratch = [state_scratch]
    else:
        in_specs = [
            chunk(SSD_CONV_DIM),
            pl.BlockSpec((1, HALO, SSD_CONV_DIM), lambda b, i: (b, jnp.maximum(i * hb - 1, 0), 0)),
            pl.BlockSpec((1, HALO, SSD_CONV_DIM), lambda b, i: (b, jnp.minimum((i + 1) * hb, nhalo - 1), 0)),
            chunk(LANES),
            pl.BlockSpec((SUBLANES, SSD_CONV_DIM), lambda b, i: (0, 0)), row(SSD_CONV_DIM),
            row(LANES), row(LANES), row(SSD_INNER), state_spec]
        args = [xbc, xbc, xbc, dt_raw,
                jnp.pad(conv_w, ((0, SUBLANES - SSD_CONV), (0, 0))), conv_b.reshape(1, -1),
                head_lanes(dt_bias), head_lanes(a_log), jnp.repeat(d_skip, SSD_HEAD_DIM).reshape(1, -1), s0]
        out_specs = [chunk(SSD_INNER), chunk(SSD_CONV_DIM), state_spec]
        out_shape = [y_shape, jax.ShapeDtypeStruct((bsz, seq, SSD_CONV_DIM), BF16), state_shape]
        scratch = [pltpu.VMEM((tstep + 2 * HALO, SSD_CONV_DIM), F32), state_scratch]
    return pl.pallas_call(
        functools.partial(_ssd_kernel, rev=rev, nsteps=nsteps, readout=rev),
        grid=(bsz, nsteps),
        in_specs=in_specs, out_specs=out_specs, out_shape=out_shape, scratch_shapes=scratch,
        compiler_params=_cparams(("parallel", "arbitrary")),
        name="ssd_scan_rev" if rev else "ssd_scan_fwd",
    )(*args)


def _conformer_kernel(p_ref, w_ref, b_ref, g_ref, beta_ref, o_ref, pad_ref, rot_ref, conv_ref, *, nseq, slen):
    v = p_ref[0, :, :CF_CH].astype(F32)
    gate = p_ref[0, :, CF_CH:].astype(F32)
    u = v * jax.nn.sigmoid(gate)
    first = CF_PAD - CF_KERNEL // 2
    span = rot_ref.shape[2]
    for s in range(nseq):
        pad_ref[s, 0:CF_PAD, :] = jnp.zeros((CF_PAD, CF_CH), F32)
        pad_ref[s, CF_PAD + slen:2 * CF_PAD + slen, :] = jnp.zeros((CF_PAD, CF_CH), F32)
        pad_ref[s, CF_PAD:CF_PAD + slen, :] = u[s * slen:(s + 1) * slen, :]
        for res in range(1, SUBLANES):
            rot_ref[res - 1, s] = pad_ref[s, pl.ds(res, span), :]
    for s in range(nseq):
        for cb in range(CF_CH // LANES):
            ch = slice(cb * LANES, (cb + 1) * LANES)
            acc = jnp.broadcast_to(b_ref[:, ch], (slen, LANES))
            for k in range(CF_KERNEL):
                res, lead = (first + k) % SUBLANES, (first + k) // SUBLANES * SUBLANES
                src = pad_ref[s, pl.ds(lead, slen), ch] if res == 0 else rot_ref[res - 1, s, pl.ds(lead, slen), ch]
                acc = acc + w_ref[k:k + 1, ch] * src
            conv_ref[s * slen:(s + 1) * slen, ch] = acc
    y = conv_ref[...]
    mu = jnp.mean(y, axis=-1, keepdims=True)
    xc = y - mu
    var = jnp.mean(xc * xc, axis=-1, keepdims=True)
    y = xc * lax.rsqrt(var + EPS) * g_ref[...] + beta_ref[...]
    o_ref[0] = _silu(y).astype(o_ref.dtype)


def _conformer(p_cf, dw_w, dw_b, ln_g, ln_b, slen):
    bsz, seq, _ = p_cf.shape
    nseq = max(1, min(seq, 256) // slen)
    tb = nseq * slen
    kpad = -(-CF_KERNEL // SUBLANES) * SUBLANES
    last_tap_row = CF_PAD - CF_KERNEL // 2 + CF_KERNEL - 1
    span = slen + last_tap_row // SUBLANES * SUBLANES
    return pl.pallas_call(
        functools.partial(_conformer_kernel, nseq=nseq, slen=slen),
        grid=(bsz, seq // tb),
        in_specs=[pl.BlockSpec((1, tb, 2 * CF_CH), lambda b, i: (b, i, 0)),
                  pl.BlockSpec((kpad, CF_CH), lambda b, i: (0, 0)),
                  pl.BlockSpec((1, CF_CH), lambda b, i: (0, 0)),
                  pl.BlockSpec((1, CF_CH), lambda b, i: (0, 0)),
                  pl.BlockSpec((1, CF_CH), lambda b, i: (0, 0))],
        out_specs=pl.BlockSpec((1, tb, CF_CH), lambda b, i: (b, i, 0)),
        out_shape=jax.ShapeDtypeStruct((bsz, seq, CF_CH), BF16),
        scratch_shapes=[pltpu.VMEM((nseq, slen + 2 * CF_PAD, CF_CH), F32),
                        pltpu.VMEM((SUBLANES - 1, nseq, span, CF_CH), F32),
                        pltpu.VMEM((tb, CF_CH), F32)],
        compiler_params=_cparams(("parallel", "parallel")),
        name="conformer_conv",
    )(p_cf, jnp.pad(dw_w, ((0, kpad - CF_KERNEL), (0, 0))), dw_b.reshape(1, -1), ln_g.reshape(1, -1),
      ln_b.reshape(1, -1))


def _hgrn2_kernel(*refs, rev, mode):
    if mode == "state":
        v_ref, f_ref, lb_ref, s0_ref, sfin_ref, st_ref = refs
    elif mode == "out":
        q_ref, v_ref, f_ref, lb_ref, s0_ref, o_ref, st_ref = refs
    else:
        q_ref, v_ref, f_ref, lb_ref, s0_ref, oin_ref, g_ref, ng_ref, perm_ref, o_ref, st_ref, osc_ref = refs
    step = pl.program_id(1)
    ck = HG_CHUNK
    per_step = v_ref.shape[1] // ck

    @pl.when(step == 0)
    def _():
        st_ref[...] = s0_ref[0]

    lb = lb_ref[...]
    tri = _scan_tri(ck, rev).astype(BF16)
    last = 0 if rev else ck - 1
    mid_pos = HG_CHUNK // 2 - 1
    mid = (ck - 1 - mid_pos) if rev else mid_pos
    r = lax.broadcasted_iota(jnp.int32, (ck, ck), 0)
    c = lax.broadcasted_iota(jnp.int32, (ck, ck), 1)
    causal = (c >= r) if rev else (c <= r)

    heads = range(HG_HEADS)
    ks = [slice(h * HG_DK, (h + 1) * HG_DK) for h in heads]
    vs = [slice(h * HG_DV, (h + 1) * HG_DV) for h in heads]

    def prepare(ci):
        rows = slice(ci * ck, (ci + 1) * ck)
        f = lb + (1.0 - lb) * jax.nn.sigmoid(f_ref[0, rows, :])
        kk = 1.0 - f
        gcum = _dot01(tri, jnp.log(f))
        g_end = gcum[last:last + 1, :]
        p = dict(k_end=(kk * jnp.exp(g_end - gcum)).astype(BF16), dec_end=jnp.exp(g_end))
        if mode != "state":
            g_mid = gcum[mid:mid + 1, :]
            q = _silu(q_ref[0, rows, :].astype(F32))
            p.update(q_rel=(q * jnp.exp(gcum - g_mid)).astype(BF16), k_rel=(kk * jnp.exp(g_mid - gcum)).astype(BF16),
                     q_dec=(q * jnp.exp(gcum)).astype(BF16))
        if mode == "readout":
            p.update(gate=_silu(g_ref[0, rows, :].astype(F32)))
        return p

    order = list(range(per_step - 1, -1, -1) if rev else range(per_step))
    ready = prepare(order[0])
    for n, ci in enumerate(order):
        rows = slice(ci * ck, (ci + 1) * ck)
        p = ready
        k_end, dec_end = p["k_end"], p["dec_end"]
        v = v_ref[0, rows, :]
        if mode != "state":
            q_rel, k_rel, q_dec = p["q_rel"], p["k_rel"], p["q_dec"]
        if mode == "readout":
            gate = p["gate"]
        if n + 1 < len(order):
            ready = prepare(order[n + 1])

        s_in = [st_ref[h] for h in heads]
        if mode != "state":
            att = [lax.dot_general(q_rel[:, ks[h]], k_rel[:, ks[h]], _NT, preferred_element_type=F32) for h in heads]
            o_inter = [lax.dot_general(q_dec[:, ks[h]], s_in[h].astype(BF16), _NT, preferred_element_type=F32)
                       for h in heads]
            att = [jnp.where(causal, a, 0.0).astype(BF16) for a in att]
            o = [jnp.dot(att[h], v[:, vs[h]], preferred_element_type=F32) + o_inter[h] for h in heads]
        chunk_state = [lax.dot_general(v[:, vs[h]], k_end[:, ks[h]], _TN, preferred_element_type=F32) for h in heads]
        for h in heads:
            st_ref[h] = s_in[h] * dec_end[:, ks[h]] + chunk_state[h]
        if mode == "readout":
            for h in heads:
                o_h = o[h] + oin_ref[0, rows, vs[h]].astype(F32)
                ms = jnp.mean(o_h * o_h, axis=-1, keepdims=True)
                o_h = (o_h * lax.rsqrt(ms + EPS) * ng_ref[...]) * gate[:, vs[h]]
                osc_ref[rows, vs[h]] = o_h.astype(osc_ref.dtype)
        elif mode == "out":
            for h in heads:
                o_ref[0, rows, vs[h]] = o[h].astype(o_ref.dtype)

    if mode == "readout":
        nat = jnp.dot(perm_ref[...], osc_ref[...], preferred_element_type=F32).astype(o_ref.dtype)
        ncols = o_ref.shape[2]
        for gr in range(o_ref.shape[1]):
            o_ref[0, gr] = nat[gr * ncols:(gr + 1) * ncols, :]
    if mode == "state":
        @pl.when(step == pl.num_programs(1) - 1)
        def _():
            sfin_ref[0] = st_ref[...]


def _hgrn2_scan(v, f_raw, lb, s0, rev, mode, q=None, o_in=None, g=None, norm_g=None):
    bsz, seq, _ = v.shape
    tstep = min(seq, HG_STEP)
    nsteps = seq // tstep
    dcol = 1 if rev else 0

    def cidx(i):
        return (nsteps - 1 - i) if rev else i

    tok = lambda width: pl.BlockSpec((1, tstep, width), lambda b, i: (b, cidx(i), 0))
    state_spec = pl.BlockSpec((1, HG_HEADS, HG_DV, HG_DK), lambda b, i: (b, 0, 0, 0))
    f_spec = pl.BlockSpec((1, tstep, HG_KEY), lambda b, i: (b, cidx(i), dcol))
    lb_spec = pl.BlockSpec((1, HG_KEY), lambda b, i: (0, 0))
    state_shape = jax.ShapeDtypeStruct((bsz, HG_HEADS, HG_DV, HG_DK), F32)
    scratch = [pltpu.VMEM((HG_HEADS, HG_DV, HG_DK), F32)]
    if mode == "state":
        in_specs, args = [tok(HG_VAL), f_spec, lb_spec, state_spec], [v, f_raw, lb.reshape(1, -1), s0]
        out_specs, out_shape = state_spec, state_shape
    else:
        in_specs = [tok(HG_KEY), tok(HG_VAL), f_spec, lb_spec, state_spec]
        args = [q, v, f_raw, lb.reshape(1, -1), s0]
        out_specs, out_shape = tok(HG_VAL), jax.ShapeDtypeStruct((bsz, seq, HG_VAL), BF16)
        if mode == "readout":
            grid_rows = seq // GRID_W
            ncols = tstep // grid_rows
            tok_id = jnp.arange(tstep)
            src = (tok_id % ncols) * grid_rows + tok_id // ncols
            perm = (src[:, None] == tok_id[None, :]).astype(BF16)
            in_specs += [tok(HG_VAL), tok(HG_VAL), pl.BlockSpec((1, HG_DV), lambda b, i: (0, 0)),
                         pl.BlockSpec((tstep, tstep), lambda b, i: (0, 0))]
            args += [o_in, g, norm_g.reshape(1, -1), perm]
            out_specs = pl.BlockSpec((1, grid_rows, ncols, HG_VAL), lambda b, i: (b, 0, cidx(i), 0))
            out_shape = jax.ShapeDtypeStruct((bsz, grid_rows, GRID_W, HG_VAL), BF16)
            scratch.append(pltpu.VMEM((tstep, HG_VAL), BF16))
    out = pl.pallas_call(
        functools.partial(_hgrn2_kernel, rev=rev, mode=mode),
        grid=(bsz, nsteps),
        in_specs=in_specs, out_specs=out_specs, out_shape=out_shape,
        scratch_shapes=scratch,
        compiler_params=_cparams(("parallel", "arbitrary")),
        name=f"hgrn2_{mode}_{'rev' if rev else 'fwd'}",
    )(*args)
    return out.reshape(bsz, seq, HG_VAL) if mode == "readout" else out


def _first_argmax(vals):
    best, idx = vals[0], jnp.zeros(vals[0].shape, jnp.int32)
    for j in range(1, len(vals)):
        better = vals[j] > best
        idx = jnp.where(better, j, idx)
        best = jnp.where(better, vals[j], best)
    return idx, best


def _pick(idx, vals):
    out = vals[-1]
    for j in range(len(vals) - 2, -1, -1):
        out = jnp.where(idx == j, vals[j], out)
    return out


def _store_token_tiles(ref, v):
    n = v.shape[0]
    for j in range(SUBLANES):
        ref[pl.ds(j, n, stride=SUBLANES), :] = v[:, j * LANES:(j + 1) * LANES]


def _load_token_tiles(ref):
    n = ref.shape[0] // SUBLANES
    return jnp.concatenate([ref[pl.ds(j, n, stride=SUBLANES), :] for j in range(SUBLANES)], axis=1)


def _token_tile(ref, idx):
    return ref.at[pl.ds(pl.multiple_of(idx * SUBLANES, SUBLANES), SUBLANES)]


def _router_kernel(x_ref, g_ref, sh_ref, sc_ref, rw_ref, rb_ref, h_ref, slot_ref, wt_ref, cnt_ref, carry_ref):
    step = pl.program_id(0)

    @pl.when(step == 0)
    def _():
        carry_ref[...] = jnp.zeros_like(carry_ref)

    h = _normmod(x_ref[...], g_ref[...], sh_ref[0], sc_ref[0])
    _store_token_tiles(h_ref, h)
    scores = jax.nn.sigmoid(_dot_f32(rw_ref[...], h, _NT))
    sel = scores + rb_ref[...]
    srow = [sel[e:e + 1, :] for e in range(N_EXPERTS)]
    prow = [scores[e:e + 1, :] for e in range(N_EXPERTS)]
    gscore = []
    for gi in range(N_EXPERT_GROUPS):
        m = srow[gi * EXPERTS_PER_GROUP:(gi + 1) * EXPERTS_PER_GROUP]
        pair_sums = [m[i] + m[j] for i in range(EXPERTS_PER_GROUP) for j in range(i + 1, EXPERTS_PER_GROUP)]
        best = pair_sums[0]
        for p in pair_sums[1:]:
            best = jnp.maximum(best, p)
        gscore.append(best)
    gidx, _ = _first_argmax(gscore)
    in_sel = [_pick(gidx, [srow[gi * EXPERTS_PER_GROUP + j] for gi in range(N_EXPERT_GROUPS)])
              for j in range(EXPERTS_PER_GROUP)]
    in_p = [_pick(gidx, [prow[gi * EXPERTS_PER_GROUP + j] for gi in range(N_EXPERT_GROUPS)])
            for j in range(EXPERTS_PER_GROUP)]
    i1, _ = _first_argmax(in_sel)
    i2, _ = _first_argmax([jnp.where(i1 == j, -jnp.inf, in_sel[j]) for j in range(EXPERTS_PER_GROUP)])
    w1, w2 = _pick(i1, in_p), _pick(i2, in_p)
    den = w1 + w2
    e1, e2 = gidx * EXPERTS_PER_GROUP + i1, gidx * EXPERTS_PER_GROUP + i2
    tl = h.shape[0]
    erow = lax.broadcasted_iota(jnp.int32, (N_EXPERTS, tl), 0)
    oh1, oh2 = erow == e1, erow == e2
    cnt = (oh1 | oh2).astype(F32)
    r = lax.broadcasted_iota(jnp.int32, (tl, tl), 0)
    c = lax.broadcasted_iota(jnp.int32, (tl, tl), 1)
    before = jnp.dot(cnt.astype(BF16), (r < c).astype(BF16), preferred_element_type=F32)
    base = carry_ref[:, 0:1] + before
    rank1 = jnp.sum(jnp.where(oh1, base, 0.0), axis=0, keepdims=True).astype(jnp.int32)
    rank2 = jnp.sum(jnp.where(oh2, base, 0.0), axis=0, keepdims=True).astype(jnp.int32)
    carry_ref[...] = carry_ref[...] + jnp.sum(cnt, axis=1, keepdims=True)
    cnt_ref[...] = carry_ref[...].astype(jnp.int32)
    srow = lax.broadcasted_iota(jnp.int32, (SUBLANES, tl), 0)
    slot_ref[0] = jnp.where(srow == 0, e1, jnp.where(srow == 1, e2, jnp.where(srow == 2, rank1, rank2)))
    wrow = lax.broadcasted_iota(jnp.int32, (LANES, tl), 0)
    wt_ref[...] = jnp.where(wrow == 0, w1 / den, jnp.where(wrow == 1, w2 / den, 0.0)).T


def _ffn_norm_router(x, g, shift, scale, router_w, router_b, seq):
    t, d = x.shape
    tl = min(seq, MOE_TL)
    nblk = t // tl
    nb = shift.shape[0]
    mod_spec = pl.BlockSpec((1, 1, d), (lambda i: (i * tl // seq, 0, 0)) if nb > 1 else (lambda i: (0, 0, 0)))
    return pl.pallas_call(
        _router_kernel,
        grid=(nblk,),
        in_specs=[pl.BlockSpec((tl, d), lambda i: (i, 0)),
                  pl.BlockSpec((1, d), lambda i: (0, 0)),
                  mod_spec, mod_spec,
                  pl.BlockSpec((N_EXPERTS, d), lambda i: (0, 0)),
                  pl.BlockSpec((N_EXPERTS, 1), lambda i: (0, 0))],
        out_specs=[pl.BlockSpec((tl * SUBLANES, LANES), lambda i: (i, 0)),
                   pl.BlockSpec((1, SUBLANES, tl), lambda i: (i, 0, 0)),
                   pl.BlockSpec((tl, LANES), lambda i: (i, 0)),
                   pl.BlockSpec((N_EXPERTS, LANES), lambda i: (0, 0))],
        out_shape=[jax.ShapeDtypeStruct((t * SUBLANES, LANES), F32),
                   jax.ShapeDtypeStruct((nblk, SUBLANES, tl), jnp.int32),
                   jax.ShapeDtypeStruct((t, LANES), F32),
                   jax.ShapeDtypeStruct((N_EXPERTS, LANES), jnp.int32)],
        scratch_shapes=[pltpu.VMEM((N_EXPERTS, LANES), F32)],
        compiler_params=_cparams(("arbitrary",)),
        name="ffn_norm_router",
    )(x, g.reshape(1, d), shift, scale, router_w.T, router_b.reshape(N_EXPERTS, 1))


DMA_UNROLL = 8


def _dispatch_kernel(slot_ref, ends_ref, h_ref, hs_ref, zero_ref, sem, *, tl, max_tiles):
    @pl.when(pl.program_id(0) == 0)
    def _():
        zero_ref[...] = jnp.zeros_like(zero_ref)
        fills = []
        for e in range(N_EXPERTS):
            end = ends_ref[e]
            gap = (MOE_TM - (end & (MOE_TM - 1))) & (MOE_TM - 1)
            for bit in range(MOE_TM.bit_length() - 1):
                size = 1 << bit
                first_row = pl.multiple_of((end + (gap & (size - 1))) * SUBLANES, SUBLANES)
                fills.append(((gap & size) != 0,
                              pltpu.make_async_copy(zero_ref.at[pl.ds(0, size * SUBLANES)],
                                                    hs_ref.at[pl.ds(first_row, size * SUBLANES)], sem)))
        for n in range(N_EXPERTS):
            first_slot = ends_ref[N_EXPERTS] + n * MOE_TM
            first_row = pl.multiple_of(first_slot * SUBLANES, SUBLANES)
            fills.append((first_slot < max_tiles * MOE_TM,
                          pltpu.make_async_copy(zero_ref, hs_ref.at[pl.ds(first_row, MOE_TM * SUBLANES)], sem)))
        for cond, cp in fills:
            pl.when(cond)(cp.start)
        for cond, cp in fills:
            pl.when(cond)(cp.wait)

    def issue(t, carry):
        pltpu.make_async_copy(_token_tile(h_ref, t), _token_tile(hs_ref, slot_ref[0, 0, t]), sem).start()
        pltpu.make_async_copy(_token_tile(h_ref, t), _token_tile(hs_ref, slot_ref[0, 0, tl + t]), sem).start(priority=1)
        return carry

    lax.fori_loop(0, tl, issue, 0, unroll=DMA_UNROLL)
    for _ in range(2):
        pltpu.make_async_copy(h_ref, hs_ref.at[pl.ds(0, tl * SUBLANES)], sem).wait()


def _dispatch(h, slot2, ends, tl, max_tiles):
    t = h.shape[0] // SUBLANES
    nblk = t // tl
    return pl.pallas_call(
        functools.partial(_dispatch_kernel, tl=tl, max_tiles=max_tiles),
        grid=(nblk,),
        in_specs=[pl.BlockSpec((1, 1, 2 * tl), lambda i: (i, 0, 0), memory_space=pltpu.SMEM),
                  pl.BlockSpec(memory_space=pltpu.SMEM),
                  pl.BlockSpec((tl * SUBLANES, LANES), lambda i: (i, 0))],
        out_specs=pl.BlockSpec(memory_space=pl.ANY),
        out_shape=jax.ShapeDtypeStruct((max_tiles * MOE_TM * SUBLANES, LANES), F32),
        scratch_shapes=[pltpu.VMEM((MOE_TM * SUBLANES, LANES), F32), pltpu.SemaphoreType.DMA(())],
        compiler_params=_cparams(("arbitrary",)),
        name="moe_dispatch",
    )(slot2, ends, h)


def _expert_kernel(te_ref, tf_ref, nt_ref, hs_ref, wg_ref, wu_ref, wd_ref, ys_ref, wgb_ref, wub_ref, wdb_ref):
    n = pl.program_id(0)

    @pl.when(n >= nt_ref[0])
    def _():
        ys_ref[...] = jnp.zeros_like(ys_ref)

    @pl.when(n < nt_ref[0])
    def _():
        @pl.when(tf_ref[n] == 1)
        def _():
            wgb_ref[...] = wg_ref[0].astype(BF16)
            wub_ref[...] = wu_ref[0].astype(BF16)
            wdb_ref[...] = wd_ref[0].astype(BF16)

        h = _load_token_tiles(hs_ref).astype(BF16)
        a = jnp.dot(h, wgb_ref[...], preferred_element_type=F32)
        b = jnp.dot(h, wub_ref[...], preferred_element_type=F32)
        y = jnp.dot((_silu(a) * b).astype(BF16), wdb_ref[...], preferred_element_type=F32)
        _store_token_tiles(ys_ref, y)


def _experts(hs, tile_expert, tile_first, n_tiles, w_gate, w_up, w_down, layer, max_tiles):
    d = w_gate.shape[2]
    tile_spec = pl.BlockSpec((MOE_TM * SUBLANES, LANES), lambda n, te, tf, nt: (n, 0))
    grid_spec = pltpu.PrefetchScalarGridSpec(
        num_scalar_prefetch=3,
        grid=(max_tiles,),
        in_specs=[tile_spec,
                  pl.BlockSpec((None, 1, d, D_EXPERT), lambda n, te, tf, nt: (layer, te[n], 0, 0)),
                  pl.BlockSpec((None, 1, d, D_EXPERT), lambda n, te, tf, nt: (layer, te[n], 0, 0)),
                  pl.BlockSpec((None, 1, D_EXPERT, d), lambda n, te, tf, nt: (layer, te[n], 0, 0))],
        out_specs=tile_spec,
        scratch_shapes=[pltpu.VMEM((d, D_EXPERT), BF16), pltpu.VMEM((d, D_EXPERT), BF16),
                        pltpu.VMEM((D_EXPERT, d), BF16)])
    return pl.pallas_call(
        _expert_kernel,
        grid_spec=grid_spec,
        out_shape=jax.ShapeDtypeStruct((max_tiles * MOE_TM * SUBLANES, LANES), F32),
        compiler_params=_cparams(("arbitrary",)),
        name="moe_experts",
    )(tile_expert, tile_first, n_tiles, hs, w_gate, w_up, w_down)


def _combine_kernel(slot_ref, nslot_ref, wt_ref, res_ref, gate_ref, fg_ref, ys_ref, o_ref, y1_ref, y2_ref, sems, *,
                    tl, final):
    step, nsteps = pl.program_id(0), pl.num_programs(0)
    cur = step % 2

    def start_gather(sref, buf):
        def issue(t, carry):
            pltpu.make_async_copy(_token_tile(ys_ref, sref[0, 0, t]), _token_tile(y1_ref.at[buf], t),
                                  sems.at[buf]).start()
            pltpu.make_async_copy(_token_tile(ys_ref, sref[0, 0, tl + t]), _token_tile(y2_ref.at[buf], t),
                                  sems.at[buf]).start(priority=1)
            return carry

        lax.fori_loop(0, tl, issue, 0, unroll=DMA_UNROLL)

    @pl.when(step == 0)
    def _():
        start_gather(slot_ref, 0)

    @pl.when(step + 1 < nsteps)
    def _():
        start_gather(nslot_ref, 1 - cur)

    pltpu.make_async_copy(ys_ref.at[pl.ds(0, tl * SUBLANES)], y1_ref.at[cur], sems.at[cur]).wait()
    pltpu.make_async_copy(ys_ref.at[pl.ds(0, tl * SUBLANES)], y2_ref.at[cur], sems.at[cur]).wait()
    wt = wt_ref[...]
    moe = wt[:, 0:1] * _load_token_tiles(y1_ref.at[cur]) + wt[:, 1:2] * _load_token_tiles(y2_ref.at[cur])
    x = res_ref[...] + gate_ref[0] * moe
    if final:
        ms = jnp.mean(x * x, axis=-1, keepdims=True)
        x = x * lax.rsqrt(ms + EPS) * fg_ref[...]
    o_ref[...] = x


def _combine(ys, slot2, wt, res, gate, final_g, seq, tl):
    t, d = res.shape
    nblk = t // tl
    nb = gate.shape[0]
    final = final_g is not None
    fg = final_g.reshape(1, d) if final else jnp.ones((1, d), F32)
    return pl.pallas_call(
        functools.partial(_combine_kernel, tl=tl, final=final),
        grid=(nblk,),
        in_specs=[pl.BlockSpec((1, 1, 2 * tl), lambda i: (i, 0, 0), memory_space=pltpu.SMEM),
                  pl.BlockSpec((1, 1, 2 * tl), lambda i: (jnp.minimum(i + 1, nblk - 1), 0, 0),
                               memory_space=pltpu.SMEM),
                  pl.BlockSpec((tl, LANES), lambda i: (i, 0)),
                  pl.BlockSpec((tl, d), lambda i: (i, 0)),
                  pl.BlockSpec((1, 1, d), (lambda i: (i * tl // seq, 0, 0)) if nb > 1 else (lambda i: (0, 0, 0))),
                  pl.BlockSpec((1, d), lambda i: (0, 0)),
                  pl.BlockSpec(memory_space=pl.ANY)],
        out_specs=pl.BlockSpec((tl, d), lambda i: (i, 0)),
        out_shape=jax.ShapeDtypeStruct((t, d), F32),
        scratch_shapes=[pltpu.VMEM((2, tl * SUBLANES, LANES), F32), pltpu.VMEM((2, tl * SUBLANES, LANES), F32),
                        pltpu.SemaphoreType.DMA((2,))],
        compiler_params=_cparams(("arbitrary",)),
        name="moe_combine",
    )(slot2, slot2, wt, res, gate, fg, ys)


def _even_layer_mixer(xs, mods, norm_g, w_in, conv_w, conv_b, dt_bias, a_log, d_skip, ssd_g,
                      cf_w, cf_b, cf_lng, cf_lnb, w_out, rows):
    s_lo = SSD_INNER
    dt_lo = SSD_INNER + SSD_CONV_DIM
    cf_lo = dt_lo + 2 * SSD_HEADS
    w_cf = _realign_cols(w_in, cf_lo, 2 * CF_CH)
    bsz = xs[0].shape[0]
    zero_state = jnp.zeros((bsz, SSD_GROUPS, SSD_STATE, SSD_GROUP_W), F32)
    proj = []
    for x, m in zip(xs, mods):
        h = _norm_mod(x, norm_g, m[0], m[1])
        proj.append(dict(z=_matmul(h, w_in, BF16, 0, s_lo), xbc=_matmul(h, w_in, BF16, s_lo, SSD_CONV_DIM),
                         dt=_matmul(h, w_in, F32, dt_lo, LANES), cf=_matmul(h, w_cf, BF16)))
    y_fwd, act, o_ssd = [None, None], [None, None], [None, None]
    state = zero_state
    for si in range(2):
        p = proj[si]
        y_fwd[si], act[si], state = _ssd_scan(p["xbc"], p["dt"], dt_bias[0], a_log[0], state, False,
                                              conv_w=conv_w, conv_b=conv_b, d_skip=d_skip)
    state = zero_state
    for si in range(2):
        p = proj[si]
        o_ssd[si], state = _ssd_scan(act[si], p["dt"], dt_bias[1], a_log[1], state, True,
                                     y_in=y_fwd[si], z=p["z"], norm_g=ssd_g)
    outs = []
    for si, (x, m) in enumerate(zip(xs, mods)):
        slen = x.shape[1] if si == 0 else GRID_W
        o_cf = _conformer(proj[si]["cf"], cf_w, cf_b, cf_lng, cf_lnb, slen)
        outs.append(_matmul_res2(o_ssd[si], o_cf, w_out, x, m[2]))
    return outs


def _odd_layer_mixer(xc, xl, m_c, m_l, norm_g, w_in, lb, hg_g, w_out):
    st0 = HG_KEY + HG_VAL
    f0 = st0 + HG_VAL
    bsz = xl.shape[0]
    h_c = _norm_mod(xc, norm_g, m_c[0], m_c[1])
    h_l = _norm_mod_colmajor(xl, norm_g, m_l[0], m_l[1])
    v_c, f_c = _matmul(h_c, w_in, BF16, st0, HG_VAL), _matmul(h_c, w_in, F32, f0, 2 * HG_KEY)
    q_l, g_l = _matmul(h_l, w_in, BF16, 0, HG_KEY), _matmul(h_l, w_in, BF16, HG_KEY, HG_VAL)
    v_l, f_l = _matmul(h_l, w_in, BF16, st0, HG_VAL), _matmul(h_l, w_in, F32, f0, 2 * HG_KEY)
    zero_state = jnp.zeros((bsz, HG_HEADS, HG_DV, HG_DK), F32)
    s_f = _hgrn2_scan(v_c, f_c, lb, zero_state, False, "state")
    s_r = _hgrn2_scan(v_c, f_c, lb, zero_state, True, "state")
    o_f = _hgrn2_scan(v_l, f_l, lb, s_f, False, "out", q=q_l)
    o = _hgrn2_scan(v_l, f_l, lb, s_r, True, "readout", q=q_l, o_in=o_f, g=g_l, norm_g=hg_g)
    return _matmul_res(o, w_out, xl, m_l[2])


def _dispatch_layout(counts, slot, max_tiles):
    tiles = (counts + MOE_TM - 1) // MOE_TM
    tile_ends = jnp.cumsum(tiles)
    tile_starts = tile_ends - tiles
    total = tile_ends[-1]
    starts = tile_starts * MOE_TM
    ends = jnp.concatenate([starts + counts, (total * MOE_TM).reshape(1)]).astype(jnp.int32)
    experts = jnp.arange(N_EXPERTS, dtype=jnp.int32)

    def lookup(table, idx):
        return jnp.sum(jnp.where(idx[..., None] == experts, table, 0), axis=-1)

    slots = jnp.concatenate([lookup(starts, slot[:, 0]) + slot[:, 2], lookup(starts, slot[:, 1]) + slot[:, 3]],
                            axis=-1)
    n = jnp.arange(max_tiles, dtype=jnp.int32)
    nn = jnp.minimum(n, total - 1)
    expert = jnp.sum((nn[:, None] >= tile_ends[None, :]).astype(jnp.int32), axis=1)
    is_first = ((nn == lookup(tile_starts, expert)) & (n < total)).astype(jnp.int32)
    return slots[:, None, :].astype(jnp.int32), ends, expert, is_first, total.reshape(1).astype(jnp.int32)


def _moe_block(x, m, norm_g, router_w, router_b, w_gate, w_up, w_down, layer, final_g=None):
    bsz, seq, d = x.shape
    t = bsz * seq
    xf = x.reshape(t, d)
    tl = min(seq, MOE_TL)
    h, slot, wt, cnt = _ffn_norm_router(xf, norm_g, m[3], m[4], router_w, router_b, seq)
    max_tiles = 2 * t // MOE_TM + N_EXPERTS
    slot2, ends, tile_expert, tile_first, n_tiles = _dispatch_layout(cnt[:, 0], slot, max_tiles)
    hs = _dispatch(h, slot2, ends, tl, max_tiles)
    ys = _experts(hs, tile_expert, tile_first, n_tiles, w_gate, w_up, w_down, layer, max_tiles)
    return _combine(ys, slot2, wt, xf, m[5], final_g, seq, tl).reshape(bsz, seq, d)


def kernel(x, c, ctx, c_ctx, mod_w, mod_b, norm_mix_g, norm_ffn_g, router_w, router_b, moe_w_gate, moe_w_up,
           moe_w_down, ab_w_in, ssd_conv_w, ssd_conv_b, ssd_dt_bias, ssd_a_log, ssd_d, ssd_norm_g, cf_dw_w,
           cf_dw_b, cf_ln_g, cf_ln_b, ab_w_out, hg_w_in, hg_lb, hg_norm_g, hg_w_out, final_norm_g):
    depth = mod_w.shape[0]
    assert depth == 2, "layer schedule below is written for one even and one odd layer"
    bsz, seq, d = x.shape
    rows = seq // GRID_W
    lb_all = jnp.cumsum(jax.nn.softmax(hg_lb.astype(F32), axis=0), axis=0)
    lb_all = lb_all - lb_all[0]

    nrow = -(-(bsz + 1) // SUBLANES) * SUBLANES
    cond = jnp.zeros((nrow, d), F32).at[:bsz].set(c).at[bsz].set(c_ctx)
    mod = _modulation(cond, mod_w, mod_b).reshape(depth, nrow, N_MOD, 1, d)

    def mods(l):
        m_l = [mod[l, :bsz, k] for k in range(N_MOD)]
        m_c = [mod[l, bsz:bsz + 1, k] for k in range(N_MOD)]
        return m_c, m_l

    m_c, m_l = mods(0)
    xc, xl = _even_layer_mixer((ctx, x), (m_c, m_l), norm_mix_g[0], ab_w_in[0], ssd_conv_w[0], ssd_conv_b[0],
                               ssd_dt_bias[0], ssd_a_log[0], ssd_d[0], ssd_norm_g[0], cf_dw_w[0], cf_dw_b[0],
                               cf_ln_g[0], cf_ln_b[0], ab_w_out[0], rows)
    xl = _moe_block(xl, m_l, norm_ffn_g[0], router_w, router_b, moe_w_gate, moe_w_up, moe_w_down, 0)
    xc = _moe_block(xc, m_c, norm_ffn_g[0], router_w, router_b, moe_w_gate, moe_w_up, moe_w_down, 0)

    m_c, m_l = mods(1)
    xl = _odd_layer_mixer(xc, xl, m_c, m_l, norm_mix_g[1], hg_w_in[0], lb_all[1], hg_norm_g[0], hg_w_out[0])
    return _moe_block(xl, m_l, norm_ffn_g[1], router_w, router_b, moe_w_gate, moe_w_up, moe_w_down, 1,
                      final_g=final_norm_g)
```

```python
import functools

import jax
import jax.numpy as jnp
from jax import lax
from jax.experimental import pallas as pl
from jax.experimental.pallas import tpu as pltpu

F32 = jnp.float32
BF16 = jnp.bfloat16

D_MODEL = 1024
GRID_W = 64
EPS = 1e-6
N_MOD = 6

SSD_HEADS = 16
SSD_HEAD_DIM = 64
SSD_INNER = SSD_HEADS * SSD_HEAD_DIM
SSD_GROUPS = 4
SSD_STATE = 128
SSD_CONV = 5
SSD_CHUNK = 128
SSD_STEP = 512
SSD_BC = SSD_GROUPS * SSD_STATE
SSD_CONV_DIM = SSD_INNER + 2 * SSD_BC
SSD_GROUP_W = SSD_INNER // SSD_GROUPS

CF_CH = 1024
CF_KERNEL = 31
CF_PAD = 16

HG_HEADS = 8
HG_DK = 128
HG_DV = 128
HG_KEY = HG_HEADS * HG_DK
HG_VAL = HG_HEADS * HG_DV
HG_CHUNK = 64
HG_STEP = 512

N_EXPERTS = 16
N_EXPERT_GROUPS = 4
EXPERTS_PER_GROUP = 4
D_EXPERT = 512
MOE_TL = 512
MOE_TL_MOVE = 1024
MOE_TM = 512

LANES = 128
SUBLANES = 8
HALO = 16
CM_COLS = 16
VMEM_LIMIT = 48 * 1024 * 1024


def _cparams(sem):
    return pltpu.CompilerParams(dimension_semantics=sem, vmem_limit_bytes=VMEM_LIMIT)


def _silu(x):
    return x * jax.nn.sigmoid(x)


def _split3(v):
    hi = v.astype(BF16)
    r1 = v - hi.astype(F32)
    mid = r1.astype(BF16)
    lo = (r1 - mid.astype(F32)).astype(BF16)
    return hi, mid, lo


def _dot01(m01, v):
    hi, mid, lo = _split3(v)
    out = jnp.dot(m01, lo, preferred_element_type=F32)
    out = out + jnp.dot(m01, mid, preferred_element_type=F32)
    return out + jnp.dot(m01, hi, preferred_element_type=F32)


def _dot_v01(v, m01):
    hi, mid, lo = _split3(v)
    out = jnp.dot(lo, m01, preferred_element_type=F32)
    out = out + jnp.dot(mid, m01, preferred_element_type=F32)
    return out + jnp.dot(hi, m01, preferred_element_type=F32)


_NN = (((1,), (0,)), ((), ()))
_NT = (((1,), (1,)), ((), ()))
_TN = (((0,), (0,)), ((), ()))


def _dot_f32(a, b, dn):
    a1, a2, a3 = _split3(a)
    b1, b2, b3 = _split3(b)
    out = lax.dot_general(a3, b1, dn, preferred_element_type=F32)
    out = out + lax.dot_general(a1, b3, dn, preferred_element_type=F32)
    out = out + lax.dot_general(a2, b2, dn, preferred_element_type=F32)
    out = out + lax.dot_general(a2, b1, dn, preferred_element_type=F32)
    out = out + lax.dot_general(a1, b2, dn, preferred_element_type=F32)
    return out + lax.dot_general(a1, b1, dn, preferred_element_type=F32)


def _scan_tri(n, rev):
    r = lax.broadcasted_iota(jnp.int32, (n, n), 0)
    c = lax.broadcasted_iota(jnp.int32, (n, n), 1)
    return ((c >= r) if rev else (c <= r))


def _mod_kernel(c_ref, w_ref, b_ref, o_ref):
    cond = _silu(c_ref[...])
    o_ref[0] = _dot_f32(cond, w_ref[0], _NN) + b_ref[0]


def _modulation(cond, mod_w, mod_b):
    depth, d, n = mod_w.shape
    r = cond.shape[0]
    tn = 512
    return pl.pallas_call(
        _mod_kernel,
        grid=(depth, n // tn),
        in_specs=[pl.BlockSpec((r, d), lambda l, j: (0, 0)),
                  pl.BlockSpec((1, d, tn), lambda l, j: (l, 0, j)),
                  pl.BlockSpec((1, 1, tn), lambda l, j: (l, 0, j))],
        out_specs=pl.BlockSpec((1, r, tn), lambda l, j: (l, 0, j)),
        out_shape=jax.ShapeDtypeStruct((depth, r, n), F32),
        compiler_params=_cparams(("parallel", "parallel")),
        name="modulation",
    )(cond, mod_w, mod_b.reshape(depth, 1, n))


def _normmod(x, g, shift, scale):
    ms = jnp.mean(x * x, axis=-1, keepdims=True)
    return (x * lax.rsqrt(ms + EPS) * g) * (1.0 + scale) + shift


def _normmod_kernel(x_ref, g_ref, sh_ref, sc_ref, o_ref):
    o_ref[0] = _normmod(x_ref[0], g_ref[...], sh_ref[0], sc_ref[0]).astype(o_ref.dtype)


def _mod_spec(nb):
    return pl.BlockSpec((1, 1, D_MODEL), (lambda b, *_: (b, 0, 0)) if nb > 1 else (lambda b, *_: (0, 0, 0)))


def _norm_mod(x, g, shift, scale):
    bsz, seq, d = x.shape
    tl = min(seq, 512)
    return pl.pallas_call(
        _normmod_kernel,
        grid=(bsz, seq // tl),
        in_specs=[pl.BlockSpec((1, tl, d), lambda b, i: (b, i, 0)),
                  pl.BlockSpec((1, d), lambda b, i: (0, 0)),
                  _mod_spec(shift.shape[0]), _mod_spec(scale.shape[0])],
        out_specs=pl.BlockSpec((1, tl, d), lambda b, i: (b, i, 0)),
        out_shape=jax.ShapeDtypeStruct((bsz, seq, d), BF16),
        compiler_params=_cparams(("parallel", "parallel")),
        name="norm_mod",
    )(x, g.reshape(1, d), shift, scale)


def _normmod_cm_kernel(x_ref, g_ref, sh_ref, sc_ref, o_ref, h_ref, *, rows):
    h = _normmod(x_ref[0], g_ref[...], sh_ref[0], sc_ref[0])
    nlt = h.shape[-1] // LANES
    for r in range(rows):
        for j in range(nlt):
            h_ref[j, r * CM_COLS:(r + 1) * CM_COLS, :] = h[r, :, j * LANES:(j + 1) * LANES]
    for c in range(CM_COLS):
        for j in range(nlt):
            piece = h_ref[j, pl.ds(c, rows, stride=CM_COLS), :]
            o_ref[0, c * rows:(c + 1) * rows, j * LANES:(j + 1) * LANES] = piece.astype(o_ref.dtype)


def _norm_mod_colmajor(x, g, shift, scale):
    bsz, seq, d = x.shape
    rows = seq // GRID_W
    x4 = x.reshape(bsz, rows, GRID_W, d)
    return pl.pallas_call(
        functools.partial(_normmod_cm_kernel, rows=rows),
        grid=(bsz, GRID_W // CM_COLS),
        in_specs=[pl.BlockSpec((1, rows, CM_COLS, d), lambda b, i: (b, 0, i, 0)),
                  pl.BlockSpec((1, d), lambda b, i: (0, 0)),
                  _mod_spec(shift.shape[0]), _mod_spec(scale.shape[0])],
        out_specs=pl.BlockSpec((1, CM_COLS * rows, d), lambda b, i: (b, i, 0)),
        out_shape=jax.ShapeDtypeStruct((bsz, seq, d), BF16),
        scratch_shapes=[pltpu.VMEM((d // LANES, CM_COLS * rows, LANES), F32)],
        compiler_params=_cparams(("parallel", "parallel")),
        name="norm_mod_colmajor",
    )(x4, g.reshape(1, d), shift, scale)


MM_SEMANTICS = ("arbitrary", "arbitrary", "arbitrary")


def _stage_weights(b_ref, wb_ref):
    @pl.when((pl.program_id(1) == 0) & (pl.program_id(2) == 0))
    def _():
        wb_ref[...] = b_ref[...].astype(BF16)


def _mm_kernel(a_ref, b_ref, o_ref, wb_ref):
    _stage_weights(b_ref, wb_ref)
    o_ref[0] = jnp.dot(a_ref[0], wb_ref[...], preferred_element_type=F32).astype(o_ref.dtype)


def _mm_res2_kernel(a1_ref, a2_ref, b1_ref, b2_ref, res_ref, gate_ref, o_ref, wb1_ref, wb2_ref):
    _stage_weights(b1_ref, wb1_ref)
    _stage_weights(b2_ref, wb2_ref)
    y = jnp.dot(a1_ref[0], wb1_ref[...], preferred_element_type=F32)
    y = y + jnp.dot(a2_ref[0], wb2_ref[...], preferred_element_type=F32)
    o_ref[0] = res_ref[0] + gate_ref[0] * y


def _mm_res_kernel(a_ref, b_ref, res_ref, gate_ref, o_ref, wb_ref):
    _stage_weights(b_ref, wb_ref)
    o_ref[0] = res_ref[0] + gate_ref[0] * jnp.dot(a_ref[0], wb_ref[...], preferred_element_type=F32)


def _matmul(a, w, out_dtype, col0=0, ncols=None):
    bsz, seq, k = a.shape
    n = w.shape[1] - col0 if ncols is None else ncols
    tm, tn = min(seq, 1024), min(n, 1024)
    assert col0 % tn == 0 and n % tn == 0
    jb = col0 // tn
    return pl.pallas_call(
        _mm_kernel,
        grid=(n // tn, bsz, seq // tm),
        in_specs=[pl.BlockSpec((1, tm, k), lambda j, b, i: (b, i, 0)),
                  pl.BlockSpec((k, tn), lambda j, b, i: (0, j + jb))],
        out_specs=pl.BlockSpec((1, tm, tn), lambda j, b, i: (b, i, j)),
        out_shape=jax.ShapeDtypeStruct((bsz, seq, n), out_dtype),
        scratch_shapes=[pltpu.VMEM((k, tn), BF16)],
        compiler_params=_cparams(MM_SEMANTICS),
        name="matmul",
    )(a, w)


def _matmul_res2(a1, a2, w, res, gate):
    bsz, seq, kh = a1.shape
    n = w.shape[1]
    tm, tn = min(seq, 1024), min(n, 512)
    nb = gate.shape[0]
    return pl.pallas_call(
        _mm_res2_kernel,
        grid=(n // tn, bsz, seq // tm),
        in_specs=[pl.BlockSpec((1, tm, kh), lambda j, b, i: (b, i, 0)),
                  pl.BlockSpec((1, tm, kh), lambda j, b, i: (b, i, 0)),
                  pl.BlockSpec((kh, tn), lambda j, b, i: (0, j)),
                  pl.BlockSpec((kh, tn), lambda j, b, i: (1, j)),
                  pl.BlockSpec((1, tm, tn), lambda j, b, i: (b, i, j)),
                  pl.BlockSpec((1, 1, tn), (lambda j, b, i: (b, 0, j)) if nb > 1 else (lambda j, b, i: (0, 0, j)))],
        out_specs=pl.BlockSpec((1, tm, tn), lambda j, b, i: (b, i, j)),
        out_shape=jax.ShapeDtypeStruct((bsz, seq, n), F32),
        scratch_shapes=[pltpu.VMEM((kh, tn), BF16), pltpu.VMEM((kh, tn), BF16)],
        compiler_params=_cparams(MM_SEMANTICS),
        name="matmul_res",
    )(a1, a2, w, w, res, gate)


def _realign_kernel(w_ref, o_ref, *, col0):
    o_ref[...] = w_ref[:, col0:col0 + o_ref.shape[1]].astype(o_ref.dtype)


def _realign_cols(w, col0, ncols):
    k, n = w.shape
    tk = 128
    return pl.pallas_call(
        functools.partial(_realign_kernel, col0=col0),
        grid=(k // tk,),
        in_specs=[pl.BlockSpec((tk, n), lambda i: (i, 0))],
        out_specs=pl.BlockSpec((tk, ncols), lambda i: (i, 0)),
        out_shape=jax.ShapeDtypeStruct((k, ncols), BF16),
        compiler_params=_cparams(("parallel",)),
        name="realign_cols",
    )(w)


def _matmul_res(a, w, res, gate):
    bsz, seq, k = a.shape
    n = w.shape[1]
    tm, tn = min(seq, 1024), min(n, 512)
    return pl.pallas_call(
        _mm_res_kernel,
        grid=(n // tn, bsz, seq // tm),
        in_specs=[pl.BlockSpec((1, tm, k), lambda j, b, i: (b, i, 0)),
                  pl.BlockSpec((k, tn), lambda j, b, i: (0, j)),
                  pl.BlockSpec((1, tm, tn), lambda j, b, i: (b, i, j)),
                  pl.BlockSpec((1, 1, tn), lambda j, b, i: (b, 0, j))],
        out_specs=pl.BlockSpec((1, tm, tn), lambda j, b, i: (b, i, j)),
        out_shape=jax.ShapeDtypeStruct((bsz, seq, n), F32),
        scratch_shapes=[pltpu.VMEM((k, tn), BF16)],
        compiler_params=_cparams(MM_SEMANTICS),
        name="matmul_res1",
    )(a, w, res, gate)


def _ssd_kernel(*refs, rev, nsteps, readout):
    if readout:
        act_ref, dt_ref, dtb_ref, alog_ref, s0_ref, yin_ref, z_ref, ng_ref, y_ref, sfin_ref, st_ref = refs
    else:
        (xbc_ref, prev_ref, next_ref, dt_ref, cw_ref, cb_ref, dtb_ref, alog_ref, dsk_ref, s0_ref,
         y_ref, act_ref, sfin_ref, pad_ref, st_ref) = refs
    step = pl.program_id(1)
    u = (nsteps - 1 - step) if rev else step
    ck = SSD_CHUNK
    tstep = dt_ref.shape[1]
    per_step = tstep // ck
    hoff = SSD_HEADS if rev else 0

    @pl.when(step == 0)
    def _():
        st_ref[...] = s0_ref[0]

    if not readout:
        pad_ref[HALO:HALO + tstep, :] = xbc_ref[0].astype(F32)
        pad_ref[0:HALO, :] = jnp.where(u > 0, prev_ref[0].astype(F32), 0.0)
        pad_ref[HALO + tstep:2 * HALO + tstep, :] = jnp.where(u < nsteps - 1, next_ref[0].astype(F32), 0.0)

    hlane = lax.broadcasted_iota(jnp.int32, (ck, LANES), 1)
    is_head = (hlane >= hoff) & (hlane < hoff + SSD_HEADS)
    neg_a = -jnp.exp(alog_ref[...])
    tri = _scan_tri(ck, rev).astype(BF16)
    last = 0 if rev else ck - 1
    hrow = lax.broadcasted_iota(jnp.int32, (LANES, SSD_INNER), 0)
    hcol = lax.broadcasted_iota(jnp.int32, (LANES, SSD_INNER), 1)
    expand = (hcol // SSD_HEAD_DIM + hoff == hrow).astype(BF16)
    r = lax.broadcasted_iota(jnp.int32, (ck, ck), 0)
    c = lax.broadcasted_iota(jnp.int32, (ck, ck), 1)
    causal = (c >= r) if rev else (c <= r)
    lane = lax.broadcasted_iota(jnp.int32, (ck, LANES), 1)
    groups = range(SSD_GROUPS)
    gw = [slice(g * SSD_GROUP_W, (g + 1) * SSD_GROUP_W) for g in groups]
    heads_per_group = SSD_HEADS // SSD_GROUPS

    def prepare(ci):
        rows = slice(ci * ck, (ci + 1) * ck)
        if readout:
            xbc = act_ref[0, rows, :].astype(F32)
        else:
            half = SSD_CONV // 2
            acc = cb_ref[...] + cw_ref[0:1, :] * pad_ref[pl.ds(ci * ck + HALO - half, ck), :]
            for k in range(1, SSD_CONV):
                acc = acc + cw_ref[k:k + 1, :] * pad_ref[pl.ds(ci * ck + HALO - half + k, ck), :]
            xbc = _silu(acc)
            act_ref[0, rows, :] = xbc.astype(act_ref.dtype)
        xs = xbc[:, :SSD_INNER]
        dt = jnp.where(is_head, jax.nn.softplus(dt_ref[0, rows, :] + dtb_ref[...]), 0.0)
        acum = _dot01(tri, dt * neg_a)
        total = acum[last:last + 1, :]
        acum_t = acum.T
        dt_x = _dot_v01(dt, expand)
        te_x = _dot_v01(jnp.exp(total - acum), expand)
        xdt = xs * dt_x
        xdt_b = xdt.astype(BF16)
        p = dict(xs=xs, ea_x=_dot_v01(jnp.exp(acum), expand), xw_b=(xdt * te_x).astype(BF16),
                 cd_x=_dot_v01(jnp.broadcast_to(jnp.exp(total), (SUBLANES, LANES)), expand)[0:1, :],
                 b_g=[xbc[:, SSD_INNER + g * SSD_STATE:SSD_INNER + (g + 1) * SSD_STATE].astype(BF16) for g in groups],
                 c_g=[xbc[:, SSD_INNER + SSD_BC + g * SSD_STATE:SSD_INNER + SSD_BC + (g + 1) * SSD_STATE].astype(BF16)
                      for g in groups],
                 decay=[], masked_x=[])
        for hd in range(SSD_HEADS):
            tile, half_i = hd // 2, hd % 2
            h = hoff + hd
            seg = acum[:, h:h + 1] - acum_t[h:h + 1, :]
            p["decay"].append(jnp.where(causal, jnp.exp(seg), 0.0))
            xpair = xdt_b[:, tile * LANES:(tile + 1) * LANES]
            in_half = (lane >= half_i * SSD_HEAD_DIM) & (lane < (half_i + 1) * SSD_HEAD_DIM)
            p["masked_x"].append(jnp.where(in_half, xpair, jnp.zeros_like(xpair)))
        return p

    order = list(range(per_step - 1, -1, -1) if rev else range(per_step))
    ready = prepare(order[0])
    for n, ci in enumerate(order):
        rows = slice(ci * ck, (ci + 1) * ck)
        p = ready
        if n + 1 < len(order):
            ready = prepare(order[n + 1])
        b_g, c_g = p["b_g"], p["c_g"]
        s_in = [st_ref[g] for g in groups]
        cb = [lax.dot_general(c_g[g], b_g[g], _NT, preferred_element_type=F32) for g in groups]
        y_off = [jnp.dot(c_g[g], s_in[g].astype(BF16), preferred_element_type=F32) for g in groups]
        chunk_state = [lax.dot_general(b_g[g], p["xw_b"][:, gw[g]], _TN, preferred_element_type=F32) for g in groups]
        for g in groups:
            st_ref[g] = p["cd_x"][:, gw[g]] * s_in[g] + chunk_state[g]
        decay_cb = [(cb[hd // heads_per_group] * p["decay"][hd]).astype(BF16) for hd in range(SSD_HEADS)]
        y_diag = [jnp.dot(decay_cb[hd], p["masked_x"][hd], preferred_element_type=F32) for hd in range(SSD_HEADS)]
        y_parts = []
        for tile in range(SSD_HEADS // 2):
            g, within = (2 * tile) // heads_per_group, tile % (heads_per_group // 2)
            y_tile = y_off[g][:, within * LANES:(within + 1) * LANES] * p["ea_x"][:, tile * LANES:(tile + 1) * LANES]
            y_parts.append(y_tile + y_diag[2 * tile] + y_diag[2 * tile + 1])
        y = jnp.concatenate(y_parts, axis=1)

        if readout:
            y = y + yin_ref[0, rows, :].astype(F32)
            y = y * _silu(z_ref[0, rows, :].astype(F32))
            ms = jnp.mean(y * y, axis=-1, keepdims=True)
            y_ref[0, rows, :] = (y * lax.rsqrt(ms + EPS) * ng_ref[...]).astype(y_ref.dtype)
        else:
            y_ref[0, rows, :] = (y + dsk_ref[...] * p["xs"]).astype(y_ref.dtype)

    @pl.when(step == nsteps - 1)
    def _():
        sfin_ref[0] = st_ref[...]


def _ssd_scan(xbc, dt_raw, dt_bias, a_log, s0, rev, conv_w=None, conv_b=None, d_skip=None,
              y_in=None, z=None, norm_g=None):
    bsz, seq, _ = xbc.shape
    tstep = min(seq, SSD_STEP)
    nsteps = seq // tstep
    hb = tstep // HALO
    nhalo = seq // HALO
    hoff = SSD_HEADS if rev else 0

    def cidx(i):
        return (nsteps - 1 - i) if rev else i

    def head_lanes(v):
        return jnp.pad(v.reshape(1, -1), ((0, 0), (hoff, LANES - hoff - v.shape[-1])))

    chunk = lambda width: pl.BlockSpec((1, tstep, width), lambda b, i: (b, cidx(i), 0))
    row = lambda width: pl.BlockSpec((1, width), lambda b, i: (0, 0))
    state_spec = pl.BlockSpec((1, SSD_GROUPS, SSD_STATE, SSD_GROUP_W), lambda b, i: (b, 0, 0, 0))
    state_shape = jax.ShapeDtypeStruct((bsz, SSD_GROUPS, SSD_STATE, SSD_GROUP_W), F32)
    y_shape = jax.ShapeDtypeStruct((bsz, seq, SSD_INNER), BF16)
    state_scratch = pltpu.VMEM((SSD_GROUPS, SSD_STATE, SSD_GROUP_W), F32)
    if rev:
        in_specs = [chunk(SSD_CONV_DIM), chunk(LANES), row(LANES), row(LANES), state_spec,
                    chunk(SSD_INNER), chunk(SSD_INNER), row(SSD_INNER)]
        args = [xbc, dt_raw, head_lanes(dt_bias), head_lanes(a_log), s0, y_in, z, norm_g.reshape(1, -1)]
        out_specs, out_shape = [chunk(SSD_INNER), state_spec], [y_shape, state_shape]
        scratch = [state_scratch]
    else:
        in_specs = [
            chunk(SSD_CONV_DIM),
            pl.BlockSpec((1, HALO, SSD_CONV_DIM), lambda b, i: (b, jnp.maximum(i * hb - 1, 0), 0)),
            pl.BlockSpec((1, HALO, SSD_CONV_DIM), lambda b, i: (b, jnp.minimum((i + 1) * hb, nhalo - 1), 0)),
            chunk(LANES),
            pl.BlockSpec((SUBLANES, SSD_CONV_DIM), lambda b, i: (0, 0)), row(SSD_CONV_DIM),
            row(LANES), row(LANES), row(SSD_INNER), state_spec]
        args = [xbc, xbc, xbc, dt_raw,
                jnp.pad(conv_w, ((0, SUBLANES - SSD_CONV), (0, 0))), conv_b.reshape(1, -1),
                head_lanes(dt_bias), head_lanes(a_log), jnp.repeat(d_skip, SSD_HEAD_DIM).reshape(1, -1), s0]
        out_specs = [chunk(SSD_INNER), chunk(SSD_CONV_DIM), state_spec]
        out_shape = [y_shape, jax.ShapeDtypeStruct((bsz, seq, SSD_CONV_DIM), BF16), state_shape]
        scratch = [pltpu.VMEM((tstep + 2 * HALO, SSD_CONV_DIM), F32), state_scratch]
    return pl.pallas_call(
        functools.partial(_ssd_kernel, rev=rev, nsteps=nsteps, readout=rev),
        grid=(bsz, nsteps),
        in_specs=in_specs, out_specs=out_specs, out_shape=out_shape, scratch_shapes=scratch,
        compiler_params=_cparams(("parallel", "arbitrary")),
        name="ssd_scan_rev" if rev else "ssd_scan_fwd",
    )(*args)


def _conformer_kernel(p_ref, w_ref, b_ref, g_ref, beta_ref, o_ref, pad_ref, rot_ref, conv_ref, *, nseq, slen):
    v = p_ref[0, :, :CF_CH].astype(F32)
    gate = p_ref[0, :, CF_CH:].astype(F32)
    u = v * jax.nn.sigmoid(gate)
    first = CF_PAD - CF_KERNEL // 2
    span = rot_ref.shape[2]
    for s in range(nseq):
        pad_ref[s, 0:CF_PAD, :] = jnp.zeros((CF_PAD, CF_CH), F32)
        pad_ref[s, CF_PAD + slen:2 * CF_PAD + slen, :] = jnp.zeros((CF_PAD, CF_CH), F32)
        pad_ref[s, CF_PAD:CF_PAD + slen, :] = u[s * slen:(s + 1) * slen, :]
        for res in range(1, SUBLANES):
            rot_ref[res - 1, s] = pad_ref[s, pl.ds(res, span), :]
    for s in range(nseq):
        for cb in range(CF_CH // LANES):
            ch = slice(cb * LANES, (cb + 1) * LANES)
            acc = jnp.broadcast_to(b_ref[:, ch], (slen, LANES))
            for k in range(CF_KERNEL):
                res, lead = (first + k) % SUBLANES, (first + k) // SUBLANES * SUBLANES
                src = pad_ref[s, pl.ds(lead, slen), ch] if res == 0 else rot_ref[res - 1, s, pl.ds(lead, slen), ch]
                acc = acc + w_ref[k:k + 1, ch] * src
            conv_ref[s * slen:(s + 1) * slen, ch] = acc
    y = conv_ref[...]
    mu = jnp.mean(y, axis=-1, keepdims=True)
    xc = y - mu
    var = jnp.mean(xc * xc, axis=-1, keepdims=True)
    y = xc * lax.rsqrt(var + EPS) * g_ref[...] + beta_ref[...]
    o_ref[0] = _silu(y).astype(o_ref.dtype)


def _conformer(p_cf, dw_w, dw_b, ln_g, ln_b, slen):
    bsz, seq, _ = p_cf.shape
    nseq = max(1, min(seq, 256) // slen)
    tb = nseq * slen
    kpad = -(-CF_KERNEL // SUBLANES) * SUBLANES
    last_tap_row = CF_PAD - CF_KERNEL // 2 + CF_KERNEL - 1
    span = slen + last_tap_row // SUBLANES * SUBLANES
    return pl.pallas_call(
        functools.partial(_conformer_kernel, nseq=nseq, slen=slen),
        grid=(bsz, seq // tb),
        in_specs=[pl.BlockSpec((1, tb, 2 * CF_CH), lambda b, i: (b, i, 0)),
                  pl.BlockSpec((kpad, CF_CH), lambda b, i: (0, 0)),
                  pl.BlockSpec((1, CF_CH), lambda b, i: (0, 0)),
                  pl.BlockSpec((1, CF_CH), lambda b, i: (0, 0)),
                  pl.BlockSpec((1, CF_CH), lambda b, i: (0, 0))],
        out_specs=pl.BlockSpec((1, tb, CF_CH), lambda b, i: (b, i, 0)),
        out_shape=jax.ShapeDtypeStruct((bsz, seq, CF_CH), BF16),
        scratch_shapes=[pltpu.VMEM((nseq, slen + 2 * CF_PAD, CF_CH), F32),
                        pltpu.VMEM((SUBLANES - 1, nseq, span, CF_CH), F32),
                        pltpu.VMEM((tb, CF_CH), F32)],
        compiler_params=_cparams(("parallel", "parallel")),
        name="conformer_conv",
    )(p_cf, jnp.pad(dw_w, ((0, kpad - CF_KERNEL), (0, 0))), dw_b.reshape(1, -1), ln_g.reshape(1, -1),
      ln_b.reshape(1, -1))


def _hgrn2_kernel(*refs, rev, mode):
    if mode == "state":
        v_ref, f_ref, lb_ref, s0_ref, sfin_ref, st_ref = refs
    elif mode == "out":
        q_ref, v_ref, f_ref, lb_ref, s0_ref, o_ref, st_ref = refs
    else:
        q_ref, v_ref, f_ref, lb_ref, s0_ref, oin_ref, g_ref, ng_ref, perm_ref, o_ref, st_ref, osc_ref = refs
    step = pl.program_id(1)
    ck = HG_CHUNK
    per_step = v_ref.shape[1] // ck

    @pl.when(step == 0)
    def _():
        st_ref[...] = s0_ref[0]

    lb = lb_ref[...]
    tri = _scan_tri(ck, rev).astype(BF16)
    last = 0 if rev else ck - 1
    mid_pos = HG_CHUNK // 2 - 1
    mid = (ck - 1 - mid_pos) if rev else mid_pos
    r = lax.broadcasted_iota(jnp.int32, (ck, ck), 0)
    c = lax.broadcasted_iota(jnp.int32, (ck, ck), 1)
    causal = (c >= r) if rev else (c <= r)

    heads = range(HG_HEADS)
    ks = [slice(h * HG_DK, (h + 1) * HG_DK) for h in heads]
    vs = [slice(h * HG_DV, (h + 1) * HG_DV) for h in heads]

    def prepare(ci):
        rows = slice(ci * ck, (ci + 1) * ck)
        f = lb + (1.0 - lb) * jax.nn.sigmoid(f_ref[0, rows, :])
        kk = 1.0 - f
        gcum = _dot01(tri, jnp.log(f))
        g_end = gcum[last:last + 1, :]
        p = dict(k_end=(kk * jnp.exp(g_end - gcum)).astype(BF16), dec_end=jnp.exp(g_end))
        if mode != "state":
            g_mid = gcum[mid:mid + 1, :]
            q = _silu(q_ref[0, rows, :].astype(F32))
            p.update(q_rel=(q * jnp.exp(gcum - g_mid)).astype(BF16), k_rel=(kk * jnp.exp(g_mid - gcum)).astype(BF16),
                     q_dec=(q * jnp.exp(gcum)).astype(BF16))
        if mode == "readout":
            p.update(gate=_silu(g_ref[0, rows, :].astype(F32)))
        return p

    order = list(range(per_step - 1, -1, -1) if rev else range(per_step))
    ready = prepare(order[0])
    for n, ci in enumerate(order):
        rows = slice(ci * ck, (ci + 1) * ck)
        p = ready
        k_end, dec_end = p["k_end"], p["dec_end"]
        v = v_ref[0, rows, :]
        if mode != "state":
            q_rel, k_rel, q_dec = p["q_rel"], p["k_rel"], p["q_dec"]
        if mode == "readout":
            gate = p["gate"]
        if n + 1 < len(order):
            ready = prepare(order[n + 1])

        s_in = [st_ref[h] for h in heads]
        if mode != "state":
            att = [lax.dot_general(q_rel[:, ks[h]], k_rel[:, ks[h]], _NT, preferred_element_type=F32) for h in heads]
            o_inter = [lax.dot_general(q_dec[:, ks[h]], s_in[h].astype(BF16), _NT, preferred_element_type=F32)
                       for h in heads]
            att = [jnp.where(causal, a, 0.0).astype(BF16) for a in att]
            o = [jnp.dot(att[h], v[:, vs[h]], preferred_element_type=F32) + o_inter[h] for h in heads]
        chunk_state = [lax.dot_general(v[:, vs[h]], k_end[:, ks[h]], _TN, preferred_element_type=F32) for h in heads]
        for h in heads:
            st_ref[h] = s_in[h] * dec_end[:, ks[h]] + chunk_state[h]
        if mode == "readout":
            for h in heads:
                o_h = o[h] + oin_ref[0, rows, vs[h]].astype(F32)
                ms = jnp.mean(o_h * o_h, axis=-1, keepdims=True)
                o_h = (o_h * lax.rsqrt(ms + EPS) * ng_ref[...]) * gate[:, vs[h]]
                osc_ref[rows, vs[h]] = o_h.astype(osc_ref.dtype)
        elif mode == "out":
            for h in heads:
                o_ref[0, rows, vs[h]] = o[h].astype(o_ref.dtype)

    if mode == "readout":
        nat = jnp.dot(perm_ref[...], osc_ref[...], preferred_element_type=F32).astype(o_ref.dtype)
        ncols = o_ref.shape[2]
        for gr in range(o_ref.shape[1]):
            o_ref[0, gr] = nat[gr * ncols:(gr + 1) * ncols, :]
    if mode == "state":
        @pl.when(step == pl.num_programs(1) - 1)
        def _():
            sfin_ref[0] = st_ref[...]


def _hgrn2_scan(v, f_raw, lb, s0, rev, mode, q=None, o_in=None, g=None, norm_g=None):
    bsz, seq, _ = v.shape
    tstep = min(seq, HG_STEP)
    nsteps = seq // tstep
    dcol = 1 if rev else 0

    def cidx(i):
        return (nsteps - 1 - i) if rev else i

    tok = lambda width: pl.BlockSpec((1, tstep, width), lambda b, i: (b, cidx(i), 0))
    state_spec = pl.BlockSpec((1, HG_HEADS, HG_DV, HG_DK), lambda b, i: (b, 0, 0, 0))
    f_spec = pl.BlockSpec((1, tstep, HG_KEY), lambda b, i: (b, cidx(i), dcol))
    lb_spec = pl.BlockSpec((1, HG_KEY), lambda b, i: (0, 0))
    state_shape = jax.ShapeDtypeStruct((bsz, HG_HEADS, HG_DV, HG_DK), F32)
    scratch = [pltpu.VMEM((HG_HEADS, HG_DV, HG_DK), F32)]
    if mode == "state":
        in_specs, args = [tok(HG_VAL), f_spec, lb_spec, state_spec], [v, f_raw, lb.reshape(1, -1), s0]
        out_specs, out_shape = state_spec, state_shape
    else:
        in_specs = [tok(HG_KEY), tok(HG_VAL), f_spec, lb_spec, state_spec]
        args = [q, v, f_raw, lb.reshape(1, -1), s0]
        out_specs, out_shape = tok(HG_VAL), jax.ShapeDtypeStruct((bsz, seq, HG_VAL), BF16)
        if mode == "readout":
            grid_rows = seq // GRID_W
            ncols = tstep // grid_rows
            tok_id = jnp.arange(tstep)
            src = (tok_id % ncols) * grid_rows + tok_id // ncols
            perm = (src[:, None] == tok_id[None, :]).astype(BF16)
            in_specs += [tok(HG_VAL), tok(HG_VAL), pl.BlockSpec((1, HG_DV), lambda b, i: (0, 0)),
                         pl.BlockSpec((tstep, tstep), lambda b, i: (0, 0))]
            args += [o_in, g, norm_g.reshape(1, -1), perm]
            out_specs = pl.BlockSpec((1, grid_rows, ncols, HG_VAL), lambda b, i: (b, 0, cidx(i), 0))
            out_shape = jax.ShapeDtypeStruct((bsz, grid_rows, GRID_W, HG_VAL), BF16)
            scratch.append(pltpu.VMEM((tstep, HG_VAL), BF16))
    out = pl.pallas_call(
        functools.partial(_hgrn2_kernel, rev=rev, mode=mode),
        grid=(bsz, nsteps),
        in_specs=in_specs, out_specs=out_specs, out_shape=out_shape,
        scratch_shapes=scratch,
        compiler_params=_cparams(("parallel", "arbitrary")),
        name=f"hgrn2_{mode}_{'rev' if rev else 'fwd'}",
    )(*args)
    return out.reshape(bsz, seq, HG_VAL) if mode == "readout" else out


def _first_argmax(vals):
    best, idx = vals[0], jnp.zeros(vals[0].shape, jnp.int32)
    for j in range(1, len(vals)):
        better = vals[j] > best
        idx = jnp.where(better, j, idx)
        best = jnp.where(better, vals[j], best)
    return idx, best


def _pick(idx, vals):
    out = vals[-1]
    for j in range(len(vals) - 2, -1, -1):
        out = jnp.where(idx == j, vals[j], out)
    return out


def _store_token_tiles(ref, v):
    n = v.shape[0]
    for j in range(SUBLANES):
        ref[pl.ds(j, n, stride=SUBLANES), :] = v[:, j * LANES:(j + 1) * LANES]


def _load_token_tiles(ref):
    n = ref.shape[0] // SUBLANES
    return jnp.concatenate([ref[pl.ds(j, n, stride=SUBLANES), :] for j in range(SUBLANES)], axis=1)


def _token_tile(ref, idx):
    return ref.at[pl.ds(pl.multiple_of(idx * SUBLANES, SUBLANES), SUBLANES)]


def _router_kernel(x_ref, g_ref, sh_ref, sc_ref, rw_ref, rb_ref, h_ref, slot_ref, wt_ref, cnt_ref, carry_ref):
    step = pl.program_id(0)

    @pl.when(step == 0)
    def _():
        carry_ref[...] = jnp.zeros_like(carry_ref)

    h = _normmod(x_ref[...], g_ref[...], sh_ref[0], sc_ref[0])
    _store_token_tiles(h_ref, h)
    scores = jax.nn.sigmoid(_dot_f32(rw_ref[...], h, _NT))
    sel = scores + rb_ref[...]
    srow = [sel[e:e + 1, :] for e in range(N_EXPERTS)]
    prow = [scores[e:e + 1, :] for e in range(N_EXPERTS)]
    gscore = []
    for gi in range(N_EXPERT_GROUPS):
        m = srow[gi * EXPERTS_PER_GROUP:(gi + 1) * EXPERTS_PER_GROUP]
        pair_sums = [m[i] + m[j] for i in range(EXPERTS_PER_GROUP) for j in range(i + 1, EXPERTS_PER_GROUP)]
        best = pair_sums[0]
        for p in pair_sums[1:]:
            best = jnp.maximum(best, p)
        gscore.append(best)
    gidx, _ = _first_argmax(gscore)
    in_sel = [_pick(gidx, [srow[gi * EXPERTS_PER_GROUP + j] for gi in range(N_EXPERT_GROUPS)])
              for j in range(EXPERTS_PER_GROUP)]
    in_p = [_pick(gidx, [prow[gi * EXPERTS_PER_GROUP + j] for gi in range(N_EXPERT_GROUPS)])
            for j in range(EXPERTS_PER_GROUP)]
    i1, _ = _first_argmax(in_sel)
    i2, _ = _first_argmax([jnp.where(i1 == j, -jnp.inf, in_sel[j]) for j in range(EXPERTS_PER_GROUP)])
    w1, w2 = _pick(i1, in_p), _pick(i2, in_p)
    den = w1 + w2
    e1, e2 = gidx * EXPERTS_PER_GROUP + i1, gidx * EXPERTS_PER_GROUP + i2
    tl = h.shape[0]
    erow = lax.broadcasted_iota(jnp.int32, (N_EXPERTS, tl), 0)
    oh1, oh2 = erow == e1, erow == e2
    cnt = (oh1 | oh2).astype(F32)
    r = lax.broadcasted_iota(jnp.int32, (tl, tl), 0)
    c = lax.broadcasted_iota(jnp.int32, (tl, tl), 1)
    before = jnp.dot(cnt.astype(BF16), (r < c).astype(BF16), preferred_element_type=F32)
    base = carry_ref[:, 0:1] + before
    rank1 = jnp.sum(jnp.where(oh1, base, 0.0), axis=0, keepdims=True).astype(jnp.int32)
    rank2 = jnp.sum(jnp.where(oh2, base, 0.0), axis=0, keepdims=True).astype(jnp.int32)
    carry_ref[...] = carry_ref[...] + jnp.sum(cnt, axis=1, keepdims=True)
    cnt_ref[...] = carry_ref[...].astype(jnp.int32)
    srow = lax.broadcasted_iota(jnp.int32, (SUBLANES, tl), 0)
    slot_ref[0] = jnp.where(srow == 0, e1, jnp.where(srow == 1, e2, jnp.where(srow == 2, rank1, rank2)))
    wrow = lax.broadcasted_iota(jnp.int32, (LANES, tl), 0)
    wt_ref[...] = jnp.where(wrow == 0, w1 / den, jnp.where(wrow == 1, w2 / den, 0.0)).T


def _ffn_norm_router(x, g, shift, scale, router_w, router_b, seq):
    t, d = x.shape
    tl = min(seq, MOE_TL)
    nblk = t // tl
    nb = shift.shape[0]
    mod_spec = pl.BlockSpec((1, 1, d), (lambda i: (i * tl // seq, 0, 0)) if nb > 1 else (lambda i: (0, 0, 0)))
    return pl.pallas_call(
        _router_kernel,
        grid=(nblk,),
        in_specs=[pl.BlockSpec((tl, d), lambda i: (i, 0)),
                  pl.BlockSpec((1, d), lambda i: (0, 0)),
                  mod_spec, mod_spec,
                  pl.BlockSpec((N_EXPERTS, d), lambda i: (0, 0)),
                  pl.BlockSpec((N_EXPERTS, 1), lambda i: (0, 0))],
        out_specs=[pl.BlockSpec((tl * SUBLANES, LANES), lambda i: (i, 0)),
                   pl.BlockSpec((1, SUBLANES, tl), lambda i: (i, 0, 0)),
                   pl.BlockSpec((tl, LANES), lambda i: (i, 0)),
                   pl.BlockSpec((N_EXPERTS, LANES), lambda i: (0, 0))],
        out_shape=[jax.ShapeDtypeStruct((t * SUBLANES, LANES), F32),
                   jax.ShapeDtypeStruct((nblk, SUBLANES, tl), jnp.int32),
                   jax.ShapeDtypeStruct((t, LANES), F32),
                   jax.ShapeDtypeStruct((N_EXPERTS, LANES), jnp.int32)],
        scratch_shapes=[pltpu.VMEM((N_EXPERTS, LANES), F32)],
        compiler_params=_cparams(("arbitrary",)),
        name="ffn_norm_router",
    )(x, g.reshape(1, d), shift, scale, router_w.T, router_b.reshape(N_EXPERTS, 1))


DMA_UNROLL = 8


def _dispatch_kernel(slot_ref, ends_ref, h_ref, hs_ref, zero_ref, sem, *, tl, max_tiles):
    @pl.when(pl.program_id(0) == 0)
    def _():
        zero_ref[...] = jnp.zeros_like(zero_ref)
        fills = []
        for e in range(N_EXPERTS):
            end = ends_ref[e]
            gap = (MOE_TM - (end & (MOE_TM - 1))) & (MOE_TM - 1)
            for bit in range(MOE_TM.bit_length() - 1):
                size = 1 << bit
                first_row = pl.multiple_of((end + (gap & (size - 1))) * SUBLANES, SUBLANES)
                fills.append(((gap & size) != 0,
                              pltpu.make_async_copy(zero_ref.at[pl.ds(0, size * SUBLANES)],
                                                    hs_ref.at[pl.ds(first_row, size * SUBLANES)], sem)))
        for n in range(N_EXPERTS):
            first_slot = ends_ref[N_EXPERTS] + n * MOE_TM
            first_row = pl.multiple_of(first_slot * SUBLANES, SUBLANES)
            fills.append((first_slot < max_tiles * MOE_TM,
                          pltpu.make_async_copy(zero_ref, hs_ref.at[pl.ds(first_row, MOE_TM * SUBLANES)], sem)))
        for cond, cp in fills:
            pl.when(cond)(cp.start)
        for cond, cp in fills:
            pl.when(cond)(cp.wait)

    def issue(t, carry):
        pltpu.make_async_copy(_token_tile(h_ref, t), _token_tile(hs_ref, slot_ref[0, 0, t]), sem).start()
        pltpu.make_async_copy(_token_tile(h_ref, t), _token_tile(hs_ref, slot_ref[0, 0, tl + t]), sem).start(priority=1)
        return carry

    lax.fori_loop(0, tl, issue, 0, unroll=DMA_UNROLL)
    for _ in range(2):
        pltpu.make_async_copy(h_ref, hs_ref.at[pl.ds(0, tl * SUBLANES)], sem).wait()


def _dispatch(h, slot2, ends, tl, max_tiles):
    t = h.shape[0] // SUBLANES
    nblk = t // tl
    return pl.pallas_call(
        functools.partial(_dispatch_kernel, tl=tl, max_tiles=max_tiles),
        grid=(nblk,),
        in_specs=[pl.BlockSpec((1, 1, 2 * tl), lambda i: (i, 0, 0), memory_space=pltpu.SMEM),
                  pl.BlockSpec(memory_space=pltpu.SMEM),
                  pl.BlockSpec((tl * SUBLANES, LANES), lambda i: (i, 0))],
        out_specs=pl.BlockSpec(memory_space=pl.ANY),
        out_shape=jax.ShapeDtypeStruct((max_tiles * MOE_TM * SUBLANES, LANES), F32),
        scratch_shapes=[pltpu.VMEM((MOE_TM * SUBLANES, LANES), F32), pltpu.SemaphoreType.DMA(())],
        compiler_params=_cparams(("arbitrary",)),
        name="moe_dispatch",
    )(slot2, ends, h)


def _expert_kernel(te_ref, tf_ref, nt_ref, hs_ref, wg_ref, wu_ref, wd_ref, ys_ref, wgb_ref, wub_ref, wdb_ref):
    n = pl.program_id(0)

    @pl.when(n >= nt_ref[0])
    def _():
        ys_ref[...] = jnp.zeros_like(ys_ref)

    @pl.when(n < nt_ref[0])
    def _():
        @pl.when(tf_ref[n] == 1)
        def _():
            wgb_ref[...] = wg_ref[0].astype(BF16)
            wub_ref[...] = wu_ref[0].astype(BF16)
            wdb_ref[...] = wd_ref[0].astype(BF16)

        h = _load_token_tiles(hs_ref).astype(BF16)
        a = jnp.dot(h, wgb_ref[...], preferred_element_type=F32)
        b = jnp.dot(h, wub_ref[...], preferred_element_type=F32)
        y = jnp.dot((_silu(a) * b).astype(BF16), wdb_ref[...], preferred_element_type=F32)
        _store_token_tiles(ys_ref, y)


def _experts(hs, tile_expert, tile_first, n_tiles, w_gate, w_up, w_down, layer, max_tiles):
    d = w_gate.shape[2]
    tile_spec = pl.BlockSpec((MOE_TM * SUBLANES, LANES), lambda n, te, tf, nt: (n, 0))
    grid_spec = pltpu.PrefetchScalarGridSpec(
        num_scalar_prefetch=3,
        grid=(max_tiles,),
        in_specs=[tile_spec,
                  pl.BlockSpec((None, 1, d, D_EXPERT), lambda n, te, tf, nt: (layer, te[n], 0, 0)),
                  pl.BlockSpec((None, 1, d, D_EXPERT), lambda n, te, tf, nt: (layer, te[n], 0, 0)),
                  pl.BlockSpec((None, 1, D_EXPERT, d), lambda n, te, tf, nt: (layer, te[n], 0, 0))],
        out_specs=tile_spec,
        scratch_shapes=[pltpu.VMEM((d, D_EXPERT), BF16), pltpu.VMEM((d, D_EXPERT), BF16),
                        pltpu.VMEM((D_EXPERT, d), BF16)])
    return pl.pallas_call(
        _expert_kernel,
        grid_spec=grid_spec,
        out_shape=jax.ShapeDtypeStruct((max_tiles * MOE_TM * SUBLANES, LANES), F32),
        compiler_params=_cparams(("arbitrary",)),
        name="moe_experts",
    )(tile_expert, tile_first, n_tiles, hs, w_gate, w_up, w_down)


def _combine_kernel(slot_ref, nslot_ref, wt_ref, res_ref, gate_ref, fg_ref, ys_ref, o_ref, y1_ref, y2_ref, sems, *,
                    tl, final):
    step, nsteps = pl.program_id(0), pl.num_programs(0)
    cur = step % 2

    def start_gather(sref, buf):
        def issue(t, carry):
            pltpu.make_async_copy(_token_tile(ys_ref, sref[0, 0, t]), _token_tile(y1_ref.at[buf], t),
                                  sems.at[buf]).start()
            pltpu.make_async_copy(_token_tile(ys_ref, sref[0, 0, tl + t]), _token_tile(y2_ref.at[buf], t),
                                  sems.at[buf]).start(priority=1)
            return carry

        lax.fori_loop(0, tl, issue, 0, unroll=DMA_UNROLL)

    @pl.when(step == 0)
    def _():
        start_gather(slot_ref, 0)

    @pl.when(step + 1 < nsteps)
    def _():
        start_gather(nslot_ref, 1 - cur)

    pltpu.make_async_copy(ys_ref.at[pl.ds(0, tl * SUBLANES)], y1_ref.at[cur], sems.at[cur]).wait()
    pltpu.make_async_copy(ys_ref.at[pl.ds(0, tl * SUBLANES)], y2_ref.at[cur], sems.at[cur]).wait()
    wt = wt_ref[...]
    moe = wt[:, 0:1] * _load_token_tiles(y1_ref.at[cur]) + wt[:, 1:2] * _load_token_tiles(y2_ref.at[cur])
    x = res_ref[...] + gate_ref[0] * moe
    if final:
        ms = jnp.mean(x * x, axis=-1, keepdims=True)
        x = x * lax.rsqrt(ms + EPS) * fg_ref[...]
    o_ref[...] = x


def _combine(ys, slot2, wt, res, gate, final_g, seq, tl):
    t, d = res.shape
    nblk = t // tl
    nb = gate.shape[0]
    final = final_g is not None
    fg = final_g.reshape(1, d) if final else jnp.ones((1, d), F32)
    return pl.pallas_call(
        functools.partial(_combine_kernel, tl=tl, final=final),
        grid=(nblk,),
        in_specs=[pl.BlockSpec((1, 1, 2 * tl), lambda i: (i, 0, 0), memory_space=pltpu.SMEM),
                  pl.BlockSpec((1, 1, 2 * tl), lambda i: (jnp.minimum(i + 1, nblk - 1), 0, 0),
                               memory_space=pltpu.SMEM),
                  pl.BlockSpec((tl, LANES), lambda i: (i, 0)),
                  pl.BlockSpec((tl, d), lambda i: (i, 0)),
                  pl.BlockSpec((1, 1, d), (lambda i: (i * tl // seq, 0, 0)) if nb > 1 else (lambda i: (0, 0, 0))),
                  pl.BlockSpec((1, d), lambda i: (0, 0)),
                  pl.BlockSpec(memory_space=pl.ANY)],
        out_specs=pl.BlockSpec((tl, d), lambda i: (i, 0)),
        out_shape=jax.ShapeDtypeStruct((t, d), F32),
        scratch_shapes=[pltpu.VMEM((2, tl * SUBLANES, LANES), F32), pltpu.VMEM((2, tl * SUBLANES, LANES), F32),
                        pltpu.SemaphoreType.DMA((2,))],
        compiler_params=_cparams(("arbitrary",)),
        name="moe_combine",
    )(slot2, slot2, wt, res, gate, fg, ys)


def _even_layer_mixer(xs, mods, norm_g, w_in, conv_w, conv_b, dt_bias, a_log, d_skip, ssd_g,
                      cf_w, cf_b, cf_lng, cf_lnb, w_out, rows):
    s_lo = SSD_INNER
    dt_lo = SSD_INNER + SSD_CONV_DIM
    cf_lo = dt_lo + 2 * SSD_HEADS
    w_cf = _realign_cols(w_in, cf_lo, 2 * CF_CH)
    bsz = xs[0].shape[0]
    zero_state = jnp.zeros((bsz, SSD_GROUPS, SSD_STATE, SSD_GROUP_W), F32)
    proj = []
    for x, m in zip(xs, mods):
        h = _norm_mod(x, norm_g, m[0], m[1])
        proj.append(dict(z=_matmul(h, w_in, BF16, 0, s_lo), xbc=_matmul(h, w_in, BF16, s_lo, SSD_CONV_DIM),
                         dt=_matmul(h, w_in, F32, dt_lo, LANES), cf=_matmul(h, w_cf, BF16)))
    y_fwd, act, o_ssd = [None, None], [None, None], [None, None]
    state = zero_state
    for si in range(2):
        p = proj[si]
        y_fwd[si], act[si], state = _ssd_scan(p["xbc"], p["dt"], dt_bias[0], a_log[0], state, False,
                                              conv_w=conv_w, conv_b=conv_b, d_skip=d_skip)
    state = zero_state
    for si in range(2):
        p = proj[si]
        o_ssd[si], state = _ssd_scan(act[si], p["dt"], dt_bias[1], a_log[1], state, True,
                                     y_in=y_fwd[si], z=p["z"], norm_g=ssd_g)
    outs = []
    for si, (x, m) in enumerate(zip(xs, mods)):
        slen = x.shape[1] if si == 0 else GRID_W
        o_cf = _conformer(proj[si]["cf"], cf_w, cf_b, cf_lng, cf_lnb, slen)
        outs.append(_matmul_res2(o_ssd[si], o_cf, w_out, x, m[2]))
    return outs


def _odd_layer_mixer(xc, xl, m_c, m_l, norm_g, w_in, lb, hg_g, w_out):
    st0 = HG_KEY + HG_VAL
    f0 = st0 + HG_VAL
    bsz = xl.shape[0]
    h_c = _norm_mod(xc, norm_g, m_c[0], m_c[1])
    h_l = _norm_mod_colmajor(xl, norm_g, m_l[0], m_l[1])
    v_c, f_c = _matmul(h_c, w_in, BF16, st0, HG_VAL), _matmul(h_c, w_in, F32, f0, 2 * HG_KEY)
    q_l, g_l = _matmul(h_l, w_in, BF16, 0, HG_KEY), _matmul(h_l, w_in, BF16, HG_KEY, HG_VAL)
    v_l, f_l = _matmul(h_l, w_in, BF16, st0, HG_VAL), _matmul(h_l, w_in, F32, f0, 2 * HG_KEY)
    zero_state = jnp.zeros((bsz, HG_HEADS, HG_DV, HG_DK), F32)
    s_f = _hgrn2_scan(v_c, f_c, lb, zero_state, False, "state")
    s_r = _hgrn2_scan(v_c, f_c, lb, zero_state, True, "state")
    o_f = _hgrn2_scan(v_l, f_l, lb, s_f, False, "out", q=q_l)
    o = _hgrn2_scan(v_l, f_l, lb, s_r, True, "readout", q=q_l, o_in=o_f, g=g_l, norm_g=hg_g)
    return _matmul_res(o, w_out, xl, m_l[2])


def _dispatch_layout(counts, slot, max_tiles, tl_move):
    tiles = (counts + MOE_TM - 1) // MOE_TM
    tile_ends = jnp.cumsum(tiles)
    tile_starts = tile_ends - tiles
    total = tile_ends[-1]
    starts = tile_starts * MOE_TM
    ends = jnp.concatenate([starts + counts, (total * MOE_TM).reshape(1)]).astype(jnp.int32)
    experts = jnp.arange(N_EXPERTS, dtype=jnp.int32)

    def lookup(table, idx):
        return jnp.sum(jnp.where(idx[..., None] == experts, table, 0), axis=-1)

    slots = jnp.concatenate([(lookup(starts, slot[:, k]) + slot[:, 2 + k]).reshape(-1, tl_move) for k in range(2)],
                            axis=-1)
    n = jnp.arange(max_tiles, dtype=jnp.int32)
    nn = jnp.minimum(n, total - 1)
    expert = jnp.sum((nn[:, None] >= tile_ends[None, :]).astype(jnp.int32), axis=1)
    is_first = ((nn == lookup(tile_starts, expert)) & (n < total)).astype(jnp.int32)
    return slots[:, None, :].astype(jnp.int32), ends, expert, is_first, total.reshape(1).astype(jnp.int32)


def _moe_block(x, m, norm_g, router_w, router_b, w_gate, w_up, w_down, layer, final_g=None):
    bsz, seq, d = x.shape
    t = bsz * seq
    xf = x.reshape(t, d)
    tl = min(seq, MOE_TL_MOVE)
    h, slot, wt, cnt = _ffn_norm_router(xf, norm_g, m[3], m[4], router_w, router_b, seq)
    max_tiles = 2 * t // MOE_TM + N_EXPERTS
    slot2, ends, tile_expert, tile_first, n_tiles = _dispatch_layout(cnt[:, 0], slot, max_tiles, tl)
    hs = _dispatch(h, slot2, ends, tl, max_tiles)
    ys = _experts(hs, tile_expert, tile_first, n_tiles, w_gate, w_up, w_down, layer, max_tiles)
    return _combine(ys, slot2, wt, xf, m[5], final_g, seq, tl).reshape(bsz, seq, d)


def kernel(x, c, ctx, c_ctx, mod_w, mod_b, norm_mix_g, norm_ffn_g, router_w, router_b, moe_w_gate, moe_w_up,
           moe_w_down, ab_w_in, ssd_conv_w, ssd_conv_b, ssd_dt_bias, ssd_a_log, ssd_d, ssd_norm_g, cf_dw_w,
           cf_dw_b, cf_ln_g, cf_ln_b, ab_w_out, hg_w_in, hg_lb, hg_norm_g, hg_w_out, final_norm_g):
    depth = mod_w.shape[0]
    assert depth == 2, "layer schedule below is written for one even and one odd layer"
    bsz, seq, d = x.shape
    rows = seq // GRID_W
    lb_all = jnp.cumsum(jax.nn.softmax(hg_lb.astype(F32), axis=0), axis=0)
    lb_all = lb_all - lb_all[0]

    nrow = -(-(bsz + 1) // SUBLANES) * SUBLANES
    cond = jnp.zeros((nrow, d), F32).at[:bsz].set(c).at[bsz].set(c_ctx)
    mod = _modulation(cond, mod_w, mod_b).reshape(depth, nrow, N_MOD, 1, d)

    def mods(l):
        m_l = [mod[l, :bsz, k] for k in range(N_MOD)]
        m_c = [mod[l, bsz:bsz + 1, k] for k in range(N_MOD)]
        return m_c, m_l

    m_c, m_l = mods(0)
    xc, xl = _even_layer_mixer((ctx, x), (m_c, m_l), norm_mix_g[0], ab_w_in[0], ssd_conv_w[0], ssd_conv_b[0],
                               ssd_dt_bias[0], ssd_a_log[0], ssd_d[0], ssd_norm_g[0], cf_dw_w[0], cf_dw_b[0],
                               cf_ln_g[0], cf_ln_b[0], ab_w_out[0], rows)
    xl = _moe_block(xl, m_l, norm_ffn_g[0], router_w, router_b, moe_w_gate, moe_w_up, moe_w_down, 0)
    xc = _moe_block(xc, m_c, norm_ffn_g[0], router_w, router_b, moe_w_gate, moe_w_up, moe_w_down, 0)

    m_c, m_l = mods(1)
    xl = _odd_layer_mixer(xc, xl, m_c, m_l, norm_mix_g[1], hg_w_in[0], lb_all[1], hg_norm_g[0], hg_w_out[0])
    return _moe_block(xl, m_l, norm_ffn_g[1], router_w, router_b, moe_w_gate, moe_w_up, moe_w_down, 1,
                      final_g=final_norm_g)
```

```python
import functools

import jax
import jax.numpy as jnp
from jax import lax
from jax.experimental import pallas as pl
from jax.experimental.pallas import tpu as pltpu

F32 = jnp.float32
BF16 = jnp.bfloat16

D_MODEL = 1024
GRID_W = 64
EPS = 1e-6
N_MOD = 6

SSD_HEADS = 16
SSD_HEAD_DIM = 64
SSD_INNER = SSD_HEADS * SSD_HEAD_DIM
SSD_GROUPS = 4
SSD_STATE = 128
SSD_CONV = 5
SSD_CHUNK = 128
SSD_STEP = 512
SSD_BC = SSD_GROUPS * SSD_STATE
SSD_CONV_DIM = SSD_INNER + 2 * SSD_BC
SSD_GROUP_W = SSD_INNER // SSD_GROUPS

CF_CH = 1024
CF_KERNEL = 31
CF_PAD = 16

HG_HEADS = 8
HG_DK = 128
HG_DV = 128
HG_KEY = HG_HEADS * HG_DK
HG_VAL = HG_HEADS * HG_DV
HG_CHUNK = 64
HG_STEP = 512

N_EXPERTS = 16
N_EXPERT_GROUPS = 4
EXPERTS_PER_GROUP = 4
D_EXPERT = 512
MOE_TL = 512
MOE_TL_DISPATCH = 2048
MOE_TM = 512

LANES = 128
SUBLANES = 8
HALO = 16
CM_COLS = 16
VMEM_LIMIT = 48 * 1024 * 1024


def _cparams(sem):
    return pltpu.CompilerParams(dimension_semantics=sem, vmem_limit_bytes=VMEM_LIMIT)


def _silu(x):
    return x * jax.nn.sigmoid(x)


def _split3(v):
    hi = v.astype(BF16)
    r1 = v - hi.astype(F32)
    mid = r1.astype(BF16)
    lo = (r1 - mid.astype(F32)).astype(BF16)
    return hi, mid, lo


def _dot01(m01, v):
    hi, mid, lo = _split3(v)
    out = jnp.dot(m01, lo, preferred_element_type=F32)
    out = out + jnp.dot(m01, mid, preferred_element_type=F32)
    return out + jnp.dot(m01, hi, preferred_element_type=F32)


def _dot_v01(v, m01):
    hi, mid, lo = _split3(v)
    out = jnp.dot(lo, m01, preferred_element_type=F32)
    out = out + jnp.dot(mid, m01, preferred_element_type=F32)
    return out + jnp.dot(hi, m01, preferred_element_type=F32)


_NN = (((1,), (0,)), ((), ()))
_NT = (((1,), (1,)), ((), ()))
_TN = (((0,), (0,)), ((), ()))


def _dot_f32(a, b, dn):
    a1, a2, a3 = _split3(a)
    b1, b2, b3 = _split3(b)
    out = lax.dot_general(a3, b1, dn, preferred_element_type=F32)
    out = out + lax.dot_general(a1, b3, dn, preferred_element_type=F32)
    out = out + lax.dot_general(a2, b2, dn, preferred_element_type=F32)
    out = out + lax.dot_general(a2, b1, dn, preferred_element_type=F32)
    out = out + lax.dot_general(a1, b2, dn, preferred_element_type=F32)
    return out + lax.dot_general(a1, b1, dn, preferred_element_type=F32)


def _scan_tri(n, rev):
    r = lax.broadcasted_iota(jnp.int32, (n, n), 0)
    c = lax.broadcasted_iota(jnp.int32, (n, n), 1)
    return ((c >= r) if rev else (c <= r))


def _mod_kernel(c_ref, w_ref, b_ref, o_ref):
    cond = _silu(c_ref[...])
    o_ref[0] = _dot_f32(cond, w_ref[0], _NN) + b_ref[0]


def _modulation(cond, mod_w, mod_b):
    depth, d, n = mod_w.shape
    r = cond.shape[0]
    tn = 512
    return pl.pallas_call(
        _mod_kernel,
        grid=(depth, n // tn),
        in_specs=[pl.BlockSpec((r, d), lambda l, j: (0, 0)),
                  pl.BlockSpec((1, d, tn), lambda l, j: (l, 0, j)),
                  pl.BlockSpec((1, 1, tn), lambda l, j: (l, 0, j))],
        out_specs=pl.BlockSpec((1, r, tn), lambda l, j: (l, 0, j)),
        out_shape=jax.ShapeDtypeStruct((depth, r, n), F32),
        compiler_params=_cparams(("parallel", "parallel")),
        name="modulation",
    )(cond, mod_w, mod_b.reshape(depth, 1, n))


def _normmod(x, g, shift, scale):
    ms = jnp.mean(x * x, axis=-1, keepdims=True)
    return (x * lax.rsqrt(ms + EPS) * g) * (1.0 + scale) + shift


def _normmod_kernel(x_ref, g_ref, sh_ref, sc_ref, o_ref):
    o_ref[0] = _normmod(x_ref[0], g_ref[...], sh_ref[0], sc_ref[0]).astype(o_ref.dtype)


def _mod_spec(nb):
    return pl.BlockSpec((1, 1, D_MODEL), (lambda b, *_: (b, 0, 0)) if nb > 1 else (lambda b, *_: (0, 0, 0)))


def _norm_mod(x, g, shift, scale):
    bsz, seq, d = x.shape
    tl = min(seq, 512)
    return pl.pallas_call(
        _normmod_kernel,
        grid=(bsz, seq // tl),
        in_specs=[pl.BlockSpec((1, tl, d), lambda b, i: (b, i, 0)),
                  pl.BlockSpec((1, d), lambda b, i: (0, 0)),
                  _mod_spec(shift.shape[0]), _mod_spec(scale.shape[0])],
        out_specs=pl.BlockSpec((1, tl, d), lambda b, i: (b, i, 0)),
        out_shape=jax.ShapeDtypeStruct((bsz, seq, d), BF16),
        compiler_params=_cparams(("parallel", "parallel")),
        name="norm_mod",
    )(x, g.reshape(1, d), shift, scale)


def _normmod_cm_kernel(x_ref, g_ref, sh_ref, sc_ref, o_ref, h_ref, *, rows):
    h = _normmod(x_ref[0], g_ref[...], sh_ref[0], sc_ref[0])
    nlt = h.shape[-1] // LANES
    for r in range(rows):
        for j in range(nlt):
            h_ref[j, r * CM_COLS:(r + 1) * CM_COLS, :] = h[r, :, j * LANES:(j + 1) * LANES]
    for c in range(CM_COLS):
        for j in range(nlt):
            piece = h_ref[j, pl.ds(c, rows, stride=CM_COLS), :]
            o_ref[0, c * rows:(c + 1) * rows, j * LANES:(j + 1) * LANES] = piece.astype(o_ref.dtype)


def _norm_mod_colmajor(x, g, shift, scale):
    bsz, seq, d = x.shape
    rows = seq // GRID_W
    x4 = x.reshape(bsz, rows, GRID_W, d)
    return pl.pallas_call(
        functools.partial(_normmod_cm_kernel, rows=rows),
        grid=(bsz, GRID_W // CM_COLS),
        in_specs=[pl.BlockSpec((1, rows, CM_COLS, d), lambda b, i: (b, 0, i, 0)),
                  pl.BlockSpec((1, d), lambda b, i: (0, 0)),
                  _mod_spec(shift.shape[0]), _mod_spec(scale.shape[0])],
        out_specs=pl.BlockSpec((1, CM_COLS * rows, d), lambda b, i: (b, i, 0)),
        out_shape=jax.ShapeDtypeStruct((bsz, seq, d), BF16),
        scratch_shapes=[pltpu.VMEM((d // LANES, CM_COLS * rows, LANES), F32)],
        compiler_params=_cparams(("parallel", "parallel")),
        name="norm_mod_colmajor",
    )(x4, g.reshape(1, d), shift, scale)


MM_SEMANTICS = ("arbitrary", "arbitrary", "arbitrary")


def _stage_weights(b_ref, wb_ref):
    @pl.when((pl.program_id(1) == 0) & (pl.program_id(2) == 0))
    def _():
        wb_ref[...] = b_ref[...].astype(BF16)


def _mm_kernel(a_ref, b_ref, o_ref, wb_ref):
    _stage_weights(b_ref, wb_ref)
    o_ref[0] = jnp.dot(a_ref[0], wb_ref[...], preferred_element_type=F32).astype(o_ref.dtype)


def _mm_res2_kernel(a1_ref, a2_ref, b1_ref, b2_ref, res_ref, gate_ref, o_ref, wb1_ref, wb2_ref):
    _stage_weights(b1_ref, wb1_ref)
    _stage_weights(b2_ref, wb2_ref)
    y = jnp.dot(a1_ref[0], wb1_ref[...], preferred_element_type=F32)
    y = y + jnp.dot(a2_ref[0], wb2_ref[...], preferred_element_type=F32)
    o_ref[0] = res_ref[0] + gate_ref[0] * y


def _mm_res_kernel(a_ref, b_ref, res_ref, gate_ref, o_ref, wb_ref):
    _stage_weights(b_ref, wb_ref)
    o_ref[0] = res_ref[0] + gate_ref[0] * jnp.dot(a_ref[0], wb_ref[...], preferred_element_type=F32)


def _matmul(a, w, out_dtype, col0=0, ncols=None):
    bsz, seq, k = a.shape
    n = w.shape[1] - col0 if ncols is None else ncols
    tm, tn = min(seq, 1024), min(n, 1024)
    assert col0 % tn == 0 and n % tn == 0
    jb = col0 // tn
    return pl.pallas_call(
        _mm_kernel,
        grid=(n // tn, bsz, seq // tm),
        in_specs=[pl.BlockSpec((1, tm, k), lambda j, b, i: (b, i, 0)),
                  pl.BlockSpec((k, tn), lambda j, b, i: (0, j + jb))],
        out_specs=pl.BlockSpec((1, tm, tn), lambda j, b, i: (b, i, j)),
        out_shape=jax.ShapeDtypeStruct((bsz, seq, n), out_dtype),
        scratch_shapes=[pltpu.VMEM((k, tn), BF16)],
        compiler_params=_cparams(MM_SEMANTICS),
        name="matmul",
    )(a, w)


def _matmul_res2(a1, a2, w, res, gate):
    bsz, seq, kh = a1.shape
    n = w.shape[1]
    tm, tn = min(seq, 1024), min(n, 512)
    nb = gate.shape[0]
    return pl.pallas_call(
        _mm_res2_kernel,
        grid=(n // tn, bsz, seq // tm),
        in_specs=[pl.BlockSpec((1, tm, kh), lambda j, b, i: (b, i, 0)),
                  pl.BlockSpec((1, tm, kh), lambda j, b, i: (b, i, 0)),
                  pl.BlockSpec((kh, tn), lambda j, b, i: (0, j)),
                  pl.BlockSpec((kh, tn), lambda j, b, i: (1, j)),
                  pl.BlockSpec((1, tm, tn), lambda j, b, i: (b, i, j)),
                  pl.BlockSpec((1, 1, tn), (lambda j, b, i: (b, 0, j)) if nb > 1 else (lambda j, b, i: (0, 0, j)))],
        out_specs=pl.BlockSpec((1, tm, tn), lambda j, b, i: (b, i, j)),
        out_shape=jax.ShapeDtypeStruct((bsz, seq, n), F32),
        scratch_shapes=[pltpu.VMEM((kh, tn), BF16), pltpu.VMEM((kh, tn), BF16)],
        compiler_params=_cparams(MM_SEMANTICS),
        name="matmul_res",
    )(a1, a2, w, w, res, gate)


def _realign_kernel(w_ref, o_ref, *, col0):
    o_ref[...] = w_ref[:, col0:col0 + o_ref.shape[1]].astype(o_ref.dtype)


def _realign_cols(w, col0, ncols):
    k, n = w.shape
    tk = 128
    return pl.pallas_call(
        functools.partial(_realign_kernel, col0=col0),
        grid=(k // tk,),
        in_specs=[pl.BlockSpec((tk, n), lambda i: (i, 0))],
        out_specs=pl.BlockSpec((tk, ncols), lambda i: (i, 0)),
        out_shape=jax.ShapeDtypeStruct((k, ncols), BF16),
        compiler_params=_cparams(("parallel",)),
        name="realign_cols",
    )(w)


def _matmul_res(a, w, res, gate):
    bsz, seq, k = a.shape
    n = w.shape[1]
    tm, tn = min(seq, 1024), min(n, 512)
    return pl.pallas_call(
        _mm_res_kernel,
        grid=(n // tn, bsz, seq // tm),
        in_specs=[pl.BlockSpec((1, tm, k), lambda j, b, i: (b, i, 0)),
                  pl.BlockSpec((k, tn), lambda j, b, i: (0, j)),
                  pl.BlockSpec((1, tm, tn), lambda j, b, i: (b, i, j)),
                  pl.BlockSpec((1, 1, tn), lambda j, b, i: (b, 0, j))],
        out_specs=pl.BlockSpec((1, tm, tn), lambda j, b, i: (b, i, j)),
        out_shape=jax.ShapeDtypeStruct((bsz, seq, n), F32),
        scratch_shapes=[pltpu.VMEM((k, tn), BF16)],
        compiler_params=_cparams(MM_SEMANTICS),
        name="matmul_res1",
    )(a, w, res, gate)


def _ssd_kernel(*refs, rev, nsteps, readout):
    if readout:
        act_ref, dt_ref, dtb_ref, alog_ref, s0_ref, yin_ref, z_ref, ng_ref, y_ref, sfin_ref, st_ref = refs
    else:
        (xbc_ref, prev_ref, next_ref, dt_ref, cw_ref, cb_ref, dtb_ref, alog_ref, dsk_ref, s0_ref,
         y_ref, act_ref, sfin_ref, pad_ref, st_ref) = refs
    step = pl.program_id(1)
    u = (nsteps - 1 - step) if rev else step
    ck = SSD_CHUNK
    tstep = dt_ref.shape[1]
    per_step = tstep // ck
    hoff = SSD_HEADS if rev else 0

    @pl.when(step == 0)
    def _():
        st_ref[...] = s0_ref[0]

    if not readout:
        pad_ref[HALO:HALO + tstep, :] = xbc_ref[0].astype(F32)
        pad_ref[0:HALO, :] = jnp.where(u > 0, prev_ref[0].astype(F32), 0.0)
        pad_ref[HALO + tstep:2 * HALO + tstep, :] = jnp.where(u < nsteps - 1, next_ref[0].astype(F32), 0.0)

    hlane = lax.broadcasted_iota(jnp.int32, (ck, LANES), 1)
    is_head = (hlane >= hoff) & (hlane < hoff + SSD_HEADS)
    neg_a = -jnp.exp(alog_ref[...])
    tri = _scan_tri(ck, rev).astype(BF16)
    last = 0 if rev else ck - 1
    hrow = lax.broadcasted_iota(jnp.int32, (LANES, SSD_INNER), 0)
    hcol = lax.broadcasted_iota(jnp.int32, (LANES, SSD_INNER), 1)
    expand = (hcol // SSD_HEAD_DIM + hoff == hrow).astype(BF16)
    r = lax.broadcasted_iota(jnp.int32, (ck, ck), 0)
    c = lax.broadcasted_iota(jnp.int32, (ck, ck), 1)
    causal = (c >= r) if rev else (c <= r)
    lane = lax.broadcasted_iota(jnp.int32, (ck, LANES), 1)
    groups = range(SSD_GROUPS)
    gw = [slice(g * SSD_GROUP_W, (g + 1) * SSD_GROUP_W) for g in groups]
    heads_per_group = SSD_HEADS // SSD_GROUPS

    def prepare(ci):
        rows = slice(ci * ck, (ci + 1) * ck)
        if readout:
            xbc = act_ref[0, rows, :].astype(F32)
        else:
            half = SSD_CONV // 2
            acc = cb_ref[...] + cw_ref[0:1, :] * pad_ref[pl.ds(ci * ck + HALO - half, ck), :]
            for k in range(1, SSD_CONV):
                acc = acc + cw_ref[k:k + 1, :] * pad_ref[pl.ds(ci * ck + HALO - half + k, ck), :]
            xbc = _silu(acc)
            act_ref[0, rows, :] = xbc.astype(act_ref.dtype)
        xs = xbc[:, :SSD_INNER]
        dt = jnp.where(is_head, jax.nn.softplus(dt_ref[0, rows, :] + dtb_ref[...]), 0.0)
        acum = _dot01(tri, dt * neg_a)
        total = acum[last:last + 1, :]
        acum_t = acum.T
        dt_x = _dot_v01(dt, expand)
        te_x = _dot_v01(jnp.exp(total - acum), expand)
        xdt = xs * dt_x
        xdt_b = xdt.astype(BF16)
        p = dict(xs=xs, ea_x=_dot_v01(jnp.exp(acum), expand), xw_b=(xdt * te_x).astype(BF16),
                 cd_x=_dot_v01(jnp.broadcast_to(jnp.exp(total), (SUBLANES, LANES)), expand)[0:1, :],
                 b_g=[xbc[:, SSD_INNER + g * SSD_STATE:SSD_INNER + (g + 1) * SSD_STATE].astype(BF16) for g in groups],
                 c_g=[xbc[:, SSD_INNER + SSD_BC + g * SSD_STATE:SSD_INNER + SSD_BC + (g + 1) * SSD_STATE].astype(BF16)
                      for g in groups],
                 decay=[], masked_x=[])
        for hd in range(SSD_HEADS):
            tile, half_i = hd // 2, hd % 2
            h = hoff + hd
            seg = acum[:, h:h + 1] - acum_t[h:h + 1, :]
            p["decay"].append(jnp.where(causal, jnp.exp(seg), 0.0))
            xpair = xdt_b[:, tile * LANES:(tile + 1) * LANES]
            in_half = (lane >= half_i * SSD_HEAD_DIM) & (lane < (half_i + 1) * SSD_HEAD_DIM)
            p["masked_x"].append(jnp.where(in_half, xpair, jnp.zeros_like(xpair)))
        return p

    order = list(range(per_step - 1, -1, -1) if rev else range(per_step))
    ready = prepare(order[0])
    for n, ci in enumerate(order):
        rows = slice(ci * ck, (ci + 1) * ck)
        p = ready
        if n + 1 < len(order):
            ready = prepare(order[n + 1])
        b_g, c_g = p["b_g"], p["c_g"]
        s_in = [st_ref[g] for g in groups]
        cb = [lax.dot_general(c_g[g], b_g[g], _NT, preferred_element_type=F32) for g in groups]
        y_off = [jnp.dot(c_g[g], s_in[g].astype(BF16), preferred_element_type=F32) for g in groups]
        chunk_state = [lax.dot_general(b_g[g], p["xw_b"][:, gw[g]], _TN, preferred_element_type=F32) for g in groups]
        for g in groups:
            st_ref[g] = p["cd_x"][:, gw[g]] * s_in[g] + chunk_state[g]
        decay_cb = [(cb[hd // heads_per_group] * p["decay"][hd]).astype(BF16) for hd in range(SSD_HEADS)]
        y_diag = [jnp.dot(decay_cb[hd], p["masked_x"][hd], preferred_element_type=F32) for hd in range(SSD_HEADS)]
        y_parts = []
        for tile in range(SSD_HEADS // 2):
            g, within = (2 * tile) // heads_per_group, tile % (heads_per_group // 2)
            y_tile = y_off[g][:, within * LANES:(within + 1) * LANES] * p["ea_x"][:, tile * LANES:(tile + 1) * LANES]
            y_parts.append(y_tile + y_diag[2 * tile] + y_diag[2 * tile + 1])
        y = jnp.concatenate(y_parts, axis=1)

        if readout:
            y = y + yin_ref[0, rows, :].astype(F32)
            y = y * _silu(z_ref[0, rows, :].astype(F32))
            ms = jnp.mean(y * y, axis=-1, keepdims=True)
            y_ref[0, rows, :] = (y * lax.rsqrt(ms + EPS) * ng_ref[...]).astype(y_ref.dtype)
        else:
            y_ref[0, rows, :] = (y + dsk_ref[...] * p["xs"]).astype(y_ref.dtype)

    @pl.when(step == nsteps - 1)
    def _():
        sfin_ref[0] = st_ref[...]


def _ssd_scan(xbc, dt_raw, dt_bias, a_log, s0, rev, conv_w=None, conv_b=None, d_skip=None,
              y_in=None, z=None, norm_g=None):
    bsz, seq, _ = xbc.shape
    tstep = min(seq, SSD_STEP)
    nsteps = seq // tstep
    hb = tstep // HALO
    nhalo = seq // HALO
    hoff = SSD_HEADS if rev else 0

    def cidx(i):
        return (nsteps - 1 - i) if rev else i

    def head_lanes(v):
        return jnp.pad(v.reshape(1, -1), ((0, 0), (hoff, LANES - hoff - v.shape[-1])))

    chunk = lambda width: pl.BlockSpec((1, tstep, width), lambda b, i: (b, cidx(i), 0))
    row = lambda width: pl.BlockSpec((1, width), lambda b, i: (0, 0))
    state_spec = pl.BlockSpec((1, SSD_GROUPS, SSD_STATE, SSD_GROUP_W), lambda b, i: (b, 0, 0, 0))
    state_shape = jax.ShapeDtypeStruct((bsz, SSD_GROUPS, SSD_STATE, SSD_GROUP_W), F32)
    y_shape = jax.ShapeDtypeStruct((bsz, seq, SSD_INNER), BF16)
    state_scratch = pltpu.VMEM((SSD_GROUPS, SSD_STATE, SSD_GROUP_W), F32)
    if rev:
        in_specs = [chunk(SSD_CONV_DIM), chunk(LANES), row(LANES), row(LANES), state_spec,
                    chunk(SSD_INNER), chunk(SSD_INNER), row(SSD_INNER)]
        args = [xbc, dt_raw, head_lanes(dt_bias), head_lanes(a_log), s0, y_in, z, norm_g.reshape(1, -1)]
        out_specs, out_shape = [chunk(SSD_INNER), state_spec], [y_shape, state_shape]
        scratch = [state_scratch]
    else:
        in_specs = [
            chunk(SSD_CONV_DIM),
            pl.BlockSpec((1, HALO, SSD_CONV_DIM), lambda b, i: (b, jnp.maximum(i * hb - 1, 0), 0)),
            pl.BlockSpec((1, HALO, SSD_CONV_DIM), lambda b, i: (b, jnp.minimum((i + 1) * hb, nhalo - 1), 0)),
            chunk(LANES),
            pl.BlockSpec((SUBLANES, SSD_CONV_DIM), lambda b, i: (0, 0)), row(SSD_CONV_DIM),
            row(LANES), row(LANES), row(SSD_INNER), state_spec]
        args = [xbc, xbc, xbc, dt_raw,
                jnp.pad(conv_w, ((0, SUBLANES - SSD_CONV), (0, 0))), conv_b.reshape(1, -1),
                head_lanes(dt_bias), head_lanes(a_log), jnp.repeat(d_skip, SSD_HEAD_DIM).reshape(1, -1), s0]
        out_specs = [chunk(SSD_INNER), chunk(SSD_CONV_DIM), state_spec]
        out_shape = [y_shape, jax.ShapeDtypeStruct((bsz, seq, SSD_CONV_DIM), BF16), state_shape]
        scratch = [pltpu.VMEM((tstep + 2 * HALO, SSD_CONV_DIM), F32), state_scratch]
    return pl.pallas_call(
        functools.partial(_ssd_kernel, rev=rev, nsteps=nsteps, readout=rev),
        grid=(bsz, nsteps),
        in_specs=in_specs, out_specs=out_specs, out_shape=out_shape, scratch_shapes=scratch,
        compiler_params=_cparams(("parallel", "arbitrary")),
        name="ssd_scan_rev" if rev else "ssd_scan_fwd",
    )(*args)


def _conformer_kernel(p_ref, w_ref, b_ref, g_ref, beta_ref, o_ref, pad_ref, rot_ref, conv_ref, *, nseq, slen):
    v = p_ref[0, :, :CF_CH].astype(F32)
    gate = p_ref[0, :, CF_CH:].astype(F32)
    u = v * jax.nn.sigmoid(gate)
    first = CF_PAD - CF_KERNEL // 2
    span = rot_ref.shape[2]
    for s in range(nseq):
        pad_ref[s, 0:CF_PAD, :] = jnp.zeros((CF_PAD, CF_CH), F32)
        pad_ref[s, CF_PAD + slen:2 * CF_PAD + slen, :] = jnp.zeros((CF_PAD, CF_CH), F32)
        pad_ref[s, CF_PAD:CF_PAD + slen, :] = u[s * slen:(s + 1) * slen, :]
        for res in range(1, SUBLANES):
            rot_ref[res - 1, s] = pad_ref[s, pl.ds(res, span), :]
    for s in range(nseq):
        for cb in range(CF_CH // LANES):
            ch = slice(cb * LANES, (cb + 1) * LANES)
            acc = jnp.broadcast_to(b_ref[:, ch], (slen, LANES))
            for k in range(CF_KERNEL):
                res, lead = (first + k) % SUBLANES, (first + k) // SUBLANES * SUBLANES
                src = pad_ref[s, pl.ds(lead, slen), ch] if res == 0 else rot_ref[res - 1, s, pl.ds(lead, slen), ch]
                acc = acc + w_ref[k:k + 1, ch] * src
            conv_ref[s * slen:(s + 1) * slen, ch] = acc
    y = conv_ref[...]
    mu = jnp.mean(y, axis=-1, keepdims=True)
    xc = y - mu
    var = jnp.mean(xc * xc, axis=-1, keepdims=True)
    y = xc * lax.rsqrt(var + EPS) * g_ref[...] + beta_ref[...]
    o_ref[0] = _silu(y).astype(o_ref.dtype)


def _conformer(p_cf, dw_w, dw_b, ln_g, ln_b, slen):
    bsz, seq, _ = p_cf.shape
    nseq = max(1, min(seq, 256) // slen)
    tb = nseq * slen
    kpad = -(-CF_KERNEL // SUBLANES) * SUBLANES
    last_tap_row = CF_PAD - CF_KERNEL // 2 + CF_KERNEL - 1
    span = slen + last_tap_row // SUBLANES * SUBLANES
    return pl.pallas_call(
        functools.partial(_conformer_kernel, nseq=nseq, slen=slen),
        grid=(bsz, seq // tb),
        in_specs=[pl.BlockSpec((1, tb, 2 * CF_CH), lambda b, i: (b, i, 0)),
                  pl.BlockSpec((kpad, CF_CH), lambda b, i: (0, 0)),
                  pl.BlockSpec((1, CF_CH), lambda b, i: (0, 0)),
                  pl.BlockSpec((1, CF_CH), lambda b, i: (0, 0)),
                  pl.BlockSpec((1, CF_CH), lambda b, i: (0, 0))],
        out_specs=pl.BlockSpec((1, tb, CF_CH), lambda b, i: (b, i, 0)),
        out_shape=jax.ShapeDtypeStruct((bsz, seq, CF_CH), BF16),
        scratch_shapes=[pltpu.VMEM((nseq, slen + 2 * CF_PAD, CF_CH), F32),
                        pltpu.VMEM((SUBLANES - 1, nseq, span, CF_CH), F32),
                        pltpu.VMEM((tb, CF_CH), F32)],
        compiler_params=_cparams(("parallel", "parallel")),
        name="conformer_conv",
    )(p_cf, jnp.pad(dw_w, ((0, kpad - CF_KERNEL), (0, 0))), dw_b.reshape(1, -1), ln_g.reshape(1, -1),
      ln_b.reshape(1, -1))


def _hgrn2_kernel(*refs, rev, mode):
    if mode == "state":
        v_ref, f_ref, lb_ref, s0_ref, sfin_ref, st_ref = refs
    elif mode == "out":
        q_ref, v_ref, f_ref, lb_ref, s0_ref, o_ref, st_ref = refs
    else:
        q_ref, v_ref, f_ref, lb_ref, s0_ref, oin_ref, g_ref, ng_ref, perm_ref, o_ref, st_ref, osc_ref = refs
    step = pl.program_id(1)
    ck = HG_CHUNK
    per_step = v_ref.shape[1] // ck

    @pl.when(step == 0)
    def _():
        st_ref[...] = s0_ref[0]

    lb = lb_ref[...]
    tri = _scan_tri(ck, rev).astype(BF16)
    last = 0 if rev else ck - 1
    mid_pos = HG_CHUNK // 2 - 1
    mid = (ck - 1 - mid_pos) if rev else mid_pos
    r = lax.broadcasted_iota(jnp.int32, (ck, ck), 0)
    c = lax.broadcasted_iota(jnp.int32, (ck, ck), 1)
    causal = (c >= r) if rev else (c <= r)

    heads = range(HG_HEADS)
    ks = [slice(h * HG_DK, (h + 1) * HG_DK) for h in heads]
    vs = [slice(h * HG_DV, (h + 1) * HG_DV) for h in heads]

    def prepare(ci):
        rows = slice(ci * ck, (ci + 1) * ck)
        f = lb + (1.0 - lb) * jax.nn.sigmoid(f_ref[0, rows, :])
        kk = 1.0 - f
        gcum = _dot01(tri, jnp.log(f))
        g_end = gcum[last:last + 1, :]
        p = dict(k_end=(kk * jnp.exp(g_end - gcum)).astype(BF16), dec_end=jnp.exp(g_end))
        if mode != "state":
            g_mid = gcum[mid:mid + 1, :]
            q = _silu(q_ref[0, rows, :].astype(F32))
            p.update(q_rel=(q * jnp.exp(gcum - g_mid)).astype(BF16), k_rel=(kk * jnp.exp(g_mid - gcum)).astype(BF16),
                     q_dec=(q * jnp.exp(gcum)).astype(BF16))
        if mode == "readout":
            p.update(gate=_silu(g_ref[0, rows, :].astype(F32)))
        return p

    order = list(range(per_step - 1, -1, -1) if rev else range(per_step))
    ready = prepare(order[0])
    for n, ci in enumerate(order):
        rows = slice(ci * ck, (ci + 1) * ck)
        p = ready
        k_end, dec_end = p["k_end"], p["dec_end"]
        v = v_ref[0, rows, :]
        if mode != "state":
            q_rel, k_rel, q_dec = p["q_rel"], p["k_rel"], p["q_dec"]
        if mode == "readout":
            gate = p["gate"]
        if n + 1 < len(order):
            ready = prepare(order[n + 1])

        s_in = [st_ref[h] for h in heads]
        if mode != "state":
            att = [lax.dot_general(q_rel[:, ks[h]], k_rel[:, ks[h]], _NT, preferred_element_type=F32) for h in heads]
            o_inter = [lax.dot_general(q_dec[:, ks[h]], s_in[h].astype(BF16), _NT, preferred_element_type=F32)
                       for h in heads]
            att = [jnp.where(causal, a, 0.0).astype(BF16) for a in att]
            o = [jnp.dot(att[h], v[:, vs[h]], preferred_element_type=F32) + o_inter[h] for h in heads]
        chunk_state = [lax.dot_general(v[:, vs[h]], k_end[:, ks[h]], _TN, preferred_element_type=F32) for h in heads]
        for h in heads:
            st_ref[h] = s_in[h] * dec_end[:, ks[h]] + chunk_state[h]
        if mode == "readout":
            for h in heads:
                o_h = o[h] + oin_ref[0, rows, vs[h]].astype(F32)
                ms = jnp.mean(o_h * o_h, axis=-1, keepdims=True)
                o_h = (o_h * lax.rsqrt(ms + EPS) * ng_ref[...]) * gate[:, vs[h]]
                osc_ref[rows, vs[h]] = o_h.astype(osc_ref.dtype)
        elif mode == "out":
            for h in heads:
                o_ref[0, rows, vs[h]] = o[h].astype(o_ref.dtype)

    if mode == "readout":
        nat = jnp.dot(perm_ref[...], osc_ref[...], preferred_element_type=F32).astype(o_ref.dtype)
        ncols = o_ref.shape[2]
        for gr in range(o_ref.shape[1]):
            o_ref[0, gr] = nat[gr * ncols:(gr + 1) * ncols, :]
    if mode == "state":
        @pl.when(step == pl.num_programs(1) - 1)
        def _():
            sfin_ref[0] = st_ref[...]


def _hgrn2_scan(v, f_raw, lb, s0, rev, mode, q=None, o_in=None, g=None, norm_g=None):
    bsz, seq, _ = v.shape
    tstep = min(seq, HG_STEP)
    nsteps = seq // tstep
    dcol = 1 if rev else 0

    def cidx(i):
        return (nsteps - 1 - i) if rev else i

    tok = lambda width: pl.BlockSpec((1, tstep, width), lambda b, i: (b, cidx(i), 0))
    state_spec = pl.BlockSpec((1, HG_HEADS, HG_DV, HG_DK), lambda b, i: (b, 0, 0, 0))
    f_spec = pl.BlockSpec((1, tstep, HG_KEY), lambda b, i: (b, cidx(i), dcol))
    lb_spec = pl.BlockSpec((1, HG_KEY), lambda b, i: (0, 0))
    state_shape = jax.ShapeDtypeStruct((bsz, HG_HEADS, HG_DV, HG_DK), F32)
    scratch = [pltpu.VMEM((HG_HEADS, HG_DV, HG_DK), F32)]
    if mode == "state":
        in_specs, args = [tok(HG_VAL), f_spec, lb_spec, state_spec], [v, f_raw, lb.reshape(1, -1), s0]
        out_specs, out_shape = state_spec, state_shape
    else:
        in_specs = [tok(HG_KEY), tok(HG_VAL), f_spec, lb_spec, state_spec]
        args = [q, v, f_raw, lb.reshape(1, -1), s0]
        out_specs, out_shape = tok(HG_VAL), jax.ShapeDtypeStruct((bsz, seq, HG_VAL), BF16)
        if mode == "readout":
            grid_rows = seq // GRID_W
            ncols = tstep // grid_rows
            tok_id = jnp.arange(tstep)
            src = (tok_id % ncols) * grid_rows + tok_id // ncols
            perm = (src[:, None] == tok_id[None, :]).astype(BF16)
            in_specs += [tok(HG_VAL), tok(HG_VAL), pl.BlockSpec((1, HG_DV), lambda b, i: (0, 0)),
                         pl.BlockSpec((tstep, tstep), lambda b, i: (0, 0))]
            args += [o_in, g, norm_g.reshape(1, -1), perm]
            out_specs = pl.BlockSpec((1, grid_rows, ncols, HG_VAL), lambda b, i: (b, 0, cidx(i), 0))
            out_shape = jax.ShapeDtypeStruct((bsz, grid_rows, GRID_W, HG_VAL), BF16)
            scratch.append(pltpu.VMEM((tstep, HG_VAL), BF16))
    out = pl.pallas_call(
        functools.partial(_hgrn2_kernel, rev=rev, mode=mode),
        grid=(bsz, nsteps),
        in_specs=in_specs, out_specs=out_specs, out_shape=out_shape,
        scratch_shapes=scratch,
        compiler_params=_cparams(("parallel", "arbitrary")),
        name=f"hgrn2_{mode}_{'rev' if rev else 'fwd'}",
    )(*args)
    return out.reshape(bsz, seq, HG_VAL) if mode == "readout" else out


def _first_argmax(vals):
    best, idx = vals[0], jnp.zeros(vals[0].shape, jnp.int32)
    for j in range(1, len(vals)):
        better = vals[j] > best
        idx = jnp.where(better, j, idx)
        best = jnp.where(better, vals[j], best)
    return idx, best


def _pick(idx, vals):
    out = vals[-1]
    for j in range(len(vals) - 2, -1, -1):
        out = jnp.where(idx == j, vals[j], out)
    return out


def _store_token_tiles(ref, v):
    n = v.shape[0]
    for j in range(SUBLANES):
        ref[pl.ds(j, n, stride=SUBLANES), :] = v[:, j * LANES:(j + 1) * LANES]


def _load_token_tiles(ref):
    n = ref.shape[0] // SUBLANES
    return jnp.concatenate([ref[pl.ds(j, n, stride=SUBLANES), :] for j in range(SUBLANES)], axis=1)


def _token_tile(ref, idx):
    return ref.at[pl.ds(pl.multiple_of(idx * SUBLANES, SUBLANES), SUBLANES)]


def _router_kernel(x_ref, g_ref, sh_ref, sc_ref, rw_ref, rb_ref, h_ref, slot_ref, wt_ref, cnt_ref, carry_ref):
    step = pl.program_id(0)

    @pl.when(step == 0)
    def _():
        carry_ref[...] = jnp.zeros_like(carry_ref)

    h = _normmod(x_ref[...], g_ref[...], sh_ref[0], sc_ref[0])
    _store_token_tiles(h_ref, h)
    scores = jax.nn.sigmoid(_dot_f32(rw_ref[...], h, _NT))
    sel = scores + rb_ref[...]
    srow = [sel[e:e + 1, :] for e in range(N_EXPERTS)]
    prow = [scores[e:e + 1, :] for e in range(N_EXPERTS)]
    gscore = []
    for gi in range(N_EXPERT_GROUPS):
        m = srow[gi * EXPERTS_PER_GROUP:(gi + 1) * EXPERTS_PER_GROUP]
        pair_sums = [m[i] + m[j] for i in range(EXPERTS_PER_GROUP) for j in range(i + 1, EXPERTS_PER_GROUP)]
        best = pair_sums[0]
        for p in pair_sums[1:]:
            best = jnp.maximum(best, p)
        gscore.append(best)
    gidx, _ = _first_argmax(gscore)
    in_sel = [_pick(gidx, [srow[gi * EXPERTS_PER_GROUP + j] for gi in range(N_EXPERT_GROUPS)])
              for j in range(EXPERTS_PER_GROUP)]
    in_p = [_pick(gidx, [prow[gi * EXPERTS_PER_GROUP + j] for gi in range(N_EXPERT_GROUPS)])
            for j in range(EXPERTS_PER_GROUP)]
    i1, _ = _first_argmax(in_sel)
    i2, _ = _first_argmax([jnp.where(i1 == j, -jnp.inf, in_sel[j]) for j in range(EXPERTS_PER_GROUP)])
    w1, w2 = _pick(i1, in_p), _pick(i2, in_p)
    den = w1 + w2
    e1, e2 = gidx * EXPERTS_PER_GROUP + i1, gidx * EXPERTS_PER_GROUP + i2
    tl = h.shape[0]
    erow = lax.broadcasted_iota(jnp.int32, (N_EXPERTS, tl), 0)
    oh1, oh2 = erow == e1, erow == e2
    cnt = (oh1 | oh2).astype(F32)
    r = lax.broadcasted_iota(jnp.int32, (tl, tl), 0)
    c = lax.broadcasted_iota(jnp.int32, (tl, tl), 1)
    before = jnp.dot(cnt.astype(BF16), (r < c).astype(BF16), preferred_element_type=F32)
    base = carry_ref[:, 0:1] + before
    rank1 = jnp.sum(jnp.where(oh1, base, 0.0), axis=0, keepdims=True).astype(jnp.int32)
    rank2 = jnp.sum(jnp.where(oh2, base, 0.0), axis=0, keepdims=True).astype(jnp.int32)
    carry_ref[...] = carry_ref[...] + jnp.sum(cnt, axis=1, keepdims=True)
    cnt_ref[...] = carry_ref[...].astype(jnp.int32)
    srow = lax.broadcasted_iota(jnp.int32, (SUBLANES, tl), 0)
    slot_ref[0] = jnp.where(srow == 0, e1, jnp.where(srow == 1, e2, jnp.where(srow == 2, rank1, rank2)))
    wrow = lax.broadcasted_iota(jnp.int32, (LANES, tl), 0)
    wt_ref[...] = jnp.where(wrow == 0, w1 / den, jnp.where(wrow == 1, w2 / den, 0.0)).T


def _ffn_norm_router(x, g, shift, scale, router_w, router_b, seq):
    t, d = x.shape
    tl = min(seq, MOE_TL)
    nblk = t // tl
    nb = shift.shape[0]
    mod_spec = pl.BlockSpec((1, 1, d), (lambda i: (i * tl // seq, 0, 0)) if nb > 1 else (lambda i: (0, 0, 0)))
    return pl.pallas_call(
        _router_kernel,
        grid=(nblk,),
        in_specs=[pl.BlockSpec((tl, d), lambda i: (i, 0)),
                  pl.BlockSpec((1, d), lambda i: (0, 0)),
                  mod_spec, mod_spec,
                  pl.BlockSpec((N_EXPERTS, d), lambda i: (0, 0)),
                  pl.BlockSpec((N_EXPERTS, 1), lambda i: (0, 0))],
        out_specs=[pl.BlockSpec((tl * SUBLANES, LANES), lambda i: (i, 0)),
                   pl.BlockSpec((1, SUBLANES, tl), lambda i: (i, 0, 0)),
                   pl.BlockSpec((tl, LANES), lambda i: (i, 0)),
                   pl.BlockSpec((N_EXPERTS, LANES), lambda i: (0, 0))],
        out_shape=[jax.ShapeDtypeStruct((t * SUBLANES, LANES), F32),
                   jax.ShapeDtypeStruct((nblk, SUBLANES, tl), jnp.int32),
                   jax.ShapeDtypeStruct((t, LANES), F32),
                   jax.ShapeDtypeStruct((N_EXPERTS, LANES), jnp.int32)],
        scratch_shapes=[pltpu.VMEM((N_EXPERTS, LANES), F32)],
        compiler_params=_cparams(("arbitrary",)),
        name="ffn_norm_router",
    )(x, g.reshape(1, d), shift, scale, router_w.T, router_b.reshape(N_EXPERTS, 1))


DMA_UNROLL = 8


def _dispatch_kernel(slot_ref, ends_ref, h_ref, hs_ref, zero_ref, sem, *, tl, max_tiles):
    @pl.when(pl.program_id(0) == 0)
    def _():
        zero_ref[...] = jnp.zeros_like(zero_ref)
        fills = []
        for e in range(N_EXPERTS):
            end = ends_ref[e]
            gap = (MOE_TM - (end & (MOE_TM - 1))) & (MOE_TM - 1)
            for bit in range(MOE_TM.bit_length() - 1):
                size = 1 << bit
                first_row = pl.multiple_of((end + (gap & (size - 1))) * SUBLANES, SUBLANES)
                fills.append(((gap & size) != 0,
                              pltpu.make_async_copy(zero_ref.at[pl.ds(0, size * SUBLANES)],
                                                    hs_ref.at[pl.ds(first_row, size * SUBLANES)], sem)))
        for n in range(N_EXPERTS):
            first_slot = ends_ref[N_EXPERTS] + n * MOE_TM
            first_row = pl.multiple_of(first_slot * SUBLANES, SUBLANES)
            fills.append((first_slot < max_tiles * MOE_TM,
                          pltpu.make_async_copy(zero_ref, hs_ref.at[pl.ds(first_row, MOE_TM * SUBLANES)], sem)))
        for cond, cp in fills:
            pl.when(cond)(cp.start)
        for cond, cp in fills:
            pl.when(cond)(cp.wait)

    def issue(t, carry):
        pltpu.make_async_copy(_token_tile(h_ref, t), _token_tile(hs_ref, slot_ref[0, 0, t]), sem).start()
        pltpu.make_async_copy(_token_tile(h_ref, t), _token_tile(hs_ref, slot_ref[0, 0, tl + t]), sem).start(priority=1)
        return carry

    lax.fori_loop(0, tl, issue, 0, unroll=DMA_UNROLL)
    for _ in range(2):
        pltpu.make_async_copy(h_ref, hs_ref.at[pl.ds(0, tl * SUBLANES)], sem).wait()


def _dispatch(h, slot2, ends, tl, max_tiles):
    t = h.shape[0] // SUBLANES
    nblk = t // tl
    return pl.pallas_call(
        functools.partial(_dispatch_kernel, tl=tl, max_tiles=max_tiles),
        grid=(nblk,),
        in_specs=[pl.BlockSpec((1, 1, 2 * tl), lambda i: (i, 0, 0), memory_space=pltpu.SMEM),
                  pl.BlockSpec(memory_space=pltpu.SMEM),
                  pl.BlockSpec((tl * SUBLANES, LANES), lambda i: (i, 0))],
        out_specs=pl.BlockSpec(memory_space=pl.ANY),
        out_shape=jax.ShapeDtypeStruct((max_tiles * MOE_TM * SUBLANES, LANES), F32),
        scratch_shapes=[pltpu.VMEM((MOE_TM * SUBLANES, LANES), F32), pltpu.SemaphoreType.DMA(())],
        compiler_params=_cparams(("arbitrary",)),
        name="moe_dispatch",
    )(slot2, ends, h)


def _expert_kernel(te_ref, tf_ref, nt_ref, hs_ref, wg_ref, wu_ref, wd_ref, ys_ref, wgb_ref, wub_ref, wdb_ref):
    n = pl.program_id(0)

    @pl.when(n >= nt_ref[0])
    def _():
        ys_ref[...] = jnp.zeros_like(ys_ref)

    @pl.when(n < nt_ref[0])
    def _():
        @pl.when(tf_ref[n] == 1)
        def _():
            wgb_ref[...] = wg_ref[0].astype(BF16)
            wub_ref[...] = wu_ref[0].astype(BF16)
            wdb_ref[...] = wd_ref[0].astype(BF16)

        h = _load_token_tiles(hs_ref).astype(BF16)
        a = jnp.dot(h, wgb_ref[...], preferred_element_type=F32)
        b = jnp.dot(h, wub_ref[...], preferred_element_type=F32)
        y = jnp.dot((_silu(a) * b).astype(BF16), wdb_ref[...], preferred_element_type=F32)
        _store_token_tiles(ys_ref, y)


def _experts(hs, tile_expert, tile_first, n_tiles, w_gate, w_up, w_down, layer, max_tiles):
    d = w_gate.shape[2]
    tile_spec = pl.BlockSpec((MOE_TM * SUBLANES, LANES), lambda n, te, tf, nt: (n, 0))
    grid_spec = pltpu.PrefetchScalarGridSpec(
        num_scalar_prefetch=3,
        grid=(max_tiles,),
        in_specs=[tile_spec,
                  pl.BlockSpec((None, 1, d, D_EXPERT), lambda n, te, tf, nt: (layer, te[n], 0, 0)),
                  pl.BlockSpec((None, 1, d, D_EXPERT), lambda n, te, tf, nt: (layer, te[n], 0, 0)),
                  pl.BlockSpec((None, 1, D_EXPERT, d), lambda n, te, tf, nt: (layer, te[n], 0, 0))],
        out_specs=tile_spec,
        scratch_shapes=[pltpu.VMEM((d, D_EXPERT), BF16), pltpu.VMEM((d, D_EXPERT), BF16),
                        pltpu.VMEM((D_EXPERT, d), BF16)])
    return pl.pallas_call(
        _expert_kernel,
        grid_spec=grid_spec,
        out_shape=jax.ShapeDtypeStruct((max_tiles * MOE_TM * SUBLANES, LANES), F32),
        compiler_params=_cparams(("arbitrary",)),
        name="moe_experts",
    )(tile_expert, tile_first, n_tiles, hs, w_gate, w_up, w_down)


def _combine_kernel(slot_ref, nslot_ref, wt_ref, res_ref, gate_ref, fg_ref, ys_ref, o_ref, y1_ref, y2_ref, sems, *,
                    tl, final):
    step, nsteps = pl.program_id(0), pl.num_programs(0)
    cur = step % 2

    def start_gather(sref, buf):
        def issue(t, carry):
            pltpu.make_async_copy(_token_tile(ys_ref, sref[0, 0, t]), _token_tile(y1_ref.at[buf], t),
                                  sems.at[buf]).start()
            pltpu.make_async_copy(_token_tile(ys_ref, sref[0, 0, tl + t]), _token_tile(y2_ref.at[buf], t),
                                  sems.at[buf]).start(priority=1)
            return carry

        lax.fori_loop(0, tl, issue, 0, unroll=DMA_UNROLL)

    @pl.when(step == 0)
    def _():
        start_gather(slot_ref, 0)

    @pl.when(step + 1 < nsteps)
    def _():
        start_gather(nslot_ref, 1 - cur)

    pltpu.make_async_copy(ys_ref.at[pl.ds(0, tl * SUBLANES)], y1_ref.at[cur], sems.at[cur]).wait()
    pltpu.make_async_copy(ys_ref.at[pl.ds(0, tl * SUBLANES)], y2_ref.at[cur], sems.at[cur]).wait()
    wt = wt_ref[...]
    moe = wt[:, 0:1] * _load_token_tiles(y1_ref.at[cur]) + wt[:, 1:2] * _load_token_tiles(y2_ref.at[cur])
    x = res_ref[...] + gate_ref[0] * moe
    if final:
        ms = jnp.mean(x * x, axis=-1, keepdims=True)
        x = x * lax.rsqrt(ms + EPS) * fg_ref[...]
    o_ref[...] = x


def _combine(ys, slot2, wt, res, gate, final_g, seq, tl):
    t, d = res.shape
    nblk = t // tl
    nb = gate.shape[0]
    final = final_g is not None
    fg = final_g.reshape(1, d) if final else jnp.ones((1, d), F32)
    return pl.pallas_call(
        functools.partial(_combine_kernel, tl=tl, final=final),
        grid=(nblk,),
        in_specs=[pl.BlockSpec((1, 1, 2 * tl), lambda i: (i, 0, 0), memory_space=pltpu.SMEM),
                  pl.BlockSpec((1, 1, 2 * tl), lambda i: (jnp.minimum(i + 1, nblk - 1), 0, 0),
                               memory_space=pltpu.SMEM),
                  pl.BlockSpec((tl, LANES), lambda i: (i, 0)),
                  pl.BlockSpec((tl, d), lambda i: (i, 0)),
                  pl.BlockSpec((1, 1, d), (lambda i: (i * tl // seq, 0, 0)) if nb > 1 else (lambda i: (0, 0, 0))),
                  pl.BlockSpec((1, d), lambda i: (0, 0)),
                  pl.BlockSpec(memory_space=pl.ANY)],
        out_specs=pl.BlockSpec((tl, d), lambda i: (i, 0)),
        out_shape=jax.ShapeDtypeStruct((t, d), F32),
        scratch_shapes=[pltpu.VMEM((2, tl * SUBLANES, LANES), F32), pltpu.VMEM((2, tl * SUBLANES, LANES), F32),
                        pltpu.SemaphoreType.DMA((2,))],
        compiler_params=_cparams(("arbitrary",)),
        name="moe_combine",
    )(slot2, slot2, wt, res, gate, fg, ys)


def _even_layer_mixer(xs, mods, norm_g, w_in, conv_w, conv_b, dt_bias, a_log, d_skip, ssd_g,
                      cf_w, cf_b, cf_lng, cf_lnb, w_out, rows):
    s_lo = SSD_INNER
    dt_lo = SSD_INNER + SSD_CONV_DIM
    cf_lo = dt_lo + 2 * SSD_HEADS
    w_cf = _realign_cols(w_in, cf_lo, 2 * CF_CH)
    bsz = xs[0].shape[0]
    zero_state = jnp.zeros((bsz, SSD_GROUPS, SSD_STATE, SSD_GROUP_W), F32)
    proj = []
    for x, m in zip(xs, mods):
        h = _norm_mod(x, norm_g, m[0], m[1])
        proj.append(dict(z=_matmul(h, w_in, BF16, 0, s_lo), xbc=_matmul(h, w_in, BF16, s_lo, SSD_CONV_DIM),
                         dt=_matmul(h, w_in, F32, dt_lo, LANES), cf=_matmul(h, w_cf, BF16)))
    y_fwd, act, o_ssd = [None, None], [None, None], [None, None]
    state = zero_state
    for si in range(2):
        p = proj[si]
        y_fwd[si], act[si], state = _ssd_scan(p["xbc"], p["dt"], dt_bias[0], a_log[0], state, False,
                                              conv_w=conv_w, conv_b=conv_b, d_skip=d_skip)
    state = zero_state
    for si in range(2):
        p = proj[si]
        o_ssd[si], state = _ssd_scan(act[si], p["dt"], dt_bias[1], a_log[1], state, True,
                                     y_in=y_fwd[si], z=p["z"], norm_g=ssd_g)
    outs = []
    for si, (x, m) in enumerate(zip(xs, mods)):
        slen = x.shape[1] if si == 0 else GRID_W
        o_cf = _conformer(proj[si]["cf"], cf_w, cf_b, cf_lng, cf_lnb, slen)
        outs.append(_matmul_res2(o_ssd[si], o_cf, w_out, x, m[2]))
    return outs


def _odd_layer_mixer(xc, xl, m_c, m_l, norm_g, w_in, lb, hg_g, w_out):
    st0 = HG_KEY + HG_VAL
    f0 = st0 + HG_VAL
    bsz = xl.shape[0]
    h_c = _norm_mod(xc, norm_g, m_c[0], m_c[1])
    h_l = _norm_mod_colmajor(xl, norm_g, m_l[0], m_l[1])
    v_c, f_c = _matmul(h_c, w_in, BF16, st0, HG_VAL), _matmul(h_c, w_in, F32, f0, 2 * HG_KEY)
    q_l, g_l = _matmul(h_l, w_in, BF16, 0, HG_KEY), _matmul(h_l, w_in, BF16, HG_KEY, HG_VAL)
    v_l, f_l = _matmul(h_l, w_in, BF16, st0, HG_VAL), _matmul(h_l, w_in, F32, f0, 2 * HG_KEY)
    zero_state = jnp.zeros((bsz, HG_HEADS, HG_DV, HG_DK), F32)
    s_f = _hgrn2_scan(v_c, f_c, lb, zero_state, False, "state")
    s_r = _hgrn2_scan(v_c, f_c, lb, zero_state, True, "state")
    o_f = _hgrn2_scan(v_l, f_l, lb, s_f, False, "out", q=q_l)
    o = _hgrn2_scan(v_l, f_l, lb, s_r, True, "readout", q=q_l, o_in=o_f, g=g_l, norm_g=hg_g)
    return _matmul_res(o, w_out, xl, m_l[2])


def _dispatch_layout(counts, slot, max_tiles, tl_move):
    tiles = (counts + MOE_TM - 1) // MOE_TM
    tile_ends = jnp.cumsum(tiles)
    tile_starts = tile_ends - tiles
    total = tile_ends[-1]
    starts = tile_starts * MOE_TM
    ends = jnp.concatenate([starts + counts, (total * MOE_TM).reshape(1)]).astype(jnp.int32)
    experts = jnp.arange(N_EXPERTS, dtype=jnp.int32)

    def lookup(table, idx):
        return jnp.sum(jnp.where(idx[..., None] == experts, table, 0), axis=-1)

    choice_slots = [(lookup(starts, slot[:, k]) + slot[:, 2 + k]).astype(jnp.int32) for k in range(2)]
    grouped = [jnp.concatenate([s.reshape(-1, tl) for s in choice_slots], axis=-1)[:, None, :] for tl in tl_move]
    n = jnp.arange(max_tiles, dtype=jnp.int32)
    nn = jnp.minimum(n, total - 1)
    expert = jnp.sum((nn[:, None] >= tile_ends[None, :]).astype(jnp.int32), axis=1)
    is_first = ((nn == lookup(tile_starts, expert)) & (n < total)).astype(jnp.int32)
    return grouped, ends, expert, is_first, total.reshape(1).astype(jnp.int32)


def _moe_block(x, m, norm_g, router_w, router_b, w_gate, w_up, w_down, layer, final_g=None):
    bsz, seq, d = x.shape
    t = bsz * seq
    xf = x.reshape(t, d)
    tl_d, tl_c = min(seq, MOE_TL_DISPATCH), min(seq, MOE_TL)
    h, slot, wt, cnt = _ffn_norm_router(xf, norm_g, m[3], m[4], router_w, router_b, seq)
    max_tiles = 2 * t // MOE_TM + N_EXPERTS
    (slot_d, slot_c), ends, tile_expert, tile_first, n_tiles = _dispatch_layout(cnt[:, 0], slot, max_tiles,
                                                                                (tl_d, tl_c))
    hs = _dispatch(h, slot_d, ends, tl_d, max_tiles)
    ys = _experts(hs, tile_expert, tile_first, n_tiles, w_gate, w_up, w_down, layer, max_tiles)
    return _combine(ys, slot_c, wt, xf, m[5], final_g, seq, tl_c).reshape(bsz, seq, d)


def kernel(x, c, ctx, c_ctx, mod_w, mod_b, norm_mix_g, norm_ffn_g, router_w, router_b, moe_w_gate, moe_w_up,
           moe_w_down, ab_w_in, ssd_conv_w, ssd_conv_b, ssd_dt_bias, ssd_a_log, ssd_d, ssd_norm_g, cf_dw_w,
           cf_dw_b, cf_ln_g, cf_ln_b, ab_w_out, hg_w_in, hg_lb, hg_norm_g, hg_w_out, final_norm_g):
    depth = mod_w.shape[0]
    assert depth == 2, "layer schedule below is written for one even and one odd layer"
    bsz, seq, d = x.shape
    rows = seq // GRID_W
    lb_all = jnp.cumsum(jax.nn.softmax(hg_lb.astype(F32), axis=0), axis=0)
    lb_all = lb_all - lb_all[0]

    nrow = -(-(bsz + 1) // SUBLANES) * SUBLANES
    cond = jnp.zeros((nrow, d), F32).at[:bsz].set(c).at[bsz].set(c_ctx)
    mod = _modulation(cond, mod_w, mod_b).reshape(depth, nrow, N_MOD, 1, d)

    def mods(l):
        m_l = [mod[l, :bsz, k] for k in range(N_MOD)]
        m_c = [mod[l, bsz:bsz + 1, k] for k in range(N_MOD)]
        return m_c, m_l

    m_c, m_l = mods(0)
    xc, xl = _even_layer_mixer((ctx, x), (m_c, m_l), norm_mix_g[0], ab_w_in[0], ssd_conv_w[0], ssd_conv_b[0],
                               ssd_dt_bias[0], ssd_a_log[0], ssd_d[0], ssd_norm_g[0], cf_dw_w[0], cf_dw_b[0],
                               cf_ln_g[0], cf_ln_b[0], ab_w_out[0], rows)
    xl = _moe_block(xl, m_l, norm_ffn_g[0], router_w, router_b, moe_w_gate, moe_w_up, moe_w_down, 0)
    xc = _moe_block(xc, m_c, norm_ffn_g[0], router_w, router_b, moe_w_gate, moe_w_up, moe_w_down, 0)

    m_c, m_l = mods(1)
    xl = _odd_layer_mixer(xc, xl, m_c, m_l, norm_mix_g[1], hg_w_in[0], lb_all[1], hg_norm_g[0], hg_w_out[0])
    return _moe_block(xl, m_l, norm_ffn_g[1], router_w, router_b, moe_w_gate, moe_w_up, moe_w_down, 1,
                      final_g=final_norm_g)
```

```python
import functools

import jax
import jax.numpy as jnp
from jax import lax
from jax.experimental import pallas as pl
from jax.experimental.pallas import tpu as pltpu

F32 = jnp.float32
BF16 = jnp.bfloat16

D_MODEL = 1024
GRID_W = 64
EPS = 1e-6
N_MOD = 6

SSD_HEADS = 16
SSD_HEAD_DIM = 64
SSD_INNER = SSD_HEADS * SSD_HEAD_DIM
SSD_GROUPS = 4
SSD_STATE = 128
SSD_CONV = 5
SSD_CHUNK = 128
SSD_STEP = 512
SSD_BC = SSD_GROUPS * SSD_STATE
SSD_CONV_DIM = SSD_INNER + 2 * SSD_BC
SSD_GROUP_W = SSD_INNER // SSD_GROUPS

CF_CH = 1024
CF_KERNEL = 31
CF_PAD = 16

HG_HEADS = 8
HG_DK = 128
HG_DV = 128
HG_KEY = HG_HEADS * HG_DK
HG_VAL = HG_HEADS * HG_DV
HG_CHUNK = 64
HG_STEP = 512

N_EXPERTS = 16
N_EXPERT_GROUPS = 4
EXPERTS_PER_GROUP = 4
D_EXPERT = 512
MOE_TL = 512
MOE_TL_DISPATCH = 2048
MOE_TM = 512

LANES = 128
SUBLANES = 8
HALO = 16
CM_COLS = 16
VMEM_LIMIT = 48 * 1024 * 1024


def _cparams(sem):
    return pltpu.CompilerParams(dimension_semantics=sem, vmem_limit_bytes=VMEM_LIMIT)


def _silu(x):
    return x * jax.nn.sigmoid(x)


def _split3(v):
    hi = v.astype(BF16)
    r1 = v - hi.astype(F32)
    mid = r1.astype(BF16)
    lo = (r1 - mid.astype(F32)).astype(BF16)
    return hi, mid, lo


def _dot01(m01, v):
    hi, mid, lo = _split3(v)
    out = jnp.dot(m01, lo, preferred_element_type=F32)
    out = out + jnp.dot(m01, mid, preferred_element_type=F32)
    return out + jnp.dot(m01, hi, preferred_element_type=F32)


def _dot_v01(v, m01):
    hi, mid, lo = _split3(v)
    out = jnp.dot(lo, m01, preferred_element_type=F32)
    out = out + jnp.dot(mid, m01, preferred_element_type=F32)
    return out + jnp.dot(hi, m01, preferred_element_type=F32)


_NN = (((1,), (0,)), ((), ()))
_NT = (((1,), (1,)), ((), ()))
_TN = (((0,), (0,)), ((), ()))


def _dot_f32(a, b, dn):
    a1, a2, a3 = _split3(a)
    b1, b2, b3 = _split3(b)
    out = lax.dot_general(a3, b1, dn, preferred_element_type=F32)
    out = out + lax.dot_general(a1, b3, dn, preferred_element_type=F32)
    out = out + lax.dot_general(a2, b2, dn, preferred_element_type=F32)
    out = out + lax.dot_general(a2, b1, dn, preferred_element_type=F32)
    out = out + lax.dot_general(a1, b2, dn, preferred_element_type=F32)
    return out + lax.dot_general(a1, b1, dn, preferred_element_type=F32)


def _dot_f32_3pass(a, b, dn):
    a1 = a.astype(BF16)
    a2 = (a - a1.astype(F32)).astype(BF16)
    b1 = b.astype(BF16)
    b2 = (b - b1.astype(F32)).astype(BF16)
    out = lax.dot_general(a2, b1, dn, preferred_element_type=F32)
    out = out + lax.dot_general(a1, b2, dn, preferred_element_type=F32)
    return out + lax.dot_general(a1, b1, dn, preferred_element_type=F32)


def _scan_tri(n, rev):
    r = lax.broadcasted_iota(jnp.int32, (n, n), 0)
    c = lax.broadcasted_iota(jnp.int32, (n, n), 1)
    return ((c >= r) if rev else (c <= r))


def _mod_kernel(c_ref, w_ref, b_ref, o_ref):
    cond = _silu(c_ref[...])
    o_ref[0] = _dot_f32(cond, w_ref[0], _NN) + b_ref[0]


def _modulation(cond, mod_w, mod_b):
    depth, d, n = mod_w.shape
    r = cond.shape[0]
    tn = 512
    return pl.pallas_call(
        _mod_kernel,
        grid=(depth, n // tn),
        in_specs=[pl.BlockSpec((r, d), lambda l, j: (0, 0)),
                  pl.BlockSpec((1, d, tn), lambda l, j: (l, 0, j)),
                  pl.BlockSpec((1, 1, tn), lambda l, j: (l, 0, j))],
        out_specs=pl.BlockSpec((1, r, tn), lambda l, j: (l, 0, j)),
        out_shape=jax.ShapeDtypeStruct((depth, r, n), F32),
        compiler_params=_cparams(("parallel", "parallel")),
        name="modulation",
    )(cond, mod_w, mod_b.reshape(depth, 1, n))


def _normmod(x, g, shift, scale):
    ms = jnp.mean(x * x, axis=-1, keepdims=True)
    return (x * lax.rsqrt(ms + EPS) * g) * (1.0 + scale) + shift


def _normmod_kernel(x_ref, g_ref, sh_ref, sc_ref, o_ref):
    o_ref[0] = _normmod(x_ref[0], g_ref[...], sh_ref[0], sc_ref[0]).astype(o_ref.dtype)


def _mod_spec(nb):
    return pl.BlockSpec((1, 1, D_MODEL), (lambda b, *_: (b, 0, 0)) if nb > 1 else (lambda b, *_: (0, 0, 0)))


def _norm_mod(x, g, shift, scale):
    bsz, seq, d = x.shape
    tl = min(seq, 512)
    return pl.pallas_call(
        _normmod_kernel,
        grid=(bsz, seq // tl),
        in_specs=[pl.BlockSpec((1, tl, d), lambda b, i: (b, i, 0)),
                  pl.BlockSpec((1, d), lambda b, i: (0, 0)),
                  _mod_spec(shift.shape[0]), _mod_spec(scale.shape[0])],
        out_specs=pl.BlockSpec((1, tl, d), lambda b, i: (b, i, 0)),
        out_shape=jax.ShapeDtypeStruct((bsz, seq, d), BF16),
        compiler_params=_cparams(("parallel", "parallel")),
        name="norm_mod",
    )(x, g.reshape(1, d), shift, scale)


def _normmod_cm_kernel(x_ref, g_ref, sh_ref, sc_ref, o_ref, h_ref, *, rows):
    h = _normmod(x_ref[0], g_ref[...], sh_ref[0], sc_ref[0])
    nlt = h.shape[-1] // LANES
    for r in range(rows):
        for j in range(nlt):
            h_ref[j, r * CM_COLS:(r + 1) * CM_COLS, :] = h[r, :, j * LANES:(j + 1) * LANES]
    for c in range(CM_COLS):
        for j in range(nlt):
            piece = h_ref[j, pl.ds(c, rows, stride=CM_COLS), :]
            o_ref[0, c * rows:(c + 1) * rows, j * LANES:(j + 1) * LANES] = piece.astype(o_ref.dtype)


def _norm_mod_colmajor(x, g, shift, scale):
    bsz, seq, d = x.shape
    rows = seq // GRID_W
    x4 = x.reshape(bsz, rows, GRID_W, d)
    return pl.pallas_call(
        functools.partial(_normmod_cm_kernel, rows=rows),
        grid=(bsz, GRID_W // CM_COLS),
        in_specs=[pl.BlockSpec((1, rows, CM_COLS, d), lambda b, i: (b, 0, i, 0)),
                  pl.BlockSpec((1, d), lambda b, i: (0, 0)),
                  _mod_spec(shift.shape[0]), _mod_spec(scale.shape[0])],
        out_specs=pl.BlockSpec((1, CM_COLS * rows, d), lambda b, i: (b, i, 0)),
        out_shape=jax.ShapeDtypeStruct((bsz, seq, d), BF16),
        scratch_shapes=[pltpu.VMEM((d // LANES, CM_COLS * rows, LANES), F32)],
        compiler_params=_cparams(("parallel", "parallel")),
        name="norm_mod_colmajor",
    )(x4, g.reshape(1, d), shift, scale)


MM_SEMANTICS = ("arbitrary", "arbitrary", "arbitrary")


def _stage_weights(b_ref, wb_ref):
    @pl.when((pl.program_id(1) == 0) & (pl.program_id(2) == 0))
    def _():
        wb_ref[...] = b_ref[...].astype(BF16)


def _mm_kernel(a_ref, b_ref, o_ref, wb_ref):
    _stage_weights(b_ref, wb_ref)
    o_ref[0] = jnp.dot(a_ref[0], wb_ref[...], preferred_element_type=F32).astype(o_ref.dtype)


def _mm_res2_kernel(a1_ref, a2_ref, b1_ref, b2_ref, res_ref, gate_ref, o_ref, wb1_ref, wb2_ref):
    _stage_weights(b1_ref, wb1_ref)
    _stage_weights(b2_ref, wb2_ref)
    y = jnp.dot(a1_ref[0], wb1_ref[...], preferred_element_type=F32)
    y = y + jnp.dot(a2_ref[0], wb2_ref[...], preferred_element_type=F32)
    o_ref[0] = res_ref[0] + gate_ref[0] * y


def _mm_res_kernel(a_ref, b_ref, res_ref, gate_ref, o_ref, wb_ref):
    _stage_weights(b_ref, wb_ref)
    o_ref[0] = res_ref[0] + gate_ref[0] * jnp.dot(a_ref[0], wb_ref[...], preferred_element_type=F32)


def _matmul(a, w, out_dtype, col0=0, ncols=None):
    bsz, seq, k = a.shape
    n = w.shape[1] - col0 if ncols is None else ncols
    tm, tn = min(seq, 1024), min(n, 1024)
    assert col0 % tn == 0 and n % tn == 0
    jb = col0 // tn
    return pl.pallas_call(
        _mm_kernel,
        grid=(n // tn, bsz, seq // tm),
        in_specs=[pl.BlockSpec((1, tm, k), lambda j, b, i: (b, i, 0)),
                  pl.BlockSpec((k, tn), lambda j, b, i: (0, j + jb))],
        out_specs=pl.BlockSpec((1, tm, tn), lambda j, b, i: (b, i, j)),
        out_shape=jax.ShapeDtypeStruct((bsz, seq, n), out_dtype),
        scratch_shapes=[pltpu.VMEM((k, tn), BF16)],
        compiler_params=_cparams(MM_SEMANTICS),
        name="matmul",
    )(a, w)


def _matmul_res2(a1, a2, w, res, gate):
    bsz, seq, kh = a1.shape
    n = w.shape[1]
    tm, tn = min(seq, 1024), min(n, 512)
    nb = gate.shape[0]
    return pl.pallas_call(
        _mm_res2_kernel,
        grid=(n // tn, bsz, seq // tm),
        in_specs=[pl.BlockSpec((1, tm, kh), lambda j, b, i: (b, i, 0)),
                  pl.BlockSpec((1, tm, kh), lambda j, b, i: (b, i, 0)),
                  pl.BlockSpec((kh, tn), lambda j, b, i: (0, j)),
                  pl.BlockSpec((kh, tn), lambda j, b, i: (1, j)),
                  pl.BlockSpec((1, tm, tn), lambda j, b, i: (b, i, j)),
                  pl.BlockSpec((1, 1, tn), (lambda j, b, i: (b, 0, j)) if nb > 1 else (lambda j, b, i: (0, 0, j)))],
        out_specs=pl.BlockSpec((1, tm, tn), lambda j, b, i: (b, i, j)),
        out_shape=jax.ShapeDtypeStruct((bsz, seq, n), F32),
        scratch_shapes=[pltpu.VMEM((kh, tn), BF16), pltpu.VMEM((kh, tn), BF16)],
        compiler_params=_cparams(MM_SEMANTICS),
        name="matmul_res",
    )(a1, a2, w, w, res, gate)


def _realign_kernel(w_ref, o_ref, *, col0):
    o_ref[...] = w_ref[:, col0:col0 + o_ref.shape[1]].astype(o_ref.dtype)


def _realign_cols(w, col0, ncols):
    k, n = w.shape
    tk = 128
    return pl.pallas_call(
        functools.partial(_realign_kernel, col0=col0),
        grid=(k // tk,),
        in_specs=[pl.BlockSpec((tk, n), lambda i: (i, 0))],
        out_specs=pl.BlockSpec((tk, ncols), lambda i: (i, 0)),
        out_shape=jax.ShapeDtypeStruct((k, ncols), BF16),
        compiler_params=_cparams(("parallel",)),
        name="realign_cols",
    )(w)


def _matmul_res(a, w, res, gate):
    bsz, seq, k = a.shape
    n = w.shape[1]
    tm, tn = min(seq, 1024), min(n, 512)
    return pl.pallas_call(
        _mm_res_kernel,
        grid=(n // tn, bsz, seq // tm),
        in_specs=[pl.BlockSpec((1, tm, k), lambda j, b, i: (b, i, 0)),
                  pl.BlockSpec((k, tn), lambda j, b, i: (0, j)),
                  pl.BlockSpec((1, tm, tn), lambda j, b, i: (b, i, j)),
                  pl.BlockSpec((1, 1, tn), lambda j, b, i: (b, 0, j))],
        out_specs=pl.BlockSpec((1, tm, tn), lambda j, b, i: (b, i, j)),
        out_shape=jax.ShapeDtypeStruct((bsz, seq, n), F32),
        scratch_shapes=[pltpu.VMEM((k, tn), BF16)],
        compiler_params=_cparams(MM_SEMANTICS),
        name="matmul_res1",
    )(a, w, res, gate)


def _ssd_kernel(*refs, rev, nsteps, readout):
    if readout:
        act_ref, dt_ref, dtb_ref, alog_ref, s0_ref, yin_ref, z_ref, ng_ref, y_ref, sfin_ref, st_ref = refs
    else:
        (xbc_ref, prev_ref, next_ref, dt_ref, cw_ref, cb_ref, dtb_ref, alog_ref, dsk_ref, s0_ref,
         y_ref, act_ref, sfin_ref, pad_ref, st_ref) = refs
    step = pl.program_id(1)
    u = (nsteps - 1 - step) if rev else step
    ck = SSD_CHUNK
    tstep = dt_ref.shape[1]
    per_step = tstep // ck
    hoff = SSD_HEADS if rev else 0

    @pl.when(step == 0)
    def _():
        st_ref[...] = s0_ref[0]

    if not readout:
        pad_ref[HALO:HALO + tstep, :] = xbc_ref[0].astype(F32)
        pad_ref[0:HALO, :] = jnp.where(u > 0, prev_ref[0].astype(F32), 0.0)
        pad_ref[HALO + tstep:2 * HALO + tstep, :] = jnp.where(u < nsteps - 1, next_ref[0].astype(F32), 0.0)

    hlane = lax.broadcasted_iota(jnp.int32, (ck, LANES), 1)
    is_head = (hlane >= hoff) & (hlane < hoff + SSD_HEADS)
    neg_a = -jnp.exp(alog_ref[...])
    tri = _scan_tri(ck, rev).astype(BF16)
    last = 0 if rev else ck - 1
    hrow = lax.broadcasted_iota(jnp.int32, (LANES, SSD_INNER), 0)
    hcol = lax.broadcasted_iota(jnp.int32, (LANES, SSD_INNER), 1)
    expand = (hcol // SSD_HEAD_DIM + hoff == hrow).astype(BF16)
    r = lax.broadcasted_iota(jnp.int32, (ck, ck), 0)
    c = lax.broadcasted_iota(jnp.int32, (ck, ck), 1)
    causal = (c >= r) if rev else (c <= r)
    lane = lax.broadcasted_iota(jnp.int32, (ck, LANES), 1)
    groups = range(SSD_GROUPS)
    gw = [slice(g * SSD_GROUP_W, (g + 1) * SSD_GROUP_W) for g in groups]
    heads_per_group = SSD_HEADS // SSD_GROUPS

    def prepare(ci):
        rows = slice(ci * ck, (ci + 1) * ck)
        if readout:
            xbc = act_ref[0, rows, :].astype(F32)
        else:
            half = SSD_CONV // 2
            acc = cb_ref[...] + cw_ref[0:1, :] * pad_ref[pl.ds(ci * ck + HALO - half, ck), :]
            for k in range(1, SSD_CONV):
                acc = acc + cw_ref[k:k + 1, :] * pad_ref[pl.ds(ci * ck + HALO - half + k, ck), :]
            xbc = _silu(acc)
            act_ref[0, rows, :] = xbc.astype(act_ref.dtype)
        xs = xbc[:, :SSD_INNER]
        dt = jnp.where(is_head, jax.nn.softplus(dt_ref[0, rows, :] + dtb_ref[...]), 0.0)
        acum = _dot01(tri, dt * neg_a)
        total = acum[last:last + 1, :]
        acum_t = acum.T
        dt_x = _dot_v01(dt, expand)
        te_x = _dot_v01(jnp.exp(total - acum), expand)
        xdt = xs * dt_x
        xdt_b = xdt.astype(BF16)
        p = dict(xs=xs, ea_x=_dot_v01(jnp.exp(acum), expand), xw_b=(xdt * te_x).astype(BF16),
                 cd_x=_dot_v01(jnp.broadcast_to(jnp.exp(total), (SUBLANES, LANES)), expand)[0:1, :],
                 b_g=[xbc[:, SSD_INNER + g * SSD_STATE:SSD_INNER + (g + 1) * SSD_STATE].astype(BF16) for g in groups],
                 c_g=[xbc[:, SSD_INNER + SSD_BC + g * SSD_STATE:SSD_INNER + SSD_BC + (g + 1) * SSD_STATE].astype(BF16)
                      for g in groups],
                 decay=[], masked_x=[])
        for hd in range(SSD_HEADS):
            tile, half_i = hd // 2, hd % 2
            h = hoff + hd
            seg = acum[:, h:h + 1] - acum_t[h:h + 1, :]
            p["decay"].append(jnp.where(causal, jnp.exp(seg), 0.0))
            xpair = xdt_b[:, tile * LANES:(tile + 1) * LANES]
            in_half = (lane >= half_i * SSD_HEAD_DIM) & (lane < (half_i + 1) * SSD_HEAD_DIM)
            p["masked_x"].append(jnp.where(in_half, xpair, jnp.zeros_like(xpair)))
        return p

    order = list(range(per_step - 1, -1, -1) if rev else range(per_step))
    ready = prepare(order[0])
    for n, ci in enumerate(order):
        rows = slice(ci * ck, (ci + 1) * ck)
        p = ready
        if n + 1 < len(order):
            ready = prepare(order[n + 1])
        b_g, c_g = p["b_g"], p["c_g"]
        s_in = [st_ref[g] for g in groups]
        cb = [lax.dot_general(c_g[g], b_g[g], _NT, preferred_element_type=F32) for g in groups]
        y_off = [jnp.dot(c_g[g], s_in[g].astype(BF16), preferred_element_type=F32) for g in groups]
        chunk_state = [lax.dot_general(b_g[g], p["xw_b"][:, gw[g]], _TN, preferred_element_type=F32) for g in groups]
        for g in groups:
            st_ref[g] = p["cd_x"][:, gw[g]] * s_in[g] + chunk_state[g]
        decay_cb = [(cb[hd // heads_per_group] * p["decay"][hd]).astype(BF16) for hd in range(SSD_HEADS)]
        y_diag = [jnp.dot(decay_cb[hd], p["masked_x"][hd], preferred_element_type=F32) for hd in range(SSD_HEADS)]
        y_parts = []
        for tile in range(SSD_HEADS // 2):
            g, within = (2 * tile) // heads_per_group, tile % (heads_per_group // 2)
            y_tile = y_off[g][:, within * LANES:(within + 1) * LANES] * p["ea_x"][:, tile * LANES:(tile + 1) * LANES]
            y_parts.append(y_tile + y_diag[2 * tile] + y_diag[2 * tile + 1])
        y = jnp.concatenate(y_parts, axis=1)

        if readout:
            y = y + yin_ref[0, rows, :].astype(F32)
            y = y * _silu(z_ref[0, rows, :].astype(F32))
            ms = jnp.mean(y * y, axis=-1, keepdims=True)
            y_ref[0, rows, :] = (y * lax.rsqrt(ms + EPS) * ng_ref[...]).astype(y_ref.dtype)
        else:
            y_ref[0, rows, :] = (y + dsk_ref[...] * p["xs"]).astype(y_ref.dtype)

    @pl.when(step == nsteps - 1)
    def _():
        sfin_ref[0] = st_ref[...]


def _ssd_scan(xbc, dt_raw, dt_bias, a_log, s0, rev, conv_w=None, conv_b=None, d_skip=None,
              y_in=None, z=None, norm_g=None):
    bsz, seq, _ = xbc.shape
    tstep = min(seq, SSD_STEP)
    nsteps = seq // tstep
    hb = tstep // HALO
    nhalo = seq // HALO
    hoff = SSD_HEADS if rev else 0

    def cidx(i):
        return (nsteps - 1 - i) if rev else i

    def head_lanes(v):
        return jnp.pad(v.reshape(1, -1), ((0, 0), (hoff, LANES - hoff - v.shape[-1])))

    chunk = lambda width: pl.BlockSpec((1, tstep, width), lambda b, i: (b, cidx(i), 0))
    row = lambda width: pl.BlockSpec((1, width), lambda b, i: (0, 0))
    state_spec = pl.BlockSpec((1, SSD_GROUPS, SSD_STATE, SSD_GROUP_W), lambda b, i: (b, 0, 0, 0))
    state_shape = jax.ShapeDtypeStruct((bsz, SSD_GROUPS, SSD_STATE, SSD_GROUP_W), F32)
    y_shape = jax.ShapeDtypeStruct((bsz, seq, SSD_INNER), BF16)
    state_scratch = pltpu.VMEM((SSD_GROUPS, SSD_STATE, SSD_GROUP_W), F32)
    if rev:
        in_specs = [chunk(SSD_CONV_DIM), chunk(LANES), row(LANES), row(LANES), state_spec,
                    chunk(SSD_INNER), chunk(SSD_INNER), row(SSD_INNER)]
        args = [xbc, dt_raw, head_lanes(dt_bias), head_lanes(a_log), s0, y_in, z, norm_g.reshape(1, -1)]
        out_specs, out_shape = [chunk(SSD_INNER), state_spec], [y_shape, state_shape]
        scratch = [state_scratch]
    else:
        in_specs = [
            chunk(SSD_CONV_DIM),
            pl.BlockSpec((1, HALO, SSD_CONV_DIM), lambda b, i: (b, jnp.maximum(i * hb - 1, 0), 0)),
            pl.BlockSpec((1, HALO, SSD_CONV_DIM), lambda b, i: (b, jnp.minimum((i + 1) * hb, nhalo - 1), 0)),
            chunk(LANES),
            pl.BlockSpec((SUBLANES, SSD_CONV_DIM), lambda b, i: (0, 0)), row(SSD_CONV_DIM),
            row(LANES), row(LANES), row(SSD_INNER), state_spec]
        args = [xbc, xbc, xbc, dt_raw,
                jnp.pad(conv_w, ((0, SUBLANES - SSD_CONV), (0, 0))), conv_b.reshape(1, -1),
                head_lanes(dt_bias), head_lanes(a_log), jnp.repeat(d_skip, SSD_HEAD_DIM).reshape(1, -1), s0]
        out_specs = [chunk(SSD_INNER), chunk(SSD_CONV_DIM), state_spec]
        out_shape = [y_shape, jax.ShapeDtypeStruct((bsz, seq, SSD_CONV_DIM), BF16), state_shape]
        scratch = [pltpu.VMEM((tstep + 2 * HALO, SSD_CONV_DIM), F32), state_scratch]
    return pl.pallas_call(
        functools.partial(_ssd_kernel, rev=rev, nsteps=nsteps, readout=rev),
        grid=(bsz, nsteps),
        in_specs=in_specs, out_specs=out_specs, out_shape=out_shape, scratch_shapes=scratch,
        compiler_params=_cparams(("parallel", "arbitrary")),
        name="ssd_scan_rev" if rev else "ssd_scan_fwd",
    )(*args)


def _conformer_kernel(p_ref, w_ref, b_ref, g_ref, beta_ref, o_ref, pad_ref, rot_ref, conv_ref, *, nseq, slen):
    v = p_ref[0, :, :CF_CH].astype(F32)
    gate = p_ref[0, :, CF_CH:].astype(F32)
    u = v * jax.nn.sigmoid(gate)
    first = CF_PAD - CF_KERNEL // 2
    span = rot_ref.shape[2]
    for s in range(nseq):
        pad_ref[s, 0:CF_PAD, :] = jnp.zeros((CF_PAD, CF_CH), F32)
        pad_ref[s, CF_PAD + slen:2 * CF_PAD + slen, :] = jnp.zeros((CF_PAD, CF_CH), F32)
        pad_ref[s, CF_PAD:CF_PAD + slen, :] = u[s * slen:(s + 1) * slen, :]
        for res in range(1, SUBLANES):
            rot_ref[res - 1, s] = pad_ref[s, pl.ds(res, span), :]
    for s in range(nseq):
        for cb in range(CF_CH // LANES):
            ch = slice(cb * LANES, (cb + 1) * LANES)
            acc = jnp.broadcast_to(b_ref[:, ch], (slen, LANES))
            for k in range(CF_KERNEL):
                res, lead = (first + k) % SUBLANES, (first + k) // SUBLANES * SUBLANES
                src = pad_ref[s, pl.ds(lead, slen), ch] if res == 0 else rot_ref[res - 1, s, pl.ds(lead, slen), ch]
                acc = acc + w_ref[k:k + 1, ch] * src
            conv_ref[s * slen:(s + 1) * slen, ch] = acc
    y = conv_ref[...]
    mu = jnp.mean(y, axis=-1, keepdims=True)
    xc = y - mu
    var = jnp.mean(xc * xc, axis=-1, keepdims=True)
    y = xc * lax.rsqrt(var + EPS) * g_ref[...] + beta_ref[...]
    o_ref[0] = _silu(y).astype(o_ref.dtype)


def _conformer(p_cf, dw_w, dw_b, ln_g, ln_b, slen):
    bsz, seq, _ = p_cf.shape
    nseq = max(1, min(seq, 256) // slen)
    tb = nseq * slen
    kpad = -(-CF_KERNEL // SUBLANES) * SUBLANES
    last_tap_row = CF_PAD - CF_KERNEL // 2 + CF_KERNEL - 1
    span = slen + last_tap_row // SUBLANES * SUBLANES
    return pl.pallas_call(
        functools.partial(_conformer_kernel, nseq=nseq, slen=slen),
        grid=(bsz, seq // tb),
        in_specs=[pl.BlockSpec((1, tb, 2 * CF_CH), lambda b, i: (b, i, 0)),
                  pl.BlockSpec((kpad, CF_CH), lambda b, i: (0, 0)),
                  pl.BlockSpec((1, CF_CH), lambda b, i: (0, 0)),
                  pl.BlockSpec((1, CF_CH), lambda b, i: (0, 0)),
                  pl.BlockSpec((1, CF_CH), lambda b, i: (0, 0))],
        out_specs=pl.BlockSpec((1, tb, CF_CH), lambda b, i: (b, i, 0)),
        out_shape=jax.ShapeDtypeStruct((bsz, seq, CF_CH), BF16),
        scratch_shapes=[pltpu.VMEM((nseq, slen + 2 * CF_PAD, CF_CH), F32),
                        pltpu.VMEM((SUBLANES - 1, nseq, span, CF_CH), F32),
                        pltpu.VMEM((tb, CF_CH), F32)],
        compiler_params=_cparams(("parallel", "parallel")),
        name="conformer_conv",
    )(p_cf, jnp.pad(dw_w, ((0, kpad - CF_KERNEL), (0, 0))), dw_b.reshape(1, -1), ln_g.reshape(1, -1),
      ln_b.reshape(1, -1))


def _hgrn2_kernel(*refs, rev, mode):
    if mode == "state":
        v_ref, f_ref, lb_ref, s0_ref, sfin_ref, st_ref = refs
    elif mode == "out":
        q_ref, v_ref, f_ref, lb_ref, s0_ref, o_ref, st_ref = refs
    else:
        q_ref, v_ref, f_ref, lb_ref, s0_ref, oin_ref, g_ref, ng_ref, perm_ref, o_ref, st_ref, osc_ref = refs
    step = pl.program_id(1)
    ck = HG_CHUNK
    per_step = v_ref.shape[1] // ck

    @pl.when(step == 0)
    def _():
        st_ref[...] = s0_ref[0]

    lb = lb_ref[...]
    tri = _scan_tri(ck, rev).astype(BF16)
    last = 0 if rev else ck - 1
    mid_pos = HG_CHUNK // 2 - 1
    mid = (ck - 1 - mid_pos) if rev else mid_pos
    r = lax.broadcasted_iota(jnp.int32, (ck, ck), 0)
    c = lax.broadcasted_iota(jnp.int32, (ck, ck), 1)
    causal = (c >= r) if rev else (c <= r)

    heads = range(HG_HEADS)
    ks = [slice(h * HG_DK, (h + 1) * HG_DK) for h in heads]
    vs = [slice(h * HG_DV, (h + 1) * HG_DV) for h in heads]

    def prepare(ci):
        rows = slice(ci * ck, (ci + 1) * ck)
        f = lb + (1.0 - lb) * jax.nn.sigmoid(f_ref[0, rows, :])
        kk = 1.0 - f
        gcum = _dot01(tri, jnp.log(f))
        g_end = gcum[last:last + 1, :]
        p = dict(k_end=(kk * jnp.exp(g_end - gcum)).astype(BF16), dec_end=jnp.exp(g_end))
        if mode != "state":
            g_mid = gcum[mid:mid + 1, :]
            q = _silu(q_ref[0, rows, :].astype(F32))
            p.update(q_rel=(q * jnp.exp(gcum - g_mid)).astype(BF16), k_rel=(kk * jnp.exp(g_mid - gcum)).astype(BF16),
                     q_dec=(q * jnp.exp(gcum)).astype(BF16))
        if mode == "readout":
            p.update(gate=_silu(g_ref[0, rows, :].astype(F32)))
        return p

    order = list(range(per_step - 1, -1, -1) if rev else range(per_step))
    ready = prepare(order[0])
    for n, ci in enumerate(order):
        rows = slice(ci * ck, (ci + 1) * ck)
        p = ready
        k_end, dec_end = p["k_end"], p["dec_end"]
        v = v_ref[0, rows, :]
        if mode != "state":
            q_rel, k_rel, q_dec = p["q_rel"], p["k_rel"], p["q_dec"]
        if mode == "readout":
            gate = p["gate"]
        if n + 1 < len(order):
            ready = prepare(order[n + 1])

        s_in = [st_ref[h] for h in heads]
        if mode != "state":
            att = [lax.dot_general(q_rel[:, ks[h]], k_rel[:, ks[h]], _NT, preferred_element_type=F32) for h in heads]
            o_inter = [lax.dot_general(q_dec[:, ks[h]], s_in[h].astype(BF16), _NT, preferred_element_type=F32)
                       for h in heads]
            att = [jnp.where(causal, a, 0.0).astype(BF16) for a in att]
            o = [jnp.dot(att[h], v[:, vs[h]], preferred_element_type=F32) + o_inter[h] for h in heads]
        chunk_state = [lax.dot_general(v[:, vs[h]], k_end[:, ks[h]], _TN, preferred_element_type=F32) for h in heads]
        for h in heads:
            st_ref[h] = s_in[h] * dec_end[:, ks[h]] + chunk_state[h]
        if mode == "readout":
            for h in heads:
                o_h = o[h] + oin_ref[0, rows, vs[h]].astype(F32)
                ms = jnp.mean(o_h * o_h, axis=-1, keepdims=True)
                o_h = (o_h * lax.rsqrt(ms + EPS) * ng_ref[...]) * gate[:, vs[h]]
                osc_ref[rows, vs[h]] = o_h.astype(osc_ref.dtype)
        elif mode == "out":
            for h in heads:
                o_ref[0, rows, vs[h]] = o[h].astype(o_ref.dtype)

    if mode == "readout":
        nat = jnp.dot(perm_ref[...], osc_ref[...], preferred_element_type=F32).astype(o_ref.dtype)
        ncols = o_ref.shape[2]
        for gr in range(o_ref.shape[1]):
            o_ref[0, gr] = nat[gr * ncols:(gr + 1) * ncols, :]
    if mode == "state":
        @pl.when(step == pl.num_programs(1) - 1)
        def _():
            sfin_ref[0] = st_ref[...]


def _hgrn2_scan(v, f_raw, lb, s0, rev, mode, q=None, o_in=None, g=None, norm_g=None):
    bsz, seq, _ = v.shape
    tstep = min(seq, HG_STEP)
    nsteps = seq // tstep
    dcol = 1 if rev else 0

    def cidx(i):
        return (nsteps - 1 - i) if rev else i

    tok = lambda width: pl.BlockSpec((1, tstep, width), lambda b, i: (b, cidx(i), 0))
    state_spec = pl.BlockSpec((1, HG_HEADS, HG_DV, HG_DK), lambda b, i: (b, 0, 0, 0))
    f_spec = pl.BlockSpec((1, tstep, HG_KEY), lambda b, i: (b, cidx(i), dcol))
    lb_spec = pl.BlockSpec((1, HG_KEY), lambda b, i: (0, 0))
    state_shape = jax.ShapeDtypeStruct((bsz, HG_HEADS, HG_DV, HG_DK), F32)
    scratch = [pltpu.VMEM((HG_HEADS, HG_DV, HG_DK), F32)]
    if mode == "state":
        in_specs, args = [tok(HG_VAL), f_spec, lb_spec, state_spec], [v, f_raw, lb.reshape(1, -1), s0]
        out_specs, out_shape = state_spec, state_shape
    else:
        in_specs = [tok(HG_KEY), tok(HG_VAL), f_spec, lb_spec, state_spec]
        args = [q, v, f_raw, lb.reshape(1, -1), s0]
        out_specs, out_shape = tok(HG_VAL), jax.ShapeDtypeStruct((bsz, seq, HG_VAL), BF16)
        if mode == "readout":
            grid_rows = seq // GRID_W
            ncols = tstep // grid_rows
            tok_id = jnp.arange(tstep)
            src = (tok_id % ncols) * grid_rows + tok_id // ncols
            perm = (src[:, None] == tok_id[None, :]).astype(BF16)
            in_specs += [tok(HG_VAL), tok(HG_VAL), pl.BlockSpec((1, HG_DV), lambda b, i: (0, 0)),
                         pl.BlockSpec((tstep, tstep), lambda b, i: (0, 0))]
            args += [o_in, g, norm_g.reshape(1, -1), perm]
            out_specs = pl.BlockSpec((1, grid_rows, ncols, HG_VAL), lambda b, i: (b, 0, cidx(i), 0))
            out_shape = jax.ShapeDtypeStruct((bsz, grid_rows, GRID_W, HG_VAL), BF16)
            scratch.append(pltpu.VMEM((tstep, HG_VAL), BF16))
    out = pl.pallas_call(
        functools.partial(_hgrn2_kernel, rev=rev, mode=mode),
        grid=(bsz, nsteps),
        in_specs=in_specs, out_specs=out_specs, out_shape=out_shape,
        scratch_shapes=scratch,
        compiler_params=_cparams(("parallel", "arbitrary")),
        name=f"hgrn2_{mode}_{'rev' if rev else 'fwd'}",
    )(*args)
    return out.reshape(bsz, seq, HG_VAL) if mode == "readout" else out


def _first_argmax(vals):
    best, idx = vals[0], jnp.zeros(vals[0].shape, jnp.int32)
    for j in range(1, len(vals)):
        better = vals[j] > best
        idx = jnp.where(better, j, idx)
        best = jnp.where(better, vals[j], best)
    return idx, best


def _pick(idx, vals):
    out = vals[-1]
    for j in range(len(vals) - 2, -1, -1):
        out = jnp.where(idx == j, vals[j], out)
    return out


def _store_token_tiles(ref, v):
    n = v.shape[0]
    for j in range(SUBLANES):
        ref[pl.ds(j, n, stride=SUBLANES), :] = v[:, j * LANES:(j + 1) * LANES]


def _load_token_tiles(ref):
    n = ref.shape[0] // SUBLANES
    return jnp.concatenate([ref[pl.ds(j, n, stride=SUBLANES), :] for j in range(SUBLANES)], axis=1)


def _token_tile(ref, idx):
    return ref.at[pl.ds(pl.multiple_of(idx * SUBLANES, SUBLANES), SUBLANES)]


def _router_kernel(x_ref, g_ref, sh_ref, sc_ref, rw_ref, rb_ref, h_ref, slot_ref, wt_ref, cnt_ref, carry_ref):
    step = pl.program_id(0)

    @pl.when(step == 0)
    def _():
        carry_ref[...] = jnp.zeros_like(carry_ref)

    h = _normmod(x_ref[...], g_ref[...], sh_ref[0], sc_ref[0])
    _store_token_tiles(h_ref, h)
    scores = jax.nn.sigmoid(_dot_f32_3pass(rw_ref[...], h, _NT))
    sel = scores + rb_ref[...]
    srow = [sel[e:e + 1, :] for e in range(N_EXPERTS)]
    prow = [scores[e:e + 1, :] for e in range(N_EXPERTS)]
    gscore = []
    for gi in range(N_EXPERT_GROUPS):
        m = srow[gi * EXPERTS_PER_GROUP:(gi + 1) * EXPERTS_PER_GROUP]
        pair_sums = [m[i] + m[j] for i in range(EXPERTS_PER_GROUP) for j in range(i + 1, EXPERTS_PER_GROUP)]
        best = pair_sums[0]
        for p in pair_sums[1:]:
            best = jnp.maximum(best, p)
        gscore.append(best)
    gidx, _ = _first_argmax(gscore)
    in_sel = [_pick(gidx, [srow[gi * EXPERTS_PER_GROUP + j] for gi in range(N_EXPERT_GROUPS)])
              for j in range(EXPERTS_PER_GROUP)]
    in_p = [_pick(gidx, [prow[gi * EXPERTS_PER_GROUP + j] for gi in range(N_EXPERT_GROUPS)])
            for j in range(EXPERTS_PER_GROUP)]
    i1, _ = _first_argmax(in_sel)
    i2, _ = _first_argmax([jnp.where(i1 == j, -jnp.inf, in_sel[j]) for j in range(EXPERTS_PER_GROUP)])
    w1, w2 = _pick(i1, in_p), _pick(i2, in_p)
    den = w1 + w2
    e1, e2 = gidx * EXPERTS_PER_GROUP + i1, gidx * EXPERTS_PER_GROUP + i2
    tl = h.shape[0]
    erow = lax.broadcasted_iota(jnp.int32, (N_EXPERTS, tl), 0)
    oh1, oh2 = erow == e1, erow == e2
    cnt = (oh1 | oh2).astype(F32)
    r = lax.broadcasted_iota(jnp.int32, (tl, tl), 0)
    c = lax.broadcasted_iota(jnp.int32, (tl, tl), 1)
    before = jnp.dot(cnt.astype(BF16), (r < c).astype(BF16), preferred_element_type=F32)
    base = carry_ref[:, 0:1] + before
    rank1 = jnp.sum(jnp.where(oh1, base, 0.0), axis=0, keepdims=True).astype(jnp.int32)
    rank2 = jnp.sum(jnp.where(oh2, base, 0.0), axis=0, keepdims=True).astype(jnp.int32)
    carry_ref[...] = carry_ref[...] + jnp.sum(cnt, axis=1, keepdims=True)
    cnt_ref[...] = carry_ref[...].astype(jnp.int32)
    srow = lax.broadcasted_iota(jnp.int32, (SUBLANES, tl), 0)
    slot_ref[0] = jnp.where(srow == 0, e1, jnp.where(srow == 1, e2, jnp.where(srow == 2, rank1, rank2)))
    wrow = lax.broadcasted_iota(jnp.int32, (LANES, tl), 0)
    wt_ref[...] = jnp.where(wrow == 0, w1 / den, jnp.where(wrow == 1, w2 / den, 0.0)).T


def _ffn_norm_router(x, g, shift, scale, router_w, router_b, seq):
    t, d = x.shape
    tl = min(seq, MOE_TL)
    nblk = t // tl
    nb = shift.shape[0]
    mod_spec = pl.BlockSpec((1, 1, d), (lambda i: (i * tl // seq, 0, 0)) if nb > 1 else (lambda i: (0, 0, 0)))
    return pl.pallas_call(
        _router_kernel,
        grid=(nblk,),
        in_specs=[pl.BlockSpec((tl, d), lambda i: (i, 0)),
                  pl.BlockSpec((1, d), lambda i: (0, 0)),
                  mod_spec, mod_spec,
                  pl.BlockSpec((N_EXPERTS, d), lambda i: (0, 0)),
                  pl.BlockSpec((N_EXPERTS, 1), lambda i: (0, 0))],
        out_specs=[pl.BlockSpec((tl * SUBLANES, LANES), lambda i: (i, 0)),
                   pl.BlockSpec((1, SUBLANES, tl), lambda i: (i, 0, 0)),
                   pl.BlockSpec((tl, LANES), lambda i: (i, 0)),
                   pl.BlockSpec((N_EXPERTS, LANES), lambda i: (0, 0))],
        out_shape=[jax.ShapeDtypeStruct((t * SUBLANES, LANES), F32),
                   jax.ShapeDtypeStruct((nblk, SUBLANES, tl), jnp.int32),
                   jax.ShapeDtypeStruct((t, LANES), F32),
                   jax.ShapeDtypeStruct((N_EXPERTS, LANES), jnp.int32)],
        scratch_shapes=[pltpu.VMEM((N_EXPERTS, LANES), F32)],
        compiler_params=_cparams(("arbitrary",)),
        name="ffn_norm_router",
    )(x, g.reshape(1, d), shift, scale, router_w.T, router_b.reshape(N_EXPERTS, 1))


DMA_UNROLL = 8


def _dispatch_kernel(slot_ref, ends_ref, h_ref, hs_ref, zero_ref, sem, *, tl, max_tiles):
    @pl.when(pl.program_id(0) == 0)
    def _():
        zero_ref[...] = jnp.zeros_like(zero_ref)
        fills = []
        for e in range(N_EXPERTS):
            end = ends_ref[e]
            gap = (MOE_TM - (end & (MOE_TM - 1))) & (MOE_TM - 1)
            for bit in range(MOE_TM.bit_length() - 1):
                size = 1 << bit
                first_row = pl.multiple_of((end + (gap & (size - 1))) * SUBLANES, SUBLANES)
                fills.append(((gap & size) != 0,
                              pltpu.make_async_copy(zero_ref.at[pl.ds(0, size * SUBLANES)],
                                                    hs_ref.at[pl.ds(first_row, size * SUBLANES)], sem)))
        for n in range(N_EXPERTS):
            first_slot = ends_ref[N_EXPERTS] + n * MOE_TM
            first_row = pl.multiple_of(first_slot * SUBLANES, SUBLANES)
            fills.append((first_slot < max_tiles * MOE_TM,
                          pltpu.make_async_copy(zero_ref, hs_ref.at[pl.ds(first_row, MOE_TM * SUBLANES)], sem)))
        for cond, cp in fills:
            pl.when(cond)(cp.start)
        for cond, cp in fills:
            pl.when(cond)(cp.wait)

    def issue(t, carry):
        pltpu.make_async_copy(_token_tile(h_ref, t), _token_tile(hs_ref, slot_ref[0, 0, t]), sem).start()
        pltpu.make_async_copy(_token_tile(h_ref, t), _token_tile(hs_ref, slot_ref[0, 0, tl + t]), sem).start(priority=1)
        return carry

    lax.fori_loop(0, tl, issue, 0, unroll=DMA_UNROLL)
    for _ in range(2):
        pltpu.make_async_copy(h_ref, hs_ref.at[pl.ds(0, tl * SUBLANES)], sem).wait()


def _dispatch(h, slot2, ends, tl, max_tiles):
    t = h.shape[0] // SUBLANES
    nblk = t // tl
    return pl.pallas_call(
        functools.partial(_dispatch_kernel, tl=tl, max_tiles=max_tiles),
        grid=(nblk,),
        in_specs=[pl.BlockSpec((1, 1, 2 * tl), lambda i: (i, 0, 0), memory_space=pltpu.SMEM),
                  pl.BlockSpec(memory_space=pltpu.SMEM),
                  pl.BlockSpec((tl * SUBLANES, LANES), lambda i: (i, 0))],
        out_specs=pl.BlockSpec(memory_space=pl.ANY),
        out_shape=jax.ShapeDtypeStruct((max_tiles * MOE_TM * SUBLANES, LANES), F32),
        scratch_shapes=[pltpu.VMEM((MOE_TM * SUBLANES, LANES), F32), pltpu.SemaphoreType.DMA(())],
        compiler_params=_cparams(("arbitrary",)),
        name="moe_dispatch",
    )(slot2, ends, h)


def _expert_kernel(te_ref, tf_ref, nt_ref, hs_ref, wg_ref, wu_ref, wd_ref, ys_ref, wgb_ref, wub_ref, wdb_ref):
    n = pl.program_id(0)

    @pl.when(n >= nt_ref[0])
    def _():
        ys_ref[...] = jnp.zeros_like(ys_ref)

    @pl.when(n < nt_ref[0])
    def _():
        @pl.when(tf_ref[n] == 1)
        def _():
            wgb_ref[...] = wg_ref[0].astype(BF16)
            wub_ref[...] = wu_ref[0].astype(BF16)
            wdb_ref[...] = wd_ref[0].astype(BF16)

        h = _load_token_tiles(hs_ref).astype(BF16)
        a = jnp.dot(h, wgb_ref[...], preferred_element_type=F32)
        b = jnp.dot(h, wub_ref[...], preferred_element_type=F32)
        y = jnp.dot((_silu(a) * b).astype(BF16), wdb_ref[...], preferred_element_type=F32)
        _store_token_tiles(ys_ref, y)


def _experts(hs, tile_expert, tile_first, n_tiles, w_gate, w_up, w_down, layer, max_tiles):
    d = w_gate.shape[2]
    tile_spec = pl.BlockSpec((MOE_TM * SUBLANES, LANES), lambda n, te, tf, nt: (n, 0))
    grid_spec = pltpu.PrefetchScalarGridSpec(
        num_scalar_prefetch=3,
        grid=(max_tiles,),
        in_specs=[tile_spec,
                  pl.BlockSpec((None, 1, d, D_EXPERT), lambda n, te, tf, nt: (layer, te[n], 0, 0)),
                  pl.BlockSpec((None, 1, d, D_EXPERT), lambda n, te, tf, nt: (layer, te[n], 0, 0)),
                  pl.BlockSpec((None, 1, D_EXPERT, d), lambda n, te, tf, nt: (layer, te[n], 0, 0))],
        out_specs=tile_spec,
        scratch_shapes=[pltpu.VMEM((d, D_EXPERT), BF16), pltpu.VMEM((d, D_EXPERT), BF16),
                        pltpu.VMEM((D_EXPERT, d), BF16)])
    return pl.pallas_call(
        _expert_kernel,
        grid_spec=grid_spec,
        out_shape=jax.ShapeDtypeStruct((max_tiles * MOE_TM * SUBLANES, LANES), F32),
        compiler_params=_cparams(("arbitrary",)),
        name="moe_experts",
    )(tile_expert, tile_first, n_tiles, hs, w_gate, w_up, w_down)


def _combine_kernel(slot_ref, nslot_ref, wt_ref, res_ref, gate_ref, fg_ref, ys_ref, o_ref, y1_ref, y2_ref, sems, *,
                    tl, final):
    step, nsteps = pl.program_id(0), pl.num_programs(0)
    cur = step % 2

    def start_gather(sref, buf):
        def issue(t, carry):
            pltpu.make_async_copy(_token_tile(ys_ref, sref[0, 0, t]), _token_tile(y1_ref.at[buf], t),
                                  sems.at[buf]).start()
            pltpu.make_async_copy(_token_tile(ys_ref, sref[0, 0, tl + t]), _token_tile(y2_ref.at[buf], t),
                                  sems.at[buf]).start(priority=1)
            return carry

        lax.fori_loop(0, tl, issue, 0, unroll=DMA_UNROLL)

    @pl.when(step == 0)
    def _():
        start_gather(slot_ref, 0)

    @pl.when(step + 1 < nsteps)
    def _():
        start_gather(nslot_ref, 1 - cur)

    pltpu.make_async_copy(ys_ref.at[pl.ds(0, tl * SUBLANES)], y1_ref.at[cur], sems.at[cur]).wait()
    pltpu.make_async_copy(ys_ref.at[pl.ds(0, tl * SUBLANES)], y2_ref.at[cur], sems.at[cur]).wait()
    wt = wt_ref[...]
    moe = wt[:, 0:1] * _load_token_tiles(y1_ref.at[cur]) + wt[:, 1:2] * _load_token_tiles(y2_ref.at[cur])
    x = res_ref[...] + gate_ref[0] * moe
    if final:
        ms = jnp.mean(x * x, axis=-1, keepdims=True)
        x = x * lax.rsqrt(ms + EPS) * fg_ref[...]
    o_ref[...] = x


def _combine(ys, slot2, wt, res, gate, final_g, seq, tl):
    t, d = res.shape
    nblk = t // tl
    nb = gate.shape[0]
    final = final_g is not None
    fg = final_g.reshape(1, d) if final else jnp.ones((1, d), F32)
    return pl.pallas_call(
        functools.partial(_combine_kernel, tl=tl, final=final),
        grid=(nblk,),
        in_specs=[pl.BlockSpec((1, 1, 2 * tl), lambda i: (i, 0, 0), memory_space=pltpu.SMEM),
                  pl.BlockSpec((1, 1, 2 * tl), lambda i: (jnp.minimum(i + 1, nblk - 1), 0, 0),
                               memory_space=pltpu.SMEM),
                  pl.BlockSpec((tl, LANES), lambda i: (i, 0)),
                  pl.BlockSpec((tl, d), lambda i: (i, 0)),
                  pl.BlockSpec((1, 1, d), (lambda i: (i * tl // seq, 0, 0)) if nb > 1 else (lambda i: (0, 0, 0))),
                  pl.BlockSpec((1, d), lambda i: (0, 0)),
                  pl.BlockSpec(memory_space=pl.ANY)],
        out_specs=pl.BlockSpec((tl, d), lambda i: (i, 0)),
        out_shape=jax.ShapeDtypeStruct((t, d), F32),
        scratch_shapes=[pltpu.VMEM((2, tl * SUBLANES, LANES), F32), pltpu.VMEM((2, tl * SUBLANES, LANES), F32),
                        pltpu.SemaphoreType.DMA((2,))],
        compiler_params=_cparams(("arbitrary",)),
        name="moe_combine",
    )(slot2, slot2, wt, res, gate, fg, ys)


def _even_layer_mixer(xs, mods, norm_g, w_in, conv_w, conv_b, dt_bias, a_log, d_skip, ssd_g,
                      cf_w, cf_b, cf_lng, cf_lnb, w_out, rows):
    s_lo = SSD_INNER
    dt_lo = SSD_INNER + SSD_CONV_DIM
    cf_lo = dt_lo + 2 * SSD_HEADS
    w_cf = _realign_cols(w_in, cf_lo, 2 * CF_CH)
    bsz = xs[0].shape[0]
    zero_state = jnp.zeros((bsz, SSD_GROUPS, SSD_STATE, SSD_GROUP_W), F32)
    proj = []
    for x, m in zip(xs, mods):
        h = _norm_mod(x, norm_g, m[0], m[1])
        proj.append(dict(z=_matmul(h, w_in, BF16, 0, s_lo), xbc=_matmul(h, w_in, BF16, s_lo, SSD_CONV_DIM),
                         dt=_matmul(h, w_in, F32, dt_lo, LANES), cf=_matmul(h, w_cf, BF16)))
    y_fwd, act, o_ssd = [None, None], [None, None], [None, None]
    state = zero_state
    for si in range(2):
        p = proj[si]
        y_fwd[si], act[si], state = _ssd_scan(p["xbc"], p["dt"], dt_bias[0], a_log[0], state, False,
                                              conv_w=conv_w, conv_b=conv_b, d_skip=d_skip)
    state = zero_state
    for si in range(2):
        p = proj[si]
        o_ssd[si], state = _ssd_scan(act[si], p["dt"], dt_bias[1], a_log[1], state, True,
                                     y_in=y_fwd[si], z=p["z"], norm_g=ssd_g)
    outs = []
    for si, (x, m) in enumerate(zip(xs, mods)):
        slen = x.shape[1] if si == 0 else GRID_W
        o_cf = _conformer(proj[si]["cf"], cf_w, cf_b, cf_lng, cf_lnb, slen)
        outs.append(_matmul_res2(o_ssd[si], o_cf, w_out, x, m[2]))
    return outs


def _odd_layer_mixer(xc, xl, m_c, m_l, norm_g, w_in, lb, hg_g, w_out):
    st0 = HG_KEY + HG_VAL
    f0 = st0 + HG_VAL
    bsz = xl.shape[0]
    h_c = _norm_mod(xc, norm_g, m_c[0], m_c[1])
    h_l = _norm_mod_colmajor(xl, norm_g, m_l[0], m_l[1])
    v_c, f_c = _matmul(h_c, w_in, BF16, st0, HG_VAL), _matmul(h_c, w_in, F32, f0, 2 * HG_KEY)
    q_l, g_l = _matmul(h_l, w_in, BF16, 0, HG_KEY), _matmul(h_l, w_in, BF16, HG_KEY, HG_VAL)
    v_l, f_l = _matmul(h_l, w_in, BF16, st0, HG_VAL), _matmul(h_l, w_in, F32, f0, 2 * HG_KEY)
    zero_state = jnp.zeros((bsz, HG_HEADS, HG_DV, HG_DK), F32)
    s_f = _hgrn2_scan(v_c, f_c, lb, zero_state, False, "state")
    s_r = _hgrn2_scan(v_c, f_c, lb, zero_state, True, "state")
    o_f = _hgrn2_scan(v_l, f_l, lb, s_f, False, "out", q=q_l)
    o = _hgrn2_scan(v_l, f_l, lb, s_r, True, "readout", q=q_l, o_in=o_f, g=g_l, norm_g=hg_g)
    return _matmul_res(o, w_out, xl, m_l[2])


def _dispatch_layout(counts, slot, max_tiles, tl_move):
    tiles = (counts + MOE_TM - 1) // MOE_TM
    tile_ends = jnp.cumsum(tiles)
    tile_starts = tile_ends - tiles
    total = tile_ends[-1]
    starts = tile_starts * MOE_TM
    ends = jnp.concatenate([starts + counts, (total * MOE_TM).reshape(1)]).astype(jnp.int32)
    experts = jnp.arange(N_EXPERTS, dtype=jnp.int32)

    def lookup(table, idx):
        return jnp.sum(jnp.where(idx[..., None] == experts, table, 0), axis=-1)

    choice_slots = [(lookup(starts, slot[:, k]) + slot[:, 2 + k]).astype(jnp.int32) for k in range(2)]
    grouped = [jnp.concatenate([s.reshape(-1, tl) for s in choice_slots], axis=-1)[:, None, :] for tl in tl_move]
    n = jnp.arange(max_tiles, dtype=jnp.int32)
    nn = jnp.minimum(n, total - 1)
    expert = jnp.sum((nn[:, None] >= tile_ends[None, :]).astype(jnp.int32), axis=1)
    is_first = ((nn == lookup(tile_starts, expert)) & (n < total)).astype(jnp.int32)
    return grouped, ends, expert, is_first, total.reshape(1).astype(jnp.int32)


def _moe_block(x, m, norm_g, router_w, router_b, w_gate, w_up, w_down, layer, final_g=None):
    bsz, seq, d = x.shape
    t = bsz * seq
    xf = x.reshape(t, d)
    tl_d, tl_c = min(seq, MOE_TL_DISPATCH), min(seq, MOE_TL)
    h, slot, wt, cnt = _ffn_norm_router(xf, norm_g, m[3], m[4], router_w, router_b, seq)
    max_tiles = 2 * t // MOE_TM + N_EXPERTS
    (slot_d, slot_c), ends, tile_expert, tile_first, n_tiles = _dispatch_layout(cnt[:, 0], slot, max_tiles,
                                                                                (tl_d, tl_c))
    hs = _dispatch(h, slot_d, ends, tl_d, max_tiles)
    ys = _experts(hs, tile_expert, tile_first, n_tiles, w_gate, w_up, w_down, layer, max_tiles)
    return _combine(ys, slot_c, wt, xf, m[5], final_g, seq, tl_c).reshape(bsz, seq, d)


def kernel(x, c, ctx, c_ctx, mod_w, mod_b, norm_mix_g, norm_ffn_g, router_w, router_b, moe_w_gate, moe_w_up,
           moe_w_down, ab_w_in, ssd_conv_w, ssd_conv_b, ssd_dt_bias, ssd_a_log, ssd_d, ssd_norm_g, cf_dw_w,
           cf_dw_b, cf_ln_g, cf_ln_b, ab_w_out, hg_w_in, hg_lb, hg_norm_g, hg_w_out, final_norm_g):
    depth = mod_w.shape[0]
    assert depth == 2, "layer schedule below is written for one even and one odd layer"
    bsz, seq, d = x.shape
    rows = seq // GRID_W
    lb_all = jnp.cumsum(jax.nn.softmax(hg_lb.astype(F32), axis=0), axis=0)
    lb_all = lb_all - lb_all[0]

    nrow = -(-(bsz + 1) // SUBLANES) * SUBLANES
    cond = jnp.zeros((nrow, d), F32).at[:bsz].set(c).at[bsz].set(c_ctx)
    mod = _modulation(cond, mod_w, mod_b).reshape(depth, nrow, N_MOD, 1, d)

    def mods(l):
        m_l = [mod[l, :bsz, k] for k in range(N_MOD)]
        m_c = [mod[l, bsz:bsz + 1, k] for k in range(N_MOD)]
        return m_c, m_l

    m_c, m_l = mods(0)
    xc, xl = _even_layer_mixer((ctx, x), (m_c, m_l), norm_mix_g[0], ab_w_in[0], ssd_conv_w[0], ssd_conv_b[0],
                               ssd_dt_bias[0], ssd_a_log[0], ssd_d[0], ssd_norm_g[0], cf_dw_w[0], cf_dw_b[0],
                               cf_ln_g[0], cf_ln_b[0], ab_w_out[0], rows)
    xl = _moe_block(xl, m_l, norm_ffn_g[0], router_w, router_b, moe_w_gate, moe_w_up, moe_w_down, 0)
    xc = _moe_block(xc, m_c, norm_ffn_g[0], router_w, router_b, moe_w_gate, moe_w_up, moe_w_down, 0)

    m_c, m_l = mods(1)
    xl = _odd_layer_mixer(xc, xl, m_c, m_l, norm_mix_g[1], hg_w_in[0], lb_all[1], hg_norm_g[0], hg_w_out[0])
    return _moe_block(xl, m_l, norm_ffn_g[1], router_w, router_b, moe_w_gate, moe_w_up, moe_w_down, 1,
                      final_g=final_norm_g)
```
